```python
import math
import jax, jax.numpy as jnp
from jax import lax
import numpy as np


D_MODEL = 2048
BATCH = 2
SEQ = 4096
DEPTH = 1
DEC_BATCH = 32
DEC_SEQ = 1
PAST_LEN = 8192
PAGE_SIZE = 128

MIX_WIDTH = D_MODEL
SSM_WIDTH = MIX_WIDTH // 2
ATTN_WIDTH = MIX_WIDTH - SSM_WIDTH
SSM_GROUP = 16
N_SSM_GROUPS = SSM_WIDTH // SSM_GROUP
SSM_STATE = 64
HEAD_DIM = 128
N_HEADS = ATTN_WIDTH // HEAD_DIM
D_IN = SSM_WIDTH + 3 * ATTN_WIDTH
DILATED_BRANCHES = ((128, 1), (512, 4), (2048, 16))
MAX_WINDOW = 2048
ROPE_THETA = 10000.0
N_EXPERT_GROUPS = 4
EXPERTS_PER_GROUP = 8
N_EXPERTS = N_EXPERT_GROUPS * EXPERTS_PER_GROUP
TOP_K = 2
EXPERT_FF = 512
MOE_BLOCK = 64
NORM_EPS = 1e-6

kernel_name = 'hymba_s5_dilated_hmoe_step'

F32 = jnp.float32


def _rmsnorm(x, g):
    xf = x.astype(F32)
    y = xf * lax.rsqrt(jnp.mean(xf * xf, axis=-1, keepdims=True) + NORM_EPS) * g.astype(F32)
    return y.astype(x.dtype)


def _rope(x, pos):
    half = x.shape[-1] // 2
    inv = ROPE_THETA ** (-jnp.arange(half, dtype=F32) / half)
    ang = pos[:, None] * inv[None, :]
    cos = jnp.cos(ang)[None, :, None, :]
    sin = jnp.sin(ang)[None, :, None, :]
    xf = x.astype(F32)
    x1, x2 = xf[..., :half], xf[..., half:]
    return jnp.concatenate([x1 * cos - x2 * sin, x2 * cos + x1 * sin], axis=-1).astype(x.dtype)


def _mixer_inputs(x, g_norm, w_in, pos):
    b, s, _ = x.shape
    h = _rmsnorm(x, g_norm)
    proj = h @ w_in
    u, q, k, v = jnp.split(proj, [SSM_WIDTH, SSM_WIDTH + ATTN_WIDTH, SSM_WIDTH + 2 * ATTN_WIDTH], axis=-1)
    u = u.reshape(b, s, N_SSM_GROUPS, SSM_GROUP)
    q = _rope(q.reshape(b, s, N_HEADS, HEAD_DIM), pos)
    k = _rope(k.reshape(b, s, N_HEADS, HEAD_DIM), pos)
    v = v.reshape(b, s, N_HEADS, HEAD_DIM)
    return u, q, k, v


def _s5(u, x0_re, x0_im, a_re, a_im, log_dt, b_re, b_im, c_re, c_im, d_skip):
    uf = u.astype(F32)
    lam = lax.complex(a_re.astype(F32), a_im.astype(F32))
    dt = jnp.exp(log_dt.astype(F32))[:, None]
    lam_bar = jnp.exp(lam * dt)
    b_c = lax.complex(b_re.astype(F32), b_im.astype(F32))
    b_bar = ((lam_bar - 1.0) / lam)[..., None] * b_c
    bu = jnp.einsum('bsgp,gnp->bsgn', uf, b_bar)
    x0 = lax.complex(x0_re.astype(F32), x0_im.astype(F32))
    bu = bu.at[:, 0].add(lam_bar * x0)
    a = jnp.broadcast_to(lam_bar, bu.shape)

    def combine(left, right):
        a1, b1 = left
        a2, b2 = right
        return a1 * a2, a2 * b1 + b2

    _, xs = lax.associative_scan(combine, (a, bu), axis=1)
    c_c = lax.complex(c_re.astype(F32), c_im.astype(F32))
    y = jnp.einsum('gpn,bsgn->bsgp', c_c, xs).real + d_skip.astype(F32) * uf
    x_last = xs[:, -1]
    return y, x_last.real, x_last.imag


def _ssm_branch(u, x0_re, x0_im, lp):
    b, s = u.shape[:2]
    y, sr, si = _s5(u, x0_re, x0_im, lp['ssm_a_re'], lp['ssm_a_im'], lp['ssm_log_dt'],
                    lp['ssm_b_re'], lp['ssm_b_im'], lp['ssm_c_re'], lp['ssm_c_im'], lp['ssm_d'])
    z = jax.nn.gelu(y.reshape(b, s, SSM_WIDTH)).astype(u.dtype)
    z = z * jax.nn.sigmoid(z @ lp['w_glu'])
    return z, sr, si


def _branch_prompt(q, k, v, window, dilation):
    b, s, h, dh = q.shape
    n = window // dilation
    qb = n
    nb = -(-s // (dilation * qb))
    sp = nb * qb * dilation

    def strided(t):
        t = jnp.pad(t, ((0, 0), (0, sp - s), (0, 0), (0, 0)))
        t = t.reshape(b, nb * qb, dilation, h, dh).transpose(0, 2, 1, 3, 4)
        return t.reshape(b, dilation, nb, qb, h, dh)

    def with_prev(t):
        prev = jnp.pad(t[:, :, :-1], ((0, 0), (0, 0), (1, 0), (0, 0), (0, 0), (0, 0)))
        return jnp.concatenate([prev, t], axis=3)

    qs = strided(q)
    kb = with_prev(strided(k))
    vb = with_prev(strided(v))
    scores = jnp.einsum('brnqhc,brnkhc->brnhqk', qs, kb, preferred_element_type=F32) * (dh ** -0.5)
    qi = jnp.arange(qb)[:, None]
    ki = jnp.arange(2 * qb)[None, :]
    dist = qi + qb - ki
    band = (dist >= 0) & (dist <= n)
    has_prev = (jnp.arange(nb) > 0)[:, None, None] | (ki >= qb)[None]
    mask = band[None] & has_prev
    scores = jnp.where(mask[None, None, :, None], scores, -jnp.inf)
    m = jnp.max(scores, axis=-1, keepdims=True)
    p = jnp.exp(scores - m)
    l = jnp.sum(p, axis=-1, keepdims=True)
    o = jnp.einsum('brnhqk,brnkhc->brnqhc', p / l, vb.astype(F32))
    lse = (m + jnp.log(l))[..., 0]
    o = o.reshape(b, dilation, nb * qb, h, dh).transpose(0, 2, 1, 3, 4).reshape(b, sp, h, dh)[:, :s]
    lse = lse.transpose(0, 1, 2, 4, 3).reshape(b, dilation, nb * qb, h)
    lse = lse.transpose(0, 2, 1, 3).reshape(b, sp, h)[:, :s]
    return o, lse


def _branch_sample(q, k_all, v_all, window, dilation, n_past):
    t = q.shape[1]
    dh = q.shape[-1]
    n = window // dilation
    idx = (n_past + jnp.arange(t))[:, None] - dilation * jnp.arange(n + 1)[None, :]
    valid = idx >= 0
    idx = jnp.maximum(idx, 0)
    kg = k_all[:, idx]
    vg = v_all[:, idx]
    scores = jnp.einsum('bthc,btjhc->bthj', q, kg, preferred_element_type=F32) * (dh ** -0.5)
    scores = jnp.where(valid[None, :, None, :], scores, -jnp.inf)
    m = jnp.max(scores, axis=-1, keepdims=True)
    p = jnp.exp(scores - m)
    l = jnp.sum(p, axis=-1, keepdims=True)
    o = jnp.einsum('bthj,btjhc->bthc', p / l, vg.astype(F32))
    lse = (m + jnp.log(l))[..., 0]
    return o, lse


def _merge_branches(outs, lses):
    alpha = jax.nn.softmax(jnp.stack(lses, axis=0), axis=0)
    return jnp.sum(alpha[..., None] * jnp.stack(outs, axis=0), axis=0)


def _hier_moe(h, w_rg, b_rg, w_re, b_re, w_g, w_u, w_d):
    t, d = h.shape
    hf = h.astype(F32)
    lg = hf @ w_rg.astype(F32) + b_rg.astype(F32)
    pg = jax.nn.softmax(lg, axis=-1)
    grp = jnp.argmax(lg, axis=-1)
    gate1 = jnp.take_along_axis(pg, grp[:, None], axis=-1)
    le = jnp.einsum('td,dge->tge', hf, w_re.astype(F32)) + b_re.astype(F32)
    le = jnp.take_along_axis(le, grp[:, None, None], axis=1)[:, 0]
    top_v, top_i = lax.top_k(le, TOP_K)
    gates = gate1 * jax.nn.softmax(top_v, axis=-1)
    eid = grp[:, None] * EXPERTS_PER_GROUP + top_i
    tk = t * TOP_K
    e_flat = eid.reshape(tk)
    w_flat = gates.reshape(tk)
    tok_flat = jnp.repeat(jnp.arange(t, dtype=jnp.int32), TOP_K)
    order = jnp.argsort(e_flat)
    e_s, tok_s, w_s = e_flat[order], tok_flat[order], w_flat[order]
    counts = jnp.bincount(e_flat, length=N_EXPERTS)
    padded = (counts + MOE_BLOCK - 1) // MOE_BLOCK * MOE_BLOCK
    off = jnp.cumsum(counts) - counts
    pend = jnp.cumsum(padded)
    poff = pend - padded
    dest = poff[e_s] + jnp.arange(tk) - off[e_s]
    n_blocks = -(-tk // MOE_BLOCK) + N_EXPERTS
    n_rows = n_blocks * MOE_BLOCK
    row_tok = jnp.full((n_rows,), t, jnp.int32).at[dest].set(tok_s)
    block_e = jnp.minimum(jnp.searchsorted(pend, jnp.arange(n_blocks) * MOE_BLOCK, side='right'), N_EXPERTS - 1)
    h_pad = jnp.concatenate([h, jnp.zeros((1, d), h.dtype)], axis=0)

    def expert_block(args):
        rows, e = args
        xb = h_pad[rows]
        return (jax.nn.silu(xb @ w_g[e]) * (xb @ w_u[e])) @ w_d[e]

    y_rows = lax.map(expert_block, (row_tok.reshape(n_blocks, MOE_BLOCK), block_e)).reshape(n_rows, d)
    contrib = y_rows[dest].astype(F32) * w_s[:, None]
    out = jax.ops.segment_sum(contrib, tok_s, num_segments=t)
    return out.astype(h.dtype)


def _finish(x, z_ssm, o_attn, lp):
    b, s, _ = x.shape
    mix = jnp.concatenate([_rmsnorm(z_ssm, lp['g_ssm_out']),
                           _rmsnorm(o_attn.reshape(b, s, ATTN_WIDTH), lp['g_attn_out'])], axis=-1)
    x = x + mix @ lp['w_out']
    h = _rmsnorm(x, lp['g_ffn_norm'])
    y = _hier_moe(h.reshape(b * s, D_MODEL), lp['w_router_group'], lp['b_router_group'],
                  lp['w_router_expert'], lp['b_router_expert'],
                  lp['w_expert_gate'], lp['w_expert_up'], lp['w_expert_down'])
    return x + y.reshape(b, s, D_MODEL)


def _layer_prompt(x, lp):
    b, s, _ = x.shape
    pos = jnp.arange(s, dtype=F32)
    u, q, k, v = _mixer_inputs(x, lp['g_mix_norm'], lp['w_in'], pos)
    zeros = jnp.zeros((b, N_SSM_GROUPS, SSM_STATE), F32)
    z, sr, si = _ssm_branch(u, zeros, zeros, lp)
    outs, lses = [], []
    for window, dilation in DILATED_BRANCHES:
        o_i, l_i = _branch_prompt(q, k, v, window, dilation)
        outs.append(o_i)
        lses.append(l_i)
    o = _merge_branches(outs, lses).astype(x.dtype)
    x = _finish(x, z, o, lp)
    wb = min(MAX_WINDOW, s)
    return x, k[:, s - wb:], v[:, s - wb:], sr, si


def _layer_sample(x, k_past, v_past, sr0, si0, lp):
    b, t, _ = x.shape
    pos = PAST_LEN + jnp.arange(t, dtype=F32)
    u, q, k, v = _mixer_inputs(x, lp['g_mix_norm'], lp['w_in'], pos)
    z, sr, si = _ssm_branch(u, sr0, si0, lp)
    k_all = jnp.concatenate([k_past.astype(k.dtype), k], axis=1)
    v_all = jnp.concatenate([v_past.astype(v.dtype), v], axis=1)
    n_past = k_past.shape[1]
    outs, lses = [], []
    for window, dilation in DILATED_BRANCHES:
        o_i, l_i = _branch_sample(q, k_all, v_all, window, dilation, n_past)
        outs.append(o_i)
        lses.append(l_i)
    o = _merge_branches(outs, lses).astype(x.dtype)
    x = _finish(x, z, o, lp)
    return x, k, v, sr, si


def setup_inputs(seed: int = 0) -> dict:
    key = jax.random.key(seed)
    ks = jax.random.split(key, 32)

    def nrm(k, shape, std):
        return std * jax.random.normal(k, shape, F32)

    wb = min(MAX_WINDOW, PAST_LEN)
    g, n, p = N_SSM_GROUPS, SSM_STATE, SSM_GROUP
    return {
        'x_prompt': nrm(ks[0], (BATCH, SEQ, D_MODEL), 1.0),
        'x_sample': nrm(ks[1], (DEC_BATCH, DEC_SEQ, D_MODEL), 1.0),
        'cache_k': nrm(ks[2], (DEPTH, DEC_BATCH, wb, N_HEADS, HEAD_DIM), 1.0),
        'cache_v': nrm(ks[3], (DEPTH, DEC_BATCH, wb, N_HEADS, HEAD_DIM), 1.0),
        'state_ssm_re': nrm(ks[4], (DEPTH, DEC_BATCH, g, n), 0.5),
        'state_ssm_im': nrm(ks[5], (DEPTH, DEC_BATCH, g, n), 0.5),
        'g_mix_norm': 1.0 + nrm(ks[6], (DEPTH, D_MODEL), 0.02),
        'w_in': nrm(ks[7], (DEPTH, D_MODEL, D_IN), D_MODEL ** -0.5),
        'ssm_a_re': -0.5 + nrm(ks[8], (DEPTH, g, n), 0.01),
        'ssm_a_im': jnp.pi * jnp.arange(n, dtype=F32) + nrm(ks[9], (DEPTH, g, n), 0.01),
        'ssm_log_dt': jax.random.uniform(ks[10], (DEPTH, g), F32, minval=math.log(1e-3), maxval=math.log(1e-1)),
        'ssm_b_re': nrm(ks[11], (DEPTH, g, n, p), (2 * p) ** -0.5),
        'ssm_b_im': nrm(ks[12], (DEPTH, g, n, p), (2 * p) ** -0.5),
        'ssm_c_re': nrm(ks[13], (DEPTH, g, p, n), (2 * n) ** -0.5),
        'ssm_c_im': nrm(ks[14], (DEPTH, g, p, n), (2 * n) ** -0.5),
        'ssm_d': nrm(ks[15], (DEPTH, g, p), 1.0),
        'w_glu': nrm(ks[16], (DEPTH, SSM_WIDTH, SSM_WIDTH), SSM_WIDTH ** -0.5),
        'g_ssm_out': 1.0 + nrm(ks[17], (DEPTH, SSM_WIDTH), 0.02),
        'g_attn_out': 1.0 + nrm(ks[18], (DEPTH, ATTN_WIDTH), 0.02),
        'w_out': nrm(ks[19], (DEPTH, MIX_WIDTH, D_MODEL), MIX_WIDTH ** -0.5),
        'g_ffn_norm': 1.0 + nrm(ks[20], (DEPTH, D_MODEL), 0.02),
        'w_router_group': nrm(ks[21], (DEPTH, D_MODEL, N_EXPERT_GROUPS), D_MODEL ** -0.5),
        'b_router_group': nrm(ks[22], (DEPTH, N_EXPERT_GROUPS), 0.01),
        'w_router_expert': nrm(ks[23], (DEPTH, D_MODEL, N_EXPERT_GROUPS, EXPERTS_PER_GROUP), D_MODEL ** -0.5),
        'b_router_expert': nrm(ks[24], (DEPTH, N_EXPERT_GROUPS, EXPERTS_PER_GROUP), 0.01),
        'w_expert_gate': nrm(ks[25], (DEPTH, N_EXPERTS, D_MODEL, EXPERT_FF), D_MODEL ** -0.5),
        'w_expert_up': nrm(ks[26], (DEPTH, N_EXPERTS, D_MODEL, EXPERT_FF), D_MODEL ** -0.5),
        'w_expert_down': nrm(ks[27], (DEPTH, N_EXPERTS, EXPERT_FF, D_MODEL), EXPERT_FF ** -0.5),
        'g_final': 1.0 + nrm(ks[28], (D_MODEL,), 0.02),
    }


def reference(x_prompt, x_sample, cache_k, cache_v, state_ssm_re, state_ssm_im,
              g_mix_norm, w_in, ssm_a_re, ssm_a_im, ssm_log_dt, ssm_b_re, ssm_b_im,
              ssm_c_re, ssm_c_im, ssm_d, w_glu, g_ssm_out, g_attn_out, w_out, g_ffn_norm,
              w_router_group, b_router_group, w_router_expert, b_router_expert,
              w_expert_gate, w_expert_up, w_expert_down, g_final):
    xp, xs = x_prompt, x_sample
    kp_l, vp_l, srp_l, sip_l = [], [], [], []
    ks_l, vs_l, srs_l, sis_l = [], [], [], []
    for l in range(DEPTH):
        lp = {
            'g_mix_norm': g_mix_norm[l], 'w_in': w_in[l],
            'ssm_a_re': ssm_a_re[l], 'ssm_a_im': ssm_a_im[l], 'ssm_log_dt': ssm_log_dt[l],
            'ssm_b_re': ssm_b_re[l], 'ssm_b_im': ssm_b_im[l],
            'ssm_c_re': ssm_c_re[l], 'ssm_c_im': ssm_c_im[l], 'ssm_d': ssm_d[l],
            'w_glu': w_glu[l], 'g_ssm_out': g_ssm_out[l], 'g_attn_out': g_attn_out[l],
            'w_out': w_out[l], 'g_ffn_norm': g_ffn_norm[l],
            'w_router_group': w_router_group[l], 'b_router_group': b_router_group[l],
            'w_router_expert': w_router_expert[l], 'b_router_expert': b_router_expert[l],
            'w_expert_gate': w_expert_gate[l], 'w_expert_up': w_expert_up[l],
            'w_expert_down': w_expert_down[l],
        }
        xp, kp, vp, srp, sip = _layer_prompt(xp, lp)
        xs, kn, vn, srs, sis = _layer_sample(xs, cache_k[l], cache_v[l], state_ssm_re[l], state_ssm_im[l], lp)
        kp_l.append(kp); vp_l.append(vp); srp_l.append(srp); sip_l.append(sip)
        ks_l.append(kn); vs_l.append(vn); srs_l.append(srs); sis_l.append(sis)
    y_prompt = _rmsnorm(xp, g_final)
    y_sample = _rmsnorm(xs, g_final)
    new_k_prompt = jnp.stack(kp_l, axis=0)
    new_v_prompt = jnp.stack(vp_l, axis=0)
    new_ssm_re_prompt = jnp.stack(srp_l, axis=0)
    new_ssm_im_prompt = jnp.stack(sip_l, axis=0)
    new_k_sample = jnp.stack(ks_l, axis=0)
    new_v_sample = jnp.stack(vs_l, axis=0)
    new_ssm_re_sample = jnp.stack(srs_l, axis=0)
    new_ssm_im_sample = jnp.stack(sis_l, axis=0)
    return (y_prompt, y_sample, new_k_prompt, new_v_prompt, new_ssm_re_prompt, new_ssm_im_prompt,
            new_k_sample, new_v_sample, new_ssm_re_sample, new_ssm_im_sample)
```

```python
import functools
import math

import jax
import jax.numpy as jnp
from jax import lax
from jax.experimental import pallas as pl
from jax.experimental.pallas import tpu as pltpu

F32 = jnp.float32
BF16 = jnp.bfloat16
I32 = jnp.int32
U32 = jnp.uint32
HIGHEST = lax.Precision.HIGHEST

D_MODEL = 2048
BATCH = 2
SEQ = 4096
DEC_BATCH = 32
PAST_LEN = 8192
CACHE_LEN = 2048
SSM_WIDTH = 1024
ATTN_WIDTH = 1024
SSM_GROUP = 16
N_GROUPS = 64
SSM_STATE = 64
HEAD_DIM = 128
N_HEADS = 8
D_IN = 4096
BRANCHES = ((128, 1), (512, 4), (2048, 16))
BAND = 128
ROPE_THETA = 10000.0
N_EXPERT_GROUPS = 4
EXPERTS_PER_GROUP = 8
N_EXPERTS = 32
EXPERT_FF = 512
NORM_EPS = 1e-6

N_SEG = 8
SEG_LEN = SEQ // N_SEG
N_CHUNK = 4
CH_IN = SSM_WIDTH // N_CHUNK
CH_ST = N_GROUPS * SSM_STATE // N_CHUNK
SSM_TK = 64
TM = 256
T_PROMPT = BATCH * SEQ
T_REAL = T_PROMPT + DEC_BATCH
T_PAD = T_PROMPT + TM
N_TOK_BLOCKS = T_PAD // TM
MOE_BLK = 256
N_MOE_BLOCKS = -(-(2 * T_REAL) // MOE_BLK) + N_EXPERTS
N_MOE_ROWS = N_MOE_BLOCKS * MOE_BLK
VMEM_LIMIT = 56 * 1024 * 1024
NEG = -1e30


def _cparams(sem, vmem=VMEM_LIMIT):
    return pltpu.CompilerParams(dimension_semantics=sem, vmem_limit_bytes=vmem)


def _rms(x, g):
    return x * lax.rsqrt(jnp.mean(x * x, axis=-1, keepdims=True) + NORM_EPS) * g


def _gelu_tanh(y):
    return 0.5 * y * (1.0 + jnp.tanh(0.7978845608028654 * (y + 0.044715 * (y * y * y))))


def _sigmoid(x):
    return 1.0 / (1.0 + jnp.exp(-x))


def _rope_tables(pos, inv):
    ang = pos * inv
    lane = lax.broadcasted_iota(I32, ang.shape, 1)
    return jnp.cos(ang), jnp.where(lane < HEAD_DIM // 2, -jnp.sin(ang), jnp.sin(ang))


def _rope_heads(x, cos, sin):
    outs = []
    for h in range(x.shape[1] // HEAD_DIM):
        xh = x[:, h * HEAD_DIM:(h + 1) * HEAD_DIM]
        outs.append(xh * cos + pltpu.roll(xh, HEAD_DIM // 2, 1) * sin)
    return jnp.concatenate(outs, axis=1)


def _pack_bf16_pair(a, b):
    ua = lax.bitcast_convert_type(a.astype(BF16).astype(F32), U32)
    ub = lax.bitcast_convert_type(b.astype(BF16).astype(F32), U32)
    return ua | (ub >> 16)


def _unpack_bf16_pair(p):
    a = lax.bitcast_convert_type(p & jnp.uint32(0xFFFF0000), F32).astype(BF16)
    b = lax.bitcast_convert_type(p << 16, F32).astype(BF16)
    return a, b


def _ssm_param_kernel(are_ref, aim_ref, ldt_ref, btr_ref, bti_ref,
                      lr_ref, li_ref, lsr_ref, lsi_ref, bbr_ref, bbi_ref):
    ar, ai = are_ref[...], aim_ref[...]
    dt = jnp.exp(ldt_ref[...])
    er, ei = ar * dt, ai * dt
    mag = jnp.exp(er)
    lr, li = mag * jnp.cos(ei), mag * jnp.sin(ei)
    lr_ref[...] = lr
    li_ref[...] = li
    mag_s = jnp.exp(er * SEG_LEN)
    lsr_ref[...] = mag_s * jnp.cos(ei * SEG_LEN)
    lsi_ref[...] = mag_s * jnp.sin(ei * SEG_LEN)
    xr, xi = lr - 1.0, li
    den = ar * ar + ai * ai
    cr = (xr * ar + xi * ai) / den
    ci = (xi * ar - xr * ai) / den
    btr, bti = btr_ref[...], bti_ref[...]
    bbr_ref[...] = cr[:, None, :] * btr - ci[:, None, :] * bti
    bbi_ref[...] = cr[:, None, :] * bti + ci[:, None, :] * btr


def _ssm_params(a_re, a_im, log_dt, b_re, b_im):
    g, n, p = N_GROUPS, SSM_STATE, SSM_GROUP
    gn = jax.ShapeDtypeStruct((g, n), F32)
    gpn = jax.ShapeDtypeStruct((g, p, n), F32)
    return pl.pallas_call(
        _ssm_param_kernel, out_shape=(gn, gn, gn, gn, gpn, gpn), name="ssm_params",
    )(a_re, a_im, log_dt.reshape(g, 1), b_re.transpose(0, 2, 1), b_im.transpose(0, 2, 1))


def _block_diag_in(m):
    m4 = m.reshape(N_CHUNK, 16, SSM_GROUP, SSM_STATE)
    return jnp.einsum('jgpn,gh->jgphn', m4, jnp.eye(16, dtype=F32)).reshape(N_CHUNK, CH_IN, CH_ST)


def _block_diag_out(c):
    c4 = c.reshape(N_CHUNK, 16, SSM_GROUP, SSM_STATE)
    return jnp.einsum('jgpn,gh->jhngp', c4, jnp.eye(16, dtype=F32)).reshape(N_CHUNK, CH_ST, CH_IN)


def _inproj_kernel(x_ref, g_ref, inv_ref, w_ref, u_ref, q_ref, k_ref, v_ref):
    seg, kb = pl.program_id(1), pl.program_id(2)
    h = _rms(x_ref[0], g_ref[...]).astype(BF16)
    pos0 = seg * SEG_LEN + kb * TM
    pos = (lax.broadcasted_iota(I32, (TM, 1), 0) + pos0).astype(F32)
    cos, sin = _rope_tables(pos, inv_ref[...])
    w = SSM_WIDTH
    u_ref[0] = jnp.dot(h, w_ref[:, 0:w], preferred_element_type=F32)
    q_ref[0] = _rope_heads(jnp.dot(h, w_ref[:, w:2 * w], preferred_element_type=F32), cos, sin)
    k_ref[0] = _rope_heads(jnp.dot(h, w_ref[:, 2 * w:3 * w], preferred_element_type=F32), cos, sin)
    v_ref[0] = jnp.dot(h, w_ref[:, 3 * w:4 * w], preferred_element_type=F32)


def _inproj(x, g, inv, w_bf16):
    nkb = SEG_LEN // TM
    tok = lambda b, s, k: (b, s * nkb + k, 0)
    qkv = jax.ShapeDtypeStruct((BATCH, SEQ, ATTN_WIDTH), F32)
    return pl.pallas_call(
        _inproj_kernel, grid=(BATCH, N_SEG, nkb),
        in_specs=[pl.BlockSpec((1, TM, D_MODEL), tok),
                  pl.BlockSpec((1, D_MODEL), lambda b, s, k: (0, 0)),
                  pl.BlockSpec((1, HEAD_DIM), lambda b, s, k: (0, 0)),
                  pl.BlockSpec((D_MODEL, D_IN), lambda b, s, k: (0, 0))],
        out_specs=[pl.BlockSpec((1, TM, SSM_WIDTH), lambda b, s, k: (b, k, s)),
                   pl.BlockSpec((1, TM, ATTN_WIDTH), tok),
                   pl.BlockSpec((1, TM, ATTN_WIDTH), tok),
                   pl.BlockSpec((1, TM, ATTN_WIDTH), tok)],
        out_shape=(jax.ShapeDtypeStruct((BATCH, SEG_LEN, N_SEG * SSM_WIDTH), F32), qkv, qkv, qkv),
        compiler_params=_cparams(("arbitrary",) * 3), name="inproj",
    )(x, g, inv, w_bf16)


def _ssm_scan_kernel(*refs, final):
    if final:
        (u_ref, bm_ref, lr_ref, li_ref, e_ref, lsr_ref, lsi_ref, cm_ref, d_ref,
         z_ref, xend_ref, bu_scr, xr_scr, xi_scr) = refs
    else:
        u_ref, bm_ref, lr_ref, li_ref, xend_ref, bu_scr, xr_scr, xi_scr = refs
    kb = pl.program_id(2)

    @pl.when(kb == 0)
    def _():
        if final:
            lsr, lsi = lsr_ref[0:1, :], lsi_ref[0:1, :]
            seg_id = lax.broadcasted_iota(I32, (N_SEG, CH_ST), 0)
            pr = pi = jnp.zeros((1, CH_ST), F32)
            xr0 = xi0 = jnp.zeros((N_SEG, CH_ST), F32)
            for s in range(1, N_SEG):
                er = e_ref[0, 0, s - 1:s, 0:CH_ST]
                ei = e_ref[0, 0, s - 1:s, CH_ST:2 * CH_ST]
                pr, pi = er + lsr * pr - lsi * pi, ei + lsr * pi + lsi * pr
                xr0 = jnp.where(seg_id == s, pr, xr0)
                xi0 = jnp.where(seg_id == s, pi, xi0)
            xr_scr[...] = xr0
            xi_scr[...] = xi0
        else:
            xr_scr[...] = jnp.zeros((N_SEG, CH_ST), F32)
            xi_scr[...] = jnp.zeros((N_SEG, CH_ST), F32)

    bu_scr[...] = jnp.dot(u_ref[0].astype(BF16), bm_ref[0], preferred_element_type=F32)
    lr, li = lr_ref[...], li_ref[...]

    def step(k, carry):
        xr, xi = carry
        r0 = pl.multiple_of(k * N_SEG, N_SEG)
        bur = bu_scr[pl.ds(r0, N_SEG), 0:CH_ST]
        bui = bu_scr[pl.ds(r0, N_SEG), CH_ST:2 * CH_ST]
        nr = lr * xr - li * xi + bur
        ni = lr * xi + li * xr + bui
        if final:
            bu_scr[pl.ds(r0, N_SEG), 0:CH_ST] = nr
            bu_scr[pl.ds(r0, N_SEG), CH_ST:2 * CH_ST] = ni
        return nr, ni

    xr, xi = lax.fori_loop(0, SSM_TK, step, (xr_scr[...], xi_scr[...]), unroll=2)
    xr_scr[...] = xr
    xi_scr[...] = xi
    if final:
        y = jnp.dot(bu_scr[...].astype(BF16), cm_ref[0], preferred_element_type=F32)
        z_ref[0] = _gelu_tanh(y + d_ref[...] * u_ref[0])

    @pl.when(kb == pl.num_programs(2) - 1)
    def _():
        xend_ref[0, 0] = jnp.concatenate([xr, xi], axis=1)


def _ssm_scan(u_perm, bm, lr8, li8, final, e=None, lsr8=None, lsi8=None, cm=None, dskip=None):
    rows = SSM_TK * N_SEG
    nkb = SEG_LEN // SSM_TK
    in_specs = [pl.BlockSpec((1, rows, CH_IN), lambda b, j, k: (b, k, j)),
                pl.BlockSpec((1, CH_IN, 2 * CH_ST), lambda b, j, k: (j, 0, 0)),
                pl.BlockSpec((N_SEG, CH_ST), lambda b, j, k: (0, j)),
                pl.BlockSpec((N_SEG, CH_ST), lambda b, j, k: (0, j))]
    args = [u_perm, bm, lr8, li8]
    xend_spec = pl.BlockSpec((1, 1, N_SEG, 2 * CH_ST), lambda b, j, k: (b, j, 0, 0))
    xend_shape = jax.ShapeDtypeStruct((BATCH, N_CHUNK, N_SEG, 2 * CH_ST), F32)
    if final:
        in_specs += [xend_spec,
                     pl.BlockSpec((N_SEG, CH_ST), lambda b, j, k: (0, j)),
                     pl.BlockSpec((N_SEG, CH_ST), lambda b, j, k: (0, j)),
                     pl.BlockSpec((1, 2 * CH_ST, CH_IN), lambda b, j, k: (j, 0, 0)),
                     pl.BlockSpec((1, CH_IN), lambda b, j, k: (0, j))]
        args += [e, lsr8, lsi8, cm, dskip]
        out_specs = [pl.BlockSpec((1, rows, CH_IN), lambda b, j, k: (b, k, j)), xend_spec]
        out_shape = (jax.ShapeDtypeStruct((BATCH, SEQ, SSM_WIDTH), F32), xend_shape)
    else:
        out_specs = xend_spec
        out_shape = xend_shape
    return pl.pallas_call(
        functools.partial(_ssm_scan_kernel, final=final), grid=(BATCH, N_CHUNK, nkb),
        in_specs=in_specs, out_specs=out_specs, out_shape=out_shape,
        scratch_shapes=[pltpu.VMEM((rows, 2 * CH_ST), F32),
                        pltpu.VMEM((N_SEG, CH_ST), F32), pltpu.VMEM((N_SEG, CH_ST), F32)],
        compiler_params=_cparams(("arbitrary",) * 3),
        name="ssm_scan_final" if final else "ssm_scan_ends",
    )(*args)


def _glu_kernel(z_ref, w_ref, g_ref, o_ref):
    z = z_ref[0]
    zz = z * _sigmoid(jnp.dot(z.astype(BF16), w_ref[...], preferred_element_type=F32))
    o_ref[0] = _rms(zz, g_ref[...]).astype(BF16)


def _glu(z, w_bf16, g):
    rows = 512
    return pl.pallas_call(
        _glu_kernel, grid=(BATCH, SEQ // rows),
        in_specs=[pl.BlockSpec((1, rows, SSM_WIDTH), lambda b, i: (b, i, 0)),
                  pl.BlockSpec((SSM_WIDTH, SSM_WIDTH), lambda b, i: (0, 0)),
                  pl.BlockSpec((1, SSM_WIDTH), lambda b, i: (0, 0))],
        out_specs=pl.BlockSpec((1, rows, SSM_WIDTH), lambda b, i: (b, i, 0)),
        out_shape=jax.ShapeDtypeStruct((BATCH, SEQ, SSM_WIDTH), BF16),
        compiler_params=_cparams(("arbitrary",) * 2), name="ssm_glu",
    )(z, w_bf16, g)


def _attn_kernel(q_ref, k_ref, v_ref, o_ref, o_scr, lse_scr):
    scale = HEAD_DIM ** -0.5
    qi = lax.broadcasted_iota(I32, (BAND, BAND), 0)
    kj = lax.broadcasted_iota(I32, (BAND, BAND), 1)
    nt = (((1,), (1,)), ((), ()))
    for bi, (_, d) in enumerate(BRANCHES):
        nblk = SEQ // (BAND * d)
        shift = nblk.bit_length() - 1

        def tile(i, c, bi=bi, d=d, nblk=nblk, shift=shift):
            r = i >> shift
            ib = i & (nblk - 1)
            start = r + ib * (BAND * d)
            pstart = jnp.maximum(start - BAND * d, r)
            cur = pl.ds(start, BAND, stride=d)
            prev = pl.ds(pstart, BAND, stride=d)
            q = q_ref[0, cur, :].astype(BF16)
            kc, vc = k_ref[0, cur, :].astype(BF16), v_ref[0, cur, :].astype(BF16)
            kp, vp = k_ref[0, prev, :].astype(BF16), v_ref[0, prev, :].astype(BF16)
            sc = lax.dot_general(q, kc, nt, preferred_element_type=F32) * scale
            sp = lax.dot_general(q, kp, nt, preferred_element_type=F32) * scale
            sc = jnp.where(kj <= qi, sc, NEG)
            sp = jnp.where(kj >= qi + jnp.where(ib > 0, 0, BAND), sp, NEG)
            m = jnp.maximum(jnp.max(sc, axis=-1, keepdims=True), jnp.max(sp, axis=-1, keepdims=True))
            pc, pp = jnp.exp(sc - m), jnp.exp(sp - m)
            l = jnp.sum(pc, axis=-1, keepdims=True) + jnp.sum(pp, axis=-1, keepdims=True)
            o = (jnp.dot(pc.astype(BF16), vc, preferred_element_type=F32)
                 + jnp.dot(pp.astype(BF16), vp, preferred_element_type=F32)) / l
            o_scr[bi, cur, :] = o
            lse_scr[bi, cur, :] = jnp.broadcast_to(m + jnp.log(l), (BAND, HEAD_DIM))
            return c

        lax.fori_loop(0, SEQ // BAND, tile, 0)

    rows = 512

    def merge(i, c):
        sl = pl.ds(pl.multiple_of(i * rows, rows), rows)
        l0, l1, l2 = lse_scr[0, sl, :], lse_scr[1, sl, :], lse_scr[2, sl, :]
        m = jnp.maximum(jnp.maximum(l0, l1), l2)
        e0, e1, e2 = jnp.exp(l0 - m), jnp.exp(l1 - m), jnp.exp(l2 - m)
        o_ref[0, sl, :] = (e0 * o_scr[0, sl, :] + e1 * o_scr[1, sl, :] + e2 * o_scr[2, sl, :]) / (e0 + e1 + e2)
        return c

    lax.fori_loop(0, SEQ // rows, merge, 0)


def _attention(q, k, v):
    spec = pl.BlockSpec((1, SEQ, HEAD_DIM), lambda b, h: (b, 0, h))
    return pl.pallas_call(
        _attn_kernel, grid=(BATCH, N_HEADS), in_specs=[spec, spec, spec], out_specs=spec,
        out_shape=jax.ShapeDtypeStruct((BATCH, SEQ, ATTN_WIDTH), F32),
        scratch_shapes=[pltpu.VMEM((3, SEQ, HEAD_DIM), F32), pltpu.VMEM((3, SEQ, HEAD_DIM), F32)],
        compiler_params=_cparams(("arbitrary",) * 2), name="dilated_attn",
    )(q, k, v)


def _outproj_kernel(x_ref, ms_ref, o_ref, ga_ref, w_ref, gf_ref, wr_ref, br_ref, xm_ref, hp_ref, lg_ref):
    is_prompt = pl.program_id(0) < T_PROMPT // TM

    @pl.when(is_prompt)
    def _():
        ma = _rms(o_ref[0], ga_ref[...]).astype(BF16)
        xm = (x_ref[0] + jnp.dot(ms_ref[0], w_ref[0:SSM_WIDTH, :], preferred_element_type=F32)
              + jnp.dot(ma, w_ref[SSM_WIDTH:, :], preferred_element_type=F32))
        xm_ref[...] = xm
        h = _rms(xm, gf_ref[...])
        hp_ref[...] = _pack_bf16_pair(h[:, :D_MODEL // 2], h[:, D_MODEL // 2:])
        lg_ref[...] = jnp.dot(h.astype(BF16), wr_ref[...], preferred_element_type=F32) + br_ref[...]

    @pl.when(jnp.logical_not(is_prompt))
    def _():
        xm_ref[...] = jnp.zeros((TM, D_MODEL), F32)
        hp_ref[...] = jnp.zeros((TM, D_MODEL // 2), U32)
        lg_ref[...] = jnp.zeros((TM, 128), F32)


def _outproj(x, mix_ssm_perm, o_attn, g_attn, w_bf16, g_ffn, wr_bf16, br):
    nkb = SEG_LEN // TM
    per_b = N_SEG * nkb

    def split(i):
        i = jnp.minimum(i, T_PROMPT // TM - 1)
        return i // per_b, (i // nkb) % N_SEG, i % nkb

    def tok(i):
        b, s, k = split(i)
        return (b, s * nkb + k, 0)

    def interleaved(i):
        b, s, k = split(i)
        return (b, k, s)

    row = lambda i: (i, 0)
    const = lambda i: (0, 0)
    return pl.pallas_call(
        _outproj_kernel, grid=(N_TOK_BLOCKS,),
        in_specs=[pl.BlockSpec((1, TM, D_MODEL), tok),
                  pl.BlockSpec((1, TM, SSM_WIDTH), interleaved),
                  pl.BlockSpec((1, TM, ATTN_WIDTH), tok),
                  pl.BlockSpec((1, ATTN_WIDTH), const),
                  pl.BlockSpec((D_MODEL, D_MODEL), const),
                  pl.BlockSpec((1, D_MODEL), const),
                  pl.BlockSpec((D_MODEL, 128), const),
                  pl.BlockSpec((1, 128), const)],
        out_specs=[pl.BlockSpec((TM, D_MODEL), row), pl.BlockSpec((TM, D_MODEL // 2), row),
                   pl.BlockSpec((TM, 128), row)],
        out_shape=(jax.ShapeDtypeStruct((T_PAD, D_MODEL), F32),
                   jax.ShapeDtypeStruct((T_PAD, D_MODEL // 2), U32),
                   jax.ShapeDtypeStruct((T_PAD, 128), F32)),
        compiler_params=_cparams(("arbitrary",)), name="outproj",
    )(x, mix_ssm_perm, o_attn, g_attn, w_bf16, g_ffn, wr_bf16, br)


def _s_inproj_kernel(x_ref, g_ref, w_ref, o_ref):
    h = _rms(x_ref[...], g_ref[...])
    o_ref[...] = jnp.dot(h, w_ref[...], preferred_element_type=F32, precision=HIGHEST)


def _s_inproj(x, g, w):
    tn = 1024
    return pl.pallas_call(
        _s_inproj_kernel, grid=(D_IN // tn,),
        in_specs=[pl.BlockSpec((DEC_BATCH, D_MODEL), lambda j: (0, 0)),
                  pl.BlockSpec((1, D_MODEL), lambda j: (0, 0)),
                  pl.BlockSpec((D_MODEL, tn), lambda j: (0, j))],
        out_specs=pl.BlockSpec((DEC_BATCH, tn), lambda j: (0, j)),
        out_shape=jax.ShapeDtypeStruct((DEC_BATCH, D_IN), F32),
        compiler_params=_cparams(("arbitrary",)), name="s_inproj",
    )(x, g, w)


def _s_ssm_kernel(u_ref, sr_ref, si_ref, lr_ref, li_ref, bm_ref, cm_ref, d_ref, z_ref, nr_ref, ni_ref):
    u = u_ref[...]
    bu = jnp.dot(u, bm_ref[0], preferred_element_type=F32, precision=HIGHEST)
    lr, li = lr_ref[...], li_ref[...]
    xr, xi = sr_ref[...], si_ref[...]
    nr = lr * xr - li * xi + bu[:, :CH_ST]
    ni = lr * xi + li * xr + bu[:, CH_ST:]
    nr_ref[...] = nr
    ni_ref[...] = ni
    y = jnp.dot(jnp.concatenate([nr, ni], axis=1), cm_ref[0], preferred_element_type=F32, precision=HIGHEST)
    z_ref[...] = _gelu_tanh(y + d_ref[...] * u)


def _s_ssm(u, sr, si, lr1, li1, bm, cm, dskip):
    st = jax.ShapeDtypeStruct((DEC_BATCH, N_GROUPS * SSM_STATE), F32)
    return pl.pallas_call(
        _s_ssm_kernel, grid=(N_CHUNK,),
        in_specs=[pl.BlockSpec((DEC_BATCH, CH_IN), lambda j: (0, j)),
                  pl.BlockSpec((DEC_BATCH, CH_ST), lambda j: (0, j)),
                  pl.BlockSpec((DEC_BATCH, CH_ST), lambda j: (0, j)),
                  pl.BlockSpec((1, CH_ST), lambda j: (0, j)),
                  pl.BlockSpec((1, CH_ST), lambda j: (0, j)),
                  pl.BlockSpec((1, CH_IN, 2 * CH_ST), lambda j: (j, 0, 0)),
                  pl.BlockSpec((1, 2 * CH_ST, CH_IN), lambda j: (j, 0, 0)),
                  pl.BlockSpec((1, CH_IN), lambda j: (0, j))],
        out_specs=[pl.BlockSpec((DEC_BATCH, CH_IN), lambda j: (0, j)),
                   pl.BlockSpec((DEC_BATCH, CH_ST), lambda j: (0, j)),
                   pl.BlockSpec((DEC_BATCH, CH_ST), lambda j: (0, j))],
        out_shape=(jax.ShapeDtypeStruct((DEC_BATCH, SSM_WIDTH), F32), st, st),
        compiler_params=_cparams(("arbitrary",)), name="s_ssm",
    )(u, sr, si, lr1, li1, bm, cm, dskip)


def _s_glu_kernel(z_ref, w_ref, g_ref, o_ref):
    z = z_ref[...]
    zz = z * _sigmoid(jnp.dot(z, w_ref[...], preferred_element_type=F32, precision=HIGHEST))
    o_ref[...] = _rms(zz, g_ref[...])


def _s_glu(z, w, g):
    return pl.pallas_call(
        _s_glu_kernel, out_shape=jax.ShapeDtypeStruct((DEC_BATCH, SSM_WIDTH), F32),
        compiler_params=pltpu.CompilerParams(vmem_limit_bytes=VMEM_LIMIT), name="s_glu",
    )(z, w, g)


def _s_attn_kernel(q_ref, k_ref, v_ref, inv_ref, k1_ref, v1_ref, k4_ref, v4_ref, k16_ref, v16_ref,
                   o_ref, kr_ref):
    scale = HEAD_DIM ** -0.5
    pos = jnp.full((N_HEADS, 1), float(PAST_LEN), F32)
    cos, sin = _rope_tables(pos, inv_ref[...])
    q = q_ref[...] * cos + pltpu.roll(q_ref[...], HEAD_DIM // 2, 1) * sin
    kn = k_ref[...] * cos + pltpu.roll(k_ref[...], HEAD_DIM // 2, 1) * sin
    vn = v_ref[...]
    kr_ref[...] = kn
    s0 = jnp.sum(q * kn, axis=-1, keepdims=True) * scale
    outs, lses = [], []
    for kc_ref, vc_ref in ((k1_ref, v1_ref), (k4_ref, v4_ref), (k16_ref, v16_ref)):
        s = jnp.sum(kc_ref[...] * q[None], axis=-1, keepdims=True) * scale
        m = jnp.maximum(jnp.max(s, axis=0), s0)
        p = jnp.exp(s - m[None])
        p0 = jnp.exp(s0 - m)
        l = jnp.sum(p, axis=0) + p0
        outs.append((jnp.sum(p * vc_ref[...], axis=0) + p0 * vn) / l)
        lses.append(m + jnp.log(l))
    mm = jnp.maximum(jnp.maximum(lses[0], lses[1]), lses[2])
    es = [jnp.exp(x - mm) for x in lses]
    o_ref[...] = (es[0] * outs[0] + es[1] * outs[1] + es[2] * outs[2]) / (es[0] + es[1] + es[2])


def _s_attention(q, k, v, inv, cache_k, cache_v):
    hd = (N_HEADS, HEAD_DIM)
    tok = pl.BlockSpec((None,) + hd, lambda b: (b, 0, 0))
    args, specs = [], []
    for _, d in BRANCHES:
        nrow = CACHE_LEN // d
        last = nrow // BAND - 1
        spec = pl.BlockSpec((None, BAND, None) + hd, lambda b, last=last: (b, last, 0, 0, 0))
        for c in (cache_k, cache_v):
            args.append(c.reshape(DEC_BATCH, nrow, d, *hd))
            specs.append(spec)
    out = jax.ShapeDtypeStruct((DEC_BATCH,) + hd, F32)
    return pl.pallas_call(
        _s_attn_kernel, grid=(DEC_BATCH,),
        in_specs=[tok, tok, tok, pl.BlockSpec((1, HEAD_DIM), lambda b: (0, 0))] + specs,
        out_specs=[tok, tok], out_shape=(out, out),
        compiler_params=_cparams(("arbitrary",)), name="s_attn",
    )(q, k, v, inv, *args)


def _s_outproj_kernel(xa_ref, hpa_ref, lga_ref, x_ref, ms_ref, o_ref, ga_ref, w_ref, gf_ref, wr_ref, br_ref,
                      xm_ref, hp_ref, lg_ref):
    del xa_ref, hpa_ref, lga_ref
    mix = jnp.concatenate([ms_ref[...], _rms(o_ref[...], ga_ref[...])], axis=1)
    xm = x_ref[...] + jnp.dot(mix, w_ref[...], preferred_element_type=F32, precision=HIGHEST)
    h = _rms(xm, gf_ref[...])
    lg = jnp.dot(h, wr_ref[...], preferred_element_type=F32, precision=HIGHEST) + br_ref[...]
    pad = TM - DEC_BATCH
    xm_ref[...] = jnp.concatenate([xm, jnp.zeros((pad, D_MODEL), F32)], axis=0)
    hp = _pack_bf16_pair(h[:, :D_MODEL // 2], h[:, D_MODEL // 2:])
    hp_ref[...] = jnp.concatenate([hp, jnp.zeros((pad, D_MODEL // 2), U32)], axis=0)
    lg_ref[...] = jnp.concatenate([lg, jnp.zeros((pad, 128), F32)], axis=0)


def _s_outproj(xm_all, hp_all, lg_all, x, mix_ssm, o_attn, g_attn, w, g_ffn, wr, br):
    last = lambda i: (N_TOK_BLOCKS - 1, 0)
    full = lambda shape: pl.BlockSpec(shape, lambda i: (0, 0))
    any_spec = pl.BlockSpec(memory_space=pl.ANY)
    return pl.pallas_call(
        _s_outproj_kernel, grid=(1,),
        in_specs=[any_spec, any_spec, any_spec,
                  full((DEC_BATCH, D_MODEL)), full((DEC_BATCH, SSM_WIDTH)), full((DEC_BATCH, ATTN_WIDTH)),
                  full((1, ATTN_WIDTH)), full((D_MODEL, D_MODEL)), full((1, D_MODEL)),
                  full((D_MODEL, 128)), full((1, 128))],
        out_specs=[pl.BlockSpec((TM, D_MODEL), last), pl.BlockSpec((TM, D_MODEL // 2), last),
                   pl.BlockSpec((TM, 128), last)],
        out_shape=(jax.ShapeDtypeStruct(xm_all.shape, F32), jax.ShapeDtypeStruct(hp_all.shape, U32),
                   jax.ShapeDtypeStruct(lg_all.shape, F32)),
        input_output_aliases={0: 0, 1: 1, 2: 2},
        compiler_params=_cparams(("arbitrary",)), name="s_outproj",
    )(xm_all, hp_all, lg_all, x, mix_ssm, o_attn, g_attn, w, g_ffn, wr, br)


def _route_kernel(lg_ref, ri_ref, rw_ref, cnt_ref, carry):
    i = pl.program_id(0)

    @pl.when(i == 0)
    def _():
        carry[...] = jnp.zeros((1, 128), F32)

    x = lg_ref[...]
    lane = lax.broadcasted_iota(I32, (TM, 128), 1)
    row = lax.broadcasted_iota(I32, (TM, 128), 0) + i * TM
    valid = row < T_REAL
    big = jnp.int32(1 << 20)
    gmask = lane < N_EXPERT_GROUPS
    lgm = jnp.where(gmask, x, NEG)
    m = jnp.max(lgm, axis=-1, keepdims=True)
    gate1 = 1.0 / jnp.sum(jnp.where(gmask, jnp.exp(lgm - m), 0.0), axis=-1, keepdims=True)
    grp = jnp.min(jnp.where(gmask & (lgm == m), lane, big), axis=-1, keepdims=True)
    lo = N_EXPERT_GROUPS + EXPERTS_PER_GROUP * grp
    emask = (lane >= lo) & (lane < lo + EXPERTS_PER_GROUP)
    le = jnp.where(emask, x, NEG)
    t1 = jnp.max(le, axis=-1, keepdims=True)
    i1 = jnp.min(jnp.where(emask & (le == t1), lane, big), axis=-1, keepdims=True)
    emask2 = emask & (lane != i1)
    le2 = jnp.where(emask2, x, NEG)
    t2 = jnp.max(le2, axis=-1, keepdims=True)
    i2 = jnp.min(jnp.where(emask2 & (le2 == t2), lane, big), axis=-1, keepdims=True)
    e21 = jnp.exp(t2 - t1)
    w1 = gate1 / (1.0 + e21)
    w2 = gate1 * e21 / (1.0 + e21)
    eid1, eid2 = i1 - N_EXPERT_GROUPS, i2 - N_EXPERT_GROUPS
    oh1 = jnp.where(valid & (lane == eid1), 1.0, 0.0)
    oh2 = jnp.where(valid & (lane == eid2), 1.0, 0.0)
    a = oh1 + oh2
    rr = lax.broadcasted_iota(I32, (TM, TM), 0)
    cc = lax.broadcasted_iota(I32, (TM, TM), 1)
    before = jnp.where(cc < rr, 1.0, 0.0).astype(BF16)
    pre = jnp.dot(before, a.astype(BF16), preferred_element_type=F32) + carry[...]
    rank1 = jnp.sum(oh1 * pre, axis=-1, keepdims=True).astype(I32)
    rank2 = jnp.sum(oh2 * pre, axis=-1, keepdims=True).astype(I32)
    carry[...] = carry[...] + jnp.sum(a, axis=0, keepdims=True)
    cnt_ref[...] = jnp.broadcast_to(carry[...], (8, 128))
    zi = jnp.zeros((TM, 128), I32)
    ri = jnp.where(lane == 0, eid1, jnp.where(lane == 1, eid2, jnp.where(lane == 2, rank1, jnp.where(lane == 3, rank2, zi))))
    ri_ref[...] = jnp.where(valid, ri, zi)
    rw = jnp.where(lane == 0, w1, jnp.where(lane == 1, w2, 0.0))
    rw_ref[...] = jnp.where(valid, rw, 0.0)


def _route(logits):
    blk = pl.BlockSpec((TM, 128), lambda i: (i, 0))
    return pl.pallas_call(
        _route_kernel, grid=(N_TOK_BLOCKS,), in_specs=[blk],
        out_specs=[blk, blk, pl.BlockSpec((8, 128), lambda i: (0, 0))],
        out_shape=(jax.ShapeDtypeStruct((T_PAD, 128), I32), jax.ShapeDtypeStruct((T_PAD, 128), F32),
                   jax.ShapeDtypeStruct((8, 128), F32)),
        scratch_shapes=[pltpu.VMEM((1, 128), F32)],
        compiler_params=_cparams(("arbitrary",)), name="route",
    )(logits)


def _row_copy(src, s, dst, d, sem):
    return pltpu.make_async_copy(src.at[pl.ds(s, 1)], dst.at[pl.ds(d, 1)], sem)


def _dispatch_kernel(dest_ref, hp_ref, xs_in_ref, xs_ref, sem):
    del xs_in_ref
    i = pl.program_id(0)
    n = jnp.minimum(TM, T_REAL - i * TM)

    def issue(r, c):
        t = i * TM + r
        _row_copy(hp_ref, t, xs_ref, dest_ref[0, 0, 2 * r], sem).start()
        _row_copy(hp_ref, t, xs_ref, dest_ref[0, 0, 2 * r + 1], sem).start()
        return c

    def drain(r, c):
        _row_copy(hp_ref, 0, xs_ref, 0, sem).wait()
        _row_copy(hp_ref, 0, xs_ref, 0, sem).wait()
        return c

    lax.fori_loop(0, n, issue, 0)
    lax.fori_loop(0, n, drain, 0)


def _dispatch(dest3, hp_all, xs_zero):
    any_spec = pl.BlockSpec(memory_space=pl.ANY)
    return pl.pallas_call(
        _dispatch_kernel, grid=(N_TOK_BLOCKS,),
        in_specs=[pl.BlockSpec((1, 1, 2 * TM), lambda i: (i, 0, 0), memory_space=pltpu.SMEM),
                  any_spec, any_spec],
        out_specs=any_spec, out_shape=jax.ShapeDtypeStruct(xs_zero.shape, U32),
        input_output_aliases={2: 0}, scratch_shapes=[pltpu.SemaphoreType.DMA(())],
        compiler_params=_cparams(("arbitrary",)), name="moe_dispatch",
    )(dest3, hp_all, xs_zero)


def _expert_kernel(be_ref, nv_ref, xs_ref, wg_ref, wu_ref, wd_ref, y_ref, wgu_scr, wd_scr):
    i = pl.program_id(0)
    changed = (i == 0) | (be_ref[i] != be_ref[jnp.maximum(i - 1, 0)])

    @pl.when(changed)
    def _():
        wgu_scr[:, 0:EXPERT_FF] = wg_ref[0].astype(BF16)
        wgu_scr[:, EXPERT_FF:] = wu_ref[0].astype(BF16)
        wd_scr[...] = wd_ref[0].astype(BF16)

    @pl.when(nv_ref[i] > 0)
    def _():
        a, b = _unpack_bf16_pair(xs_ref[...])
        half = D_MODEL // 2
        gu = (jnp.dot(a, wgu_scr[0:half, :], preferred_element_type=F32)
              + jnp.dot(b, wgu_scr[half:, :], preferred_element_type=F32))
        g, u = gu[:, :EXPERT_FF], gu[:, EXPERT_FF:]
        act = (g * _sigmoid(g) * u).astype(BF16)
        y_ref[...] = jnp.dot(act, wd_scr[...], preferred_element_type=F32)

    @pl.when(nv_ref[i] == 0)
    def _():
        y_ref[...] = jnp.zeros((MOE_BLK, D_MODEL), F32)


def _experts(block_e, nvalid, xs, w_g, w_u, w_d):
    grid_spec = pltpu.PrefetchScalarGridSpec(
        num_scalar_prefetch=2, grid=(N_MOE_BLOCKS,),
        in_specs=[pl.BlockSpec((MOE_BLK, D_MODEL // 2), lambda i, be, nv: (i, 0)),
                  pl.BlockSpec((1, D_MODEL, EXPERT_FF), lambda i, be, nv: (be[i], 0, 0)),
                  pl.BlockSpec((1, D_MODEL, EXPERT_FF), lambda i, be, nv: (be[i], 0, 0)),
                  pl.BlockSpec((1, EXPERT_FF, D_MODEL), lambda i, be, nv: (be[i], 0, 0))],
        out_specs=pl.BlockSpec((MOE_BLK, D_MODEL), lambda i, be, nv: (i, 0)),
        scratch_shapes=[pltpu.VMEM((D_MODEL, 2 * EXPERT_FF), BF16), pltpu.VMEM((EXPERT_FF, D_MODEL), BF16)])
    return pl.pallas_call(
        _expert_kernel, grid_spec=grid_spec,
        out_shape=jax.ShapeDtypeStruct((N_MOE_ROWS, D_MODEL), F32),
        compiler_params=_cparams(("arbitrary",)), name="moe_experts",
    )(block_e, nvalid, xs, w_g, w_u, w_d)


def _combine_kernel(dest_ref, rw_ref, xm_ref, g_ref, y_ref, o_ref, ybuf, sem, *, rows):
    def issue(r, c):
        _row_copy(y_ref, dest_ref[0, 0, 2 * r], ybuf.at[0], r, sem).start()
        _row_copy(y_ref, dest_ref[0, 0, 2 * r + 1], ybuf.at[1], r, sem).start()
        return c

    def drain(r, c):
        _row_copy(y_ref, 0, ybuf.at[0], 0, sem).wait()
        _row_copy(y_ref, 0, ybuf.at[1], 0, sem).wait()
        return c

    lax.fori_loop(0, rows, issue, 0)
    lax.fori_loop(0, rows, drain, 0)
    w = rw_ref[...]
    x = xm_ref[...] + w[:, 0:1] * ybuf[0] + w[:, 1:2] * ybuf[1]
    o_ref[...] = _rms(x, g_ref[...])


def _combine(dest3, rw, xm, g, y_rows, rows, nblk, blk0):
    return pl.pallas_call(
        functools.partial(_combine_kernel, rows=rows), grid=(nblk,),
        in_specs=[pl.BlockSpec((1, 1, 2 * rows), lambda i: (i, 0, 0), memory_space=pltpu.SMEM),
                  pl.BlockSpec((rows, 128), lambda i: (blk0 + i, 0)),
                  pl.BlockSpec((rows, D_MODEL), lambda i: (blk0 + i, 0)),
                  pl.BlockSpec((1, D_MODEL), lambda i: (0, 0)),
                  pl.BlockSpec(memory_space=pl.ANY)],
        out_specs=pl.BlockSpec((rows, D_MODEL), lambda i: (i, 0)),
        out_shape=jax.ShapeDtypeStruct((nblk * rows, D_MODEL), F32),
        scratch_shapes=[pltpu.VMEM((2, rows, D_MODEL), F32), pltpu.SemaphoreType.DMA(())],
        compiler_params=_cparams(("arbitrary",)), name="moe_combine",
    )(dest3, rw, xm, g, y_rows)


def kernel(x_prompt, x_sample, cache_k, cache_v, state_ssm_re, state_ssm_im, g_mix_norm, w_in, ssm_a_re, ssm_a_im, ssm_log_dt, ssm_b_re, ssm_b_im, ssm_c_re, ssm_c_im, ssm_d, w_glu, g_ssm_out, g_attn_out, w_out, g_ffn_norm, w_router_group, b_router_group, w_router_expert, b_router_expert, w_expert_gate, w_expert_up, w_expert_down, g_final):
    row = lambda a: a.reshape(1, -1)
    half = HEAD_DIM // 2
    inv = ROPE_THETA ** (-jnp.arange(half, dtype=F32) / half)
    inv = jnp.concatenate([inv, inv]).reshape(1, HEAD_DIM)

    lam_r, lam_i, lams_r, lams_i, bb_r, bb_i = _ssm_params(ssm_a_re[0], ssm_a_im[0], ssm_log_dt[0], ssm_b_re[0], ssm_b_im[0])
    bm = jnp.concatenate([_block_diag_in(bb_r), _block_diag_in(bb_i)], axis=-1)
    cm = jnp.concatenate([_block_diag_out(ssm_c_re[0]), -_block_diag_out(ssm_c_im[0])], axis=1)
    dskip = row(ssm_d[0])
    bcast8 = lambda a: jnp.broadcast_to(a.reshape(1, -1), (N_SEG, N_GROUPS * SSM_STATE))
    lr8, li8, lsr8, lsi8 = bcast8(lam_r), bcast8(lam_i), bcast8(lams_r), bcast8(lams_i)

    wr = jnp.concatenate([w_router_group[0], w_router_expert[0].reshape(D_MODEL, N_EXPERTS)], axis=1)
    wr = jnp.pad(wr, ((0, 0), (0, 128 - wr.shape[1])))
    br = jnp.pad(jnp.concatenate([b_router_group[0], b_router_expert[0].reshape(-1)]), (0, 128 - 36)).reshape(1, 128)

    u_il, q, k, v = _inproj(x_prompt, row(g_mix_norm[0]), inv, w_in[0].astype(BF16))
    u_perm = u_il.reshape(BATCH, SEQ, SSM_WIDTH)
    bm16, cm16 = bm.astype(BF16), cm.astype(BF16)
    ends = _ssm_scan(u_perm, bm16, lr8, li8, final=False)
    z_perm, xend = _ssm_scan(u_perm, bm16, lr8, li8, final=True, e=ends, lsr8=lsr8, lsi8=lsi8, cm=cm16, dskip=dskip)
    mix_ssm = _glu(z_perm, w_glu[0].astype(BF16), row(g_ssm_out[0]))
    o_attn = _attention(q, k, v)
    xm_all, hp_all, lg_all = _outproj(
        x_prompt, mix_ssm.reshape(BATCH, SEG_LEN, N_SEG * SSM_WIDTH), o_attn, row(g_attn_out[0]),
        w_out[0].astype(BF16), row(g_ffn_norm[0]), wr.astype(BF16), br)

    xs = x_sample.reshape(DEC_BATCH, D_MODEL)
    proj = _s_inproj(xs, row(g_mix_norm[0]), w_in[0])
    us = proj[:, :SSM_WIDTH]
    qkv = proj[:, SSM_WIDTH:].reshape(DEC_BATCH, 3, N_HEADS, HEAD_DIM)
    zs, ns_r, ns_i = _s_ssm(us, state_ssm_re[0].reshape(DEC_BATCH, -1), state_ssm_im[0].reshape(DEC_BATCH, -1),
                            row(lam_r), row(lam_i), bm, cm, dskip)
    mix_ssm_s = _s_glu(zs, w_glu[0], row(g_ssm_out[0]))
    o_s, k_rot = _s_attention(qkv[:, 0], qkv[:, 1], qkv[:, 2], inv, cache_k[0], cache_v[0])
    xm_all, hp_all, lg_all = _s_outproj(
        xm_all, hp_all, lg_all, xs, mix_ssm_s, o_s.reshape(DEC_BATCH, ATTN_WIDTH), row(g_attn_out[0]),
        w_out[0], row(g_ffn_norm[0]), wr, br)

    ri, rw, cnt = _route(lg_all)
    counts = cnt[0, :N_EXPERTS].astype(I32)
    padded = (counts + MOE_BLK - 1) // MOE_BLK * MOE_BLK
    pend = jnp.cumsum(padded)
    poff = pend - padded
    dest = poff[ri[:, 0:2]] + ri[:, 2:4]
    blk_start = jnp.arange(N_MOE_BLOCKS, dtype=I32) * MOE_BLK
    block_e = jnp.minimum(jnp.searchsorted(pend, blk_start, side='right'), N_EXPERTS - 1).astype(I32)
    nvalid = jnp.clip(counts[block_e] - (blk_start - poff[block_e]), 0, MOE_BLK).astype(I32)
    dest3 = dest.reshape(N_TOK_BLOCKS, 1, 2 * TM)
    xs_sorted = _dispatch(dest3, hp_all, jnp.zeros((N_MOE_ROWS, D_MODEL // 2), U32))
    y_rows = _experts(block_e, nvalid, xs_sorted, w_expert_gate[0], w_expert_up[0], w_expert_down[0])
    gfin = row(g_final)
    y_prompt = _combine(dest3, rw, xm_all, gfin, y_rows, TM, T_PROMPT // TM, 0)
    dest_s = dest[T_PROMPT:T_PROMPT + DEC_BATCH].reshape(1, 1, 2 * DEC_BATCH)
    y_sample = _combine(dest_s, rw, xm_all, gfin, y_rows, DEC_BATCH, 1, T_PROMPT // DEC_BATCH)

    wb = min(CACHE_LEN, SEQ)
    kv_shape = (1, BATCH, wb, N_HEADS, HEAD_DIM)
    st_p = lambda a: a[:, :, N_SEG - 1, :].reshape(1, BATCH, N_GROUPS, SSM_STATE)
    st_s = lambda a: a.reshape(1, DEC_BATCH, N_GROUPS, SSM_STATE)
    kvs = lambda a: a.reshape(1, DEC_BATCH, 1, N_HEADS, HEAD_DIM)
    return (y_prompt.reshape(BATCH, SEQ, D_MODEL), y_sample.reshape(DEC_BATCH, 1, D_MODEL),
            k[:, SEQ - wb:].reshape(kv_shape), v[:, SEQ - wb:].reshape(kv_shape),
            st_p(xend[..., :CH_ST]), st_p(xend[..., CH_ST:]),
            kvs(k_rot), kvs(qkv[:, 2]), st_s(ns_r), st_s(ns_i))
```

```python
import functools
import math

import jax
import jax.numpy as jnp
from jax import lax
from jax.experimental import pallas as pl
from jax.experimental.pallas import tpu as pltpu

F32 = jnp.float32
BF16 = jnp.bfloat16
I32 = jnp.int32
U32 = jnp.uint32
HIGHEST = lax.Precision.HIGHEST

D_MODEL = 2048
BATCH = 2
SEQ = 4096
DEC_BATCH = 32
PAST_LEN = 8192
CACHE_LEN = 2048
SSM_WIDTH = 1024
ATTN_WIDTH = 1024
SSM_GROUP = 16
N_GROUPS = 64
SSM_STATE = 64
HEAD_DIM = 128
N_HEADS = 8
D_IN = 4096
BRANCHES = ((128, 1), (512, 4), (2048, 16))
BAND = 128
ROPE_THETA = 10000.0
N_EXPERT_GROUPS = 4
EXPERTS_PER_GROUP = 8
N_EXPERTS = 32
EXPERT_FF = 512
NORM_EPS = 1e-6

N_SEG = 8
SEG_LEN = SEQ // N_SEG
N_CHUNK = 4
CH_IN = SSM_WIDTH // N_CHUNK
CH_ST = N_GROUPS * SSM_STATE // N_CHUNK
SSM_TK = 64
TM = 256
TK = TM // N_SEG
TK_SHIFT = TK.bit_length() - 1
LANE = 128
T_PROMPT = BATCH * SEQ
T_REAL = T_PROMPT + DEC_BATCH
T_PAD = T_PROMPT + TM
N_TOK_BLOCKS = T_PAD // TM
MOE_BLK = 256
N_MOE_BLOCKS = -(-(2 * T_REAL) // MOE_BLK) + N_EXPERTS
N_MOE_ROWS = N_MOE_BLOCKS * MOE_BLK
VMEM_LIMIT = 56 * 1024 * 1024
NEG = -1e30


def _cparams(sem, vmem=VMEM_LIMIT):
    return pltpu.CompilerParams(dimension_semantics=sem, vmem_limit_bytes=vmem)


def _rms(x, g):
    return x * lax.rsqrt(jnp.mean(x * x, axis=-1, keepdims=True) + NORM_EPS) * g


def _gelu_tanh(y):
    return 0.5 * y * (1.0 + jnp.tanh(0.7978845608028654 * (y + 0.044715 * (y * y * y))))


def _sigmoid(x):
    return 1.0 / (1.0 + jnp.exp(-x))


def _rope_tables(pos, inv):
    ang = pos * inv
    lane = lax.broadcasted_iota(I32, ang.shape, 1)
    return jnp.cos(ang), jnp.where(lane < HEAD_DIM // 2, -jnp.sin(ang), jnp.sin(ang))


def _rope_heads(x, cos, sin):
    outs = []
    for h in range(x.shape[1] // HEAD_DIM):
        xh = x[:, h * HEAD_DIM:(h + 1) * HEAD_DIM]
        outs.append(xh * cos + pltpu.roll(xh, HEAD_DIM // 2, 1) * sin)
    return jnp.concatenate(outs, axis=1)


def _ssm_param_kernel(are_ref, aim_ref, ldt_ref, btr_ref, bti_ref,
                      lr_ref, li_ref, lsr_ref, lsi_ref, bbr_ref, bbi_ref):
    ar, ai = are_ref[...], aim_ref[...]
    dt = jnp.exp(ldt_ref[...])
    er, ei = ar * dt, ai * dt
    mag = jnp.exp(er)
    lr, li = mag * jnp.cos(ei), mag * jnp.sin(ei)
    lr_ref[...] = lr
    li_ref[...] = li
    mag_s = jnp.exp(er * SEG_LEN)
    lsr_ref[...] = mag_s * jnp.cos(ei * SEG_LEN)
    lsi_ref[...] = mag_s * jnp.sin(ei * SEG_LEN)
    xr, xi = lr - 1.0, li
    den = ar * ar + ai * ai
    cr = (xr * ar + xi * ai) / den
    ci = (xi * ar - xr * ai) / den
    btr, bti = btr_ref[...], bti_ref[...]
    bbr_ref[...] = cr[:, None, :] * btr - ci[:, None, :] * bti
    bbi_ref[...] = cr[:, None, :] * bti + ci[:, None, :] * btr


def _ssm_params(a_re, a_im, log_dt, b_re, b_im):
    g, n, p = N_GROUPS, SSM_STATE, SSM_GROUP
    gn = jax.ShapeDtypeStruct((g, n), F32)
    gpn = jax.ShapeDtypeStruct((g, p, n), F32)
    return pl.pallas_call(
        _ssm_param_kernel, out_shape=(gn, gn, gn, gn, gpn, gpn), name="ssm_params",
    )(a_re, a_im, log_dt.reshape(g, 1), b_re.transpose(0, 2, 1), b_im.transpose(0, 2, 1))


def _block_diag_in(m):
    rows = m.reshape(N_CHUNK, CH_IN, SSM_STATE)
    r = lax.broadcasted_iota(I32, (CH_IN, CH_ST), 0) // SSM_GROUP
    c = lax.broadcasted_iota(I32, (CH_IN, CH_ST), 1) // SSM_STATE
    return jnp.where((r == c)[None], jnp.tile(rows, (1, 1, 16)), 0.0)


def _block_diag_out(c):
    rows = c.transpose(0, 2, 1).reshape(N_CHUNK, CH_ST, SSM_GROUP)
    r = lax.broadcasted_iota(I32, (CH_ST, CH_IN), 0) // SSM_STATE
    k = lax.broadcasted_iota(I32, (CH_ST, CH_IN), 1) // SSM_GROUP
    return jnp.where((r == k)[None], jnp.tile(rows, (1, 1, 16)), 0.0)


def _inproj_kernel(x_ref, g_ref, inv_ref, w_ref, u_ref, q_ref, k_ref, v_ref, kn_ref, vn_ref):
    kb = pl.program_id(1)
    h = _rms(x_ref[0].reshape(TM, D_MODEL), g_ref[...]).astype(BF16)
    rid = lax.broadcasted_iota(I32, (TM, 1), 0)
    pos = ((rid >> TK_SHIFT) * SEG_LEN + kb * TK + (rid & (TK - 1))).astype(F32)
    cos, sin = _rope_tables(pos, inv_ref[...])
    w = SSM_WIDTH
    u = jnp.dot(h, w_ref[:, 0:w], preferred_element_type=F32)
    for s in range(N_SEG):
        for c in range(w // LANE):
            u_ref[0, c, pl.ds(s, TK, stride=N_SEG), :] = u[s * TK:(s + 1) * TK, c * LANE:(c + 1) * LANE]
    q_ref[0] = _rope_heads(jnp.dot(h, w_ref[:, w:2 * w], preferred_element_type=F32), cos, sin).reshape(N_SEG, TK, w)
    k = _rope_heads(jnp.dot(h, w_ref[:, 2 * w:3 * w], preferred_element_type=F32), cos, sin)
    v = jnp.dot(h, w_ref[:, 3 * w:4 * w], preferred_element_type=F32)
    k_ref[0] = k.reshape(N_SEG, TK, w)
    v_ref[0] = v.reshape(N_SEG, TK, w)
    kn_ref[0] = k[TM // 2:].reshape(N_SEG // 2, TK, w)
    vn_ref[0] = v[TM // 2:].reshape(N_SEG // 2, TK, w)


def _inproj(x4, g, inv, w_bf16):
    const = lambda b, k: (0, 0)
    blk4 = lambda nseg, width: pl.BlockSpec((1, nseg, TK, width), lambda b, k: (b, 0, k, 0))
    qkv = jax.ShapeDtypeStruct((BATCH, N_SEG, SEG_LEN, ATTN_WIDTH), F32)
    kvn = jax.ShapeDtypeStruct((BATCH, N_SEG // 2, SEG_LEN, ATTN_WIDTH), F32)
    return pl.pallas_call(
        _inproj_kernel, grid=(BATCH, SEG_LEN // TK),
        in_specs=[blk4(N_SEG, D_MODEL), pl.BlockSpec((1, D_MODEL), const),
                  pl.BlockSpec((1, HEAD_DIM), const), pl.BlockSpec((D_MODEL, D_IN), const)],
        out_specs=[pl.BlockSpec((1, SSM_WIDTH // LANE, TM, LANE), lambda b, k: (b, 0, k, 0)),
                   blk4(N_SEG, ATTN_WIDTH), blk4(N_SEG, ATTN_WIDTH), blk4(N_SEG, ATTN_WIDTH),
                   blk4(N_SEG // 2, ATTN_WIDTH), blk4(N_SEG // 2, ATTN_WIDTH)],
        out_shape=(jax.ShapeDtypeStruct((BATCH, SSM_WIDTH // LANE, SEQ, LANE), F32), qkv, qkv, qkv, kvn, kvn),
        compiler_params=_cparams(("arbitrary",) * 2), name="inproj",
    )(x4, g, inv, w_bf16)


def _ssm_scan_kernel(*refs, final):
    if final:
        (u_ref, bm_ref, lr_ref, li_ref, e_ref, lsr_ref, lsi_ref, cm_ref, d_ref,
         z_ref, xend_ref, bu_scr, xr_scr, xi_scr) = refs
    else:
        u_ref, bm_ref, lr_ref, li_ref, xend_ref, bu_scr, xr_scr, xi_scr = refs
    kb = pl.program_id(2)

    @pl.when(kb == 0)
    def _():
        if final:
            lsr, lsi = lsr_ref[0:1, :], lsi_ref[0:1, :]
            seg_id = lax.broadcasted_iota(I32, (N_SEG, CH_ST), 0)
            pr = pi = jnp.zeros((1, CH_ST), F32)
            xr0 = xi0 = jnp.zeros((N_SEG, CH_ST), F32)
            for s in range(1, N_SEG):
                er = e_ref[0, 0, s - 1:s, 0:CH_ST]
                ei = e_ref[0, 0, s - 1:s, CH_ST:2 * CH_ST]
                pr, pi = er + lsr * pr - lsi * pi, ei + lsr * pi + lsi * pr
                xr0 = jnp.where(seg_id == s, pr, xr0)
                xi0 = jnp.where(seg_id == s, pi, xi0)
            xr_scr[...] = xr0
            xi_scr[...] = xi0
        else:
            xr_scr[...] = jnp.zeros((N_SEG, CH_ST), F32)
            xi_scr[...] = jnp.zeros((N_SEG, CH_ST), F32)

    u = jnp.concatenate([u_ref[0, c] for c in range(CH_IN // LANE)], axis=1)
    bu_scr[...] = jnp.dot(u.astype(BF16), bm_ref[0], preferred_element_type=F32)
    lr, li = lr_ref[...], li_ref[...]

    def step(k, carry):
        xr, xi = carry
        r0 = pl.multiple_of(k * N_SEG, N_SEG)
        bur = bu_scr[pl.ds(r0, N_SEG), 0:CH_ST]
        bui = bu_scr[pl.ds(r0, N_SEG), CH_ST:2 * CH_ST]
        nr = lr * xr - li * xi + bur
        ni = lr * xi + li * xr + bui
        if final:
            bu_scr[pl.ds(r0, N_SEG), 0:CH_ST] = nr
            bu_scr[pl.ds(r0, N_SEG), CH_ST:2 * CH_ST] = ni
        return nr, ni

    xr, xi = lax.fori_loop(0, SSM_TK, step, (xr_scr[...], xi_scr[...]), unroll=2)
    xr_scr[...] = xr
    xi_scr[...] = xi
    if final:
        y = jnp.dot(bu_scr[...].astype(BF16), cm_ref[0], preferred_element_type=F32)
        z_ref[0] = _gelu_tanh(y + d_ref[...] * u)

    @pl.when(kb == pl.num_programs(2) - 1)
    def _():
        xend_ref[0, 0] = jnp.concatenate([xr, xi], axis=1)


def _ssm_scan(u_perm, bm, lr8, li8, final, e=None, lsr8=None, lsi8=None, cm=None, dskip=None):
    rows = SSM_TK * N_SEG
    nkb = SEG_LEN // SSM_TK
    in_specs = [pl.BlockSpec((1, CH_IN // LANE, rows, LANE), lambda b, j, k: (b, j, k, 0)),
                pl.BlockSpec((1, CH_IN, 2 * CH_ST), lambda b, j, k: (j, 0, 0)),
                pl.BlockSpec((N_SEG, CH_ST), lambda b, j, k: (0, j)),
                pl.BlockSpec((N_SEG, CH_ST), lambda b, j, k: (0, j))]
    args = [u_perm, bm, lr8, li8]
    xend_spec = pl.BlockSpec((1, 1, N_SEG, 2 * CH_ST), lambda b, j, k: (b, j, 0, 0))
    xend_shape = jax.ShapeDtypeStruct((BATCH, N_CHUNK, N_SEG, 2 * CH_ST), F32)
    if final:
        in_specs += [xend_spec,
                     pl.BlockSpec((N_SEG, CH_ST), lambda b, j, k: (0, j)),
                     pl.BlockSpec((N_SEG, CH_ST), lambda b, j, k: (0, j)),
                     pl.BlockSpec((1, 2 * CH_ST, CH_IN), lambda b, j, k: (j, 0, 0)),
                     pl.BlockSpec((1, CH_IN), lambda b, j, k: (0, j))]
        args += [e, lsr8, lsi8, cm, dskip]
        out_specs = [pl.BlockSpec((1, rows, CH_IN), lambda b, j, k: (b, k, j)), xend_spec]
        out_shape = (jax.ShapeDtypeStruct((BATCH, SEQ, SSM_WIDTH), F32), xend_shape)
    else:
        out_specs = xend_spec
        out_shape = xend_shape
    return pl.pallas_call(
        functools.partial(_ssm_scan_kernel, final=final), grid=(BATCH, N_CHUNK, nkb),
        in_specs=in_specs, out_specs=out_specs, out_shape=out_shape,
        scratch_shapes=[pltpu.VMEM((rows, 2 * CH_ST), F32),
                        pltpu.VMEM((N_SEG, CH_ST), F32), pltpu.VMEM((N_SEG, CH_ST), F32)],
        compiler_params=_cparams(("arbitrary",) * 3),
        name="ssm_scan_final" if final else "ssm_scan_ends",
    )(*args)


def _glu_kernel(z_ref, w_ref, g_ref, o_ref):
    z = z_ref[0]
    zz = z * _sigmoid(jnp.dot(z.astype(BF16), w_ref[...], preferred_element_type=F32))
    out = _rms(zz, g_ref[...])
    for c in range(SSM_WIDTH // LANE):
        o_ref[0, c] = out[:, c * LANE:(c + 1) * LANE]


def _glu(z, w_bf16, g):
    rows = 512
    return pl.pallas_call(
        _glu_kernel, grid=(BATCH, SEQ // rows),
        in_specs=[pl.BlockSpec((1, rows, SSM_WIDTH), lambda b, i: (b, i, 0)),
                  pl.BlockSpec((SSM_WIDTH, SSM_WIDTH), lambda b, i: (0, 0)),
                  pl.BlockSpec((1, SSM_WIDTH), lambda b, i: (0, 0))],
        out_specs=pl.BlockSpec((1, SSM_WIDTH // LANE, rows, LANE), lambda b, i: (b, 0, i, 0)),
        out_shape=jax.ShapeDtypeStruct((BATCH, SSM_WIDTH // LANE, SEQ, LANE), F32),
        compiler_params=_cparams(("arbitrary",) * 2), name="ssm_glu",
    )(z, w_bf16, g)


def _attn_kernel(q_ref, k_ref, v_ref, o_ref, o_scr, lse_scr):
    scale = HEAD_DIM ** -0.5
    qi = lax.broadcasted_iota(I32, (BAND, 2 * BAND), 0)
    kj = lax.broadcasted_iota(I32, (BAND, 2 * BAND), 1)
    in_prev = kj < BAND
    cur_ok = (kj >= BAND) & (kj - BAND <= qi)
    nt = (((1,), (1,)), ((), ()))
    for bi, (_, d) in enumerate(BRANCHES):
        nblk = SEQ // (BAND * d)
        shift = nblk.bit_length() - 1

        def tile(i, c, bi=bi, d=d, nblk=nblk, shift=shift):
            r = i >> shift
            ib = i & (nblk - 1)
            start = r + ib * (BAND * d)
            pstart = jnp.maximum(start - BAND * d, r)
            cur = pl.ds(start, BAND, stride=d)
            prev = pl.ds(pstart, BAND, stride=d)
            q = (q_ref[0, cur, :] * scale).astype(BF16)
            k2 = jnp.concatenate([k_ref[0, prev, :], k_ref[0, cur, :]], axis=0).astype(BF16)
            v2 = jnp.concatenate([v_ref[0, prev, :], v_ref[0, cur, :]], axis=0).astype(BF16)
            s = lax.dot_general(q, k2, nt, preferred_element_type=F32)
            prev_ok = in_prev & (kj - qi >= jnp.where(ib > 0, 0, 2 * BAND))
            s = jnp.where(cur_ok | prev_ok, s, NEG)
            m = jnp.max(s, axis=-1, keepdims=True)
            p = jnp.exp(s - m)
            l = jnp.sum(p, axis=-1, keepdims=True)
            o = jnp.dot(p.astype(BF16), v2, preferred_element_type=F32) * (1.0 / l)
            o_scr[bi, cur, :] = o
            lse_scr[bi, cur, :] = jnp.broadcast_to(m + jnp.log(l), (BAND, HEAD_DIM))
            return c

        lax.fori_loop(0, SEQ // BAND, tile, 0, unroll=4)

    rows = 512

    def merge(i, c):
        sl = pl.ds(pl.multiple_of(i * rows, rows), rows)
        l0, l1, l2 = lse_scr[0, sl, :], lse_scr[1, sl, :], lse_scr[2, sl, :]
        m = jnp.maximum(jnp.maximum(l0, l1), l2)
        e0, e1, e2 = jnp.exp(l0 - m), jnp.exp(l1 - m), jnp.exp(l2 - m)
        o_ref[0, sl, :] = (e0 * o_scr[0, sl, :] + e1 * o_scr[1, sl, :] + e2 * o_scr[2, sl, :]) / (e0 + e1 + e2)
        return c

    lax.fori_loop(0, SEQ // rows, merge, 0)


def _attention(q, k, v):
    spec = pl.BlockSpec((1, SEQ, HEAD_DIM), lambda b, h: (b, 0, h))
    return pl.pallas_call(
        _attn_kernel, grid=(BATCH, N_HEADS), in_specs=[spec, spec, spec], out_specs=spec,
        out_shape=jax.ShapeDtypeStruct((BATCH, SEQ, ATTN_WIDTH), F32),
        scratch_shapes=[pltpu.VMEM((3, SEQ, HEAD_DIM), F32), pltpu.VMEM((3, SEQ, HEAD_DIM), F32)],
        compiler_params=_cparams(("arbitrary",) * 2), name="dilated_attn",
    )(q, k, v)


def _outproj_kernel(x_ref, ms_ref, o_ref, ga_ref, w_ref, gf_ref, wr_ref, br_ref, xm_ref, lg_ref):
    is_prompt = pl.program_id(0) < T_PROMPT // TM

    @pl.when(is_prompt)
    def _():
        ms = jnp.concatenate(
            [jnp.concatenate([ms_ref[0, c, pl.ds(s, TK, stride=N_SEG), :] for c in range(SSM_WIDTH // LANE)], axis=1)
             for s in range(N_SEG)], axis=0).astype(BF16)
        ma = _rms(o_ref[0].reshape(TM, ATTN_WIDTH), ga_ref[...]).astype(BF16)
        xm = (x_ref[0].reshape(TM, D_MODEL) + jnp.dot(ms, w_ref[0:SSM_WIDTH, :], preferred_element_type=F32)
              + jnp.dot(ma, w_ref[SSM_WIDTH:, :], preferred_element_type=F32))
        xm_ref[...] = xm
        h = _rms(xm, gf_ref[...]).astype(BF16)
        lg_ref[...] = jnp.dot(h, wr_ref[...], preferred_element_type=F32) + br_ref[...]

    @pl.when(jnp.logical_not(is_prompt))
    def _():
        xm_ref[...] = jnp.zeros((TM, D_MODEL), F32)
        lg_ref[...] = jnp.zeros((TM, 128), F32)


def _outproj(x4, mix_ssm_il, o_attn4, g_attn, w_bf16, g_ffn, wr_bf16, br):
    nkb = SEG_LEN // TK

    def split(i):
        i = jnp.minimum(i, T_PROMPT // TM - 1)
        return i // nkb, i % nkb

    def blk4(i):
        b, k = split(i)
        return (b, 0, k, 0)

    row = lambda i: (i, 0)
    const = lambda i: (0, 0)
    return pl.pallas_call(
        _outproj_kernel, grid=(N_TOK_BLOCKS,),
        in_specs=[pl.BlockSpec((1, N_SEG, TK, D_MODEL), blk4),
                  pl.BlockSpec((1, SSM_WIDTH // LANE, TM, LANE), blk4),
                  pl.BlockSpec((1, N_SEG, TK, ATTN_WIDTH), blk4),
                  pl.BlockSpec((1, ATTN_WIDTH), const),
                  pl.BlockSpec((D_MODEL, D_MODEL), const),
                  pl.BlockSpec((1, D_MODEL), const),
                  pl.BlockSpec((D_MODEL, 128), const),
                  pl.BlockSpec((1, 128), const)],
        out_specs=[pl.BlockSpec((TM, D_MODEL), row), pl.BlockSpec((TM, 128), row)],
        out_shape=(jax.ShapeDtypeStruct((T_PAD, D_MODEL), F32), jax.ShapeDtypeStruct((T_PAD, 128), F32)),
        compiler_params=_cparams(("arbitrary",)), name="outproj",
    )(x4, mix_ssm_il, o_attn4, g_attn, w_bf16, g_ffn, wr_bf16, br)


def _s_inproj_kernel(x_ref, g_ref, w_ref, o_ref):
    h = _rms(x_ref[...], g_ref[...])
    o_ref[...] = jnp.dot(h, w_ref[...], preferred_element_type=F32, precision=HIGHEST)


def _s_inproj(x, g, w):
    tn = 1024
    return pl.pallas_call(
        _s_inproj_kernel, grid=(D_IN // tn,),
        in_specs=[pl.BlockSpec((DEC_BATCH, D_MODEL), lambda j: (0, 0)),
                  pl.BlockSpec((1, D_MODEL), lambda j: (0, 0)),
                  pl.BlockSpec((D_MODEL, tn), lambda j: (0, j))],
        out_specs=pl.BlockSpec((DEC_BATCH, tn), lambda j: (0, j)),
        out_shape=jax.ShapeDtypeStruct((DEC_BATCH, D_IN), F32),
        compiler_params=_cparams(("arbitrary",)), name="s_inproj",
    )(x, g, w)


def _s_ssm_kernel(u_ref, sr_ref, si_ref, lr_ref, li_ref, bm_ref, cm_ref, d_ref, z_ref, nr_ref, ni_ref):
    u = u_ref[...]
    bu = jnp.dot(u, bm_ref[0], preferred_element_type=F32, precision=HIGHEST)
    lr, li = lr_ref[...], li_ref[...]
    xr, xi = sr_ref[...], si_ref[...]
    nr = lr * xr - li * xi + bu[:, :CH_ST]
    ni = lr * xi + li * xr + bu[:, CH_ST:]
    nr_ref[...] = nr
    ni_ref[...] = ni
    y = jnp.dot(jnp.concatenate([nr, ni], axis=1), cm_ref[0], preferred_element_type=F32, precision=HIGHEST)
    z_ref[...] = _gelu_tanh(y + d_ref[...] * u)


def _s_ssm(u, sr, si, lr1, li1, bm, cm, dskip):
    st = jax.ShapeDtypeStruct((DEC_BATCH, N_GROUPS * SSM_STATE), F32)
    return pl.pallas_call(
        _s_ssm_kernel, grid=(N_CHUNK,),
        in_specs=[pl.BlockSpec((DEC_BATCH, CH_IN), lambda j: (0, j)),
                  pl.BlockSpec((DEC_BATCH, CH_ST), lambda j: (0, j)),
                  pl.BlockSpec((DEC_BATCH, CH_ST), lambda j: (0, j)),
                  pl.BlockSpec((1, CH_ST), lambda j: (0, j)),
                  pl.BlockSpec((1, CH_ST), lambda j: (0, j)),
                  pl.BlockSpec((1, CH_IN, 2 * CH_ST), lambda j: (j, 0, 0)),
                  pl.BlockSpec((1, 2 * CH_ST, CH_IN), lambda j: (j, 0, 0)),
                  pl.BlockSpec((1, CH_IN), lambda j: (0, j))],
        out_specs=[pl.BlockSpec((DEC_BATCH, CH_IN), lambda j: (0, j)),
                   pl.BlockSpec((DEC_BATCH, CH_ST), lambda j: (0, j)),
                   pl.BlockSpec((DEC_BATCH, CH_ST), lambda j: (0, j))],
        out_shape=(jax.ShapeDtypeStruct((DEC_BATCH, SSM_WIDTH), F32), st, st),
        compiler_params=_cparams(("arbitrary",)), name="s_ssm",
    )(u, sr, si, lr1, li1, bm, cm, dskip)


def _s_glu_kernel(z_ref, w_ref, g_ref, o_ref):
    z = z_ref[...]
    zz = z * _sigmoid(jnp.dot(z, w_ref[...], preferred_element_type=F32, precision=HIGHEST))
    o_ref[...] = _rms(zz, g_ref[...])


def _s_glu(z, w, g):
    return pl.pallas_call(
        _s_glu_kernel, out_shape=jax.ShapeDtypeStruct((DEC_BATCH, SSM_WIDTH), F32),
        compiler_params=pltpu.CompilerParams(vmem_limit_bytes=VMEM_LIMIT), name="s_glu",
    )(z, w, g)


def _s_attn_kernel(q_ref, k_ref, v_ref, inv_ref, k1_ref, v1_ref, k4_ref, v4_ref, k16_ref, v16_ref,
                   o_ref, kr_ref):
    scale = HEAD_DIM ** -0.5
    pos = jnp.full((N_HEADS, 1), float(PAST_LEN), F32)
    cos, sin = _rope_tables(pos, inv_ref[...])
    q = q_ref[...] * cos + pltpu.roll(q_ref[...], HEAD_DIM // 2, 1) * sin
    kn = k_ref[...] * cos + pltpu.roll(k_ref[...], HEAD_DIM // 2, 1) * sin
    vn = v_ref[...]
    kr_ref[...] = kn
    s0 = jnp.sum(q * kn, axis=-1, keepdims=True) * scale
    outs, lses = [], []
    for kc_ref, vc_ref in ((k1_ref, v1_ref), (k4_ref, v4_ref), (k16_ref, v16_ref)):
        s = jnp.sum(kc_ref[...] * q[None], axis=-1, keepdims=True) * scale
        m = jnp.maximum(jnp.max(s, axis=0), s0)
        p = jnp.exp(s - m[None])
        p0 = jnp.exp(s0 - m)
        l = jnp.sum(p, axis=0) + p0
        outs.append((jnp.sum(p * vc_ref[...], axis=0) + p0 * vn) / l)
        lses.append(m + jnp.log(l))
    mm = jnp.maximum(jnp.maximum(lses[0], lses[1]), lses[2])
    es = [jnp.exp(x - mm) for x in lses]
    o_ref[...] = (es[0] * outs[0] + es[1] * outs[1] + es[2] * outs[2]) / (es[0] + es[1] + es[2])


def _s_attention(q, k, v, inv, cache_k, cache_v):
    hd = (N_HEADS, HEAD_DIM)
    tok = pl.BlockSpec((None,) + hd, lambda b: (b, 0, 0))
    args, specs = [], []
    for _, d in BRANCHES:
        nrow = CACHE_LEN // d
        last = nrow // BAND - 1
        spec = pl.BlockSpec((None, BAND, None) + hd, lambda b, last=last: (b, last, 0, 0, 0))
        for c in (cache_k, cache_v):
            args.append(c.reshape(DEC_BATCH, nrow, d, *hd))
            specs.append(spec)
    out = jax.ShapeDtypeStruct((DEC_BATCH,) + hd, F32)
    return pl.pallas_call(
        _s_attn_kernel, grid=(DEC_BATCH,),
        in_specs=[tok, tok, tok, pl.BlockSpec((1, HEAD_DIM), lambda b: (0, 0))] + specs,
        out_specs=[tok, tok], out_shape=(out, out),
        compiler_params=_cparams(("arbitrary",)), name="s_attn",
    )(q, k, v, inv, *args)


def _s_outproj_kernel(xa_ref, lga_ref, x_ref, ms_ref, o_ref, ga_ref, w_ref, gf_ref, wr_ref, br_ref,
                      xm_ref, lg_ref):
    del xa_ref, lga_ref
    mix = jnp.concatenate([ms_ref[...], _rms(o_ref[...], ga_ref[...])], axis=1)
    xm = x_ref[...] + jnp.dot(mix, w_ref[...], preferred_element_type=F32, precision=HIGHEST)
    h = _rms(xm, gf_ref[...])
    lg = jnp.dot(h, wr_ref[...], preferred_element_type=F32, precision=HIGHEST) + br_ref[...]
    pad = TM - DEC_BATCH
    xm_ref[...] = jnp.concatenate([xm, jnp.zeros((pad, D_MODEL), F32)], axis=0)
    lg_ref[...] = jnp.concatenate([lg, jnp.zeros((pad, 128), F32)], axis=0)


def _s_outproj(xm_all, lg_all, x, mix_ssm, o_attn, g_attn, w, g_ffn, wr, br):
    last = lambda i: (N_TOK_BLOCKS - 1, 0)
    full = lambda shape: pl.BlockSpec(shape, lambda i: (0, 0))
    any_spec = pl.BlockSpec(memory_space=pl.ANY)
    return pl.pallas_call(
        _s_outproj_kernel, grid=(1,),
        in_specs=[any_spec, any_spec,
                  full((DEC_BATCH, D_MODEL)), full((DEC_BATCH, SSM_WIDTH)), full((DEC_BATCH, ATTN_WIDTH)),
                  full((1, ATTN_WIDTH)), full((D_MODEL, D_MODEL)), full((1, D_MODEL)),
                  full((D_MODEL, 128)), full((1, 128))],
        out_specs=[pl.BlockSpec((TM, D_MODEL), last), pl.BlockSpec((TM, 128), last)],
        out_shape=(jax.ShapeDtypeStruct(xm_all.shape, F32), jax.ShapeDtypeStruct(lg_all.shape, F32)),
        input_output_aliases={0: 0, 1: 1},
        compiler_params=_cparams(("arbitrary",)), name="s_outproj",
    )(xm_all, lg_all, x, mix_ssm, o_attn, g_attn, w, g_ffn, wr, br)


def _route_kernel(lg_ref, ri_ref, rw_ref, cnt_ref, carry):
    i = pl.program_id(0)

    @pl.when(i == 0)
    def _():
        carry[...] = jnp.zeros((1, 128), F32)

    x = lg_ref[...]
    lane = lax.broadcasted_iota(I32, (TM, 128), 1)
    row = lax.broadcasted_iota(I32, (TM, 128), 0) + i * TM
    valid = row < T_REAL
    big = jnp.int32(1 << 20)
    gmask = lane < N_EXPERT_GROUPS
    lgm = jnp.where(gmask, x, NEG)
    m = jnp.max(lgm, axis=-1, keepdims=True)
    gate1 = 1.0 / jnp.sum(jnp.where(gmask, jnp.exp(lgm - m), 0.0), axis=-1, keepdims=True)
    grp = jnp.min(jnp.where(gmask & (lgm == m), lane, big), axis=-1, keepdims=True)
    lo = N_EXPERT_GROUPS + EXPERTS_PER_GROUP * grp
    emask = (lane >= lo) & (lane < lo + EXPERTS_PER_GROUP)
    le = jnp.where(emask, x, NEG)
    t1 = jnp.max(le, axis=-1, keepdims=True)
    i1 = jnp.min(jnp.where(emask & (le == t1), lane, big), axis=-1, keepdims=True)
    emask2 = emask & (lane != i1)
    le2 = jnp.where(emask2, x, NEG)
    t2 = jnp.max(le2, axis=-1, keepdims=True)
    i2 = jnp.min(jnp.where(emask2 & (le2 == t2), lane, big), axis=-1, keepdims=True)
    e21 = jnp.exp(t2 - t1)
    w1 = gate1 / (1.0 + e21)
    w2 = gate1 * e21 / (1.0 + e21)
    eid1, eid2 = i1 - N_EXPERT_GROUPS, i2 - N_EXPERT_GROUPS
    oh1 = jnp.where(valid & (lane == eid1), 1.0, 0.0)
    oh2 = jnp.where(valid & (lane == eid2), 1.0, 0.0)
    a = oh1 + oh2
    rr = lax.broadcasted_iota(I32, (TM, TM), 0)
    cc = lax.broadcasted_iota(I32, (TM, TM), 1)
    before = jnp.where(cc < rr, 1.0, 0.0).astype(BF16)
    pre = jnp.dot(before, a.astype(BF16), preferred_element_type=F32) + carry[...]
    rank1 = jnp.sum(oh1 * pre, axis=-1, keepdims=True).astype(I32)
    rank2 = jnp.sum(oh2 * pre, axis=-1, keepdims=True).astype(I32)
    carry[...] = carry[...] + jnp.sum(a, axis=0, keepdims=True)
    cnt_ref[...] = jnp.broadcast_to(carry[...], (8, 128))
    zi = jnp.zeros((TM, 128), I32)
    ri = jnp.where(lane == 0, eid1, jnp.where(lane == 1, eid2, jnp.where(lane == 2, rank1, jnp.where(lane == 3, rank2, zi))))
    ri_ref[...] = jnp.where(valid, ri, zi)
    rw = jnp.where(lane == 0, w1, jnp.where(lane == 1, w2, 0.0))
    rw_ref[...] = jnp.where(valid, rw, 0.0)


def _route(logits):
    blk = pl.BlockSpec((TM, 128), lambda i: (i, 0))
    return pl.pallas_call(
        _route_kernel, grid=(N_TOK_BLOCKS,), in_specs=[blk],
        out_specs=[blk, blk, pl.BlockSpec((8, 128), lambda i: (0, 0))],
        out_shape=(jax.ShapeDtypeStruct((T_PAD, 128), I32), jax.ShapeDtypeStruct((T_PAD, 128), F32),
                   jax.ShapeDtypeStruct((8, 128), F32)),
        scratch_shapes=[pltpu.VMEM((1, 128), F32)],
        compiler_params=_cparams(("arbitrary",)), name="route",
    )(logits)


def _row_copy(src, s, dst, d, sem):
    return pltpu.make_async_copy(src.at[pl.ds(s, 1)], dst.at[pl.ds(d, 1)], sem)


def _invert_kernel(poff_ref, ri_ref, dest_ref, rt_ref):
    i = pl.program_id(0)

    @pl.when(i == 0)
    def _():
        def clear(p, c):
            rt_ref[p] = 0
            return c
        lax.fori_loop(0, N_MOE_ROWS, clear, 0, unroll=32)

    def put(r, c):
        t = i * TM + r
        for s in range(2):
            d = poff_ref[ri_ref[0, 0, 4 * r + s]] + ri_ref[0, 0, 4 * r + 2 + s]
            dest_ref[0, 0, 2 * r + s] = d
            rt_ref[d] = t
        return c

    def blank(r, c):
        dest_ref[0, 0, 2 * r] = 0
        dest_ref[0, 0, 2 * r + 1] = 0
        return c

    @pl.when(i < T_PROMPT // TM)
    def _():
        lax.fori_loop(0, TM, put, 0, unroll=8)

    @pl.when(i == T_PROMPT // TM)
    def _():
        lax.fori_loop(0, DEC_BATCH, put, 0, unroll=8)
        lax.fori_loop(DEC_BATCH, TM, blank, 0, unroll=8)


def _invert(poff, ri4):
    grid_spec = pltpu.PrefetchScalarGridSpec(
        num_scalar_prefetch=1, grid=(N_TOK_BLOCKS,),
        in_specs=[pl.BlockSpec((1, 1, 4 * TM), lambda i, po: (i, 0, 0), memory_space=pltpu.SMEM)],
        out_specs=[pl.BlockSpec((1, 1, 2 * TM), lambda i, po: (i, 0, 0), memory_space=pltpu.SMEM),
                   pl.BlockSpec(memory_space=pltpu.SMEM)])
    return pl.pallas_call(
        _invert_kernel, grid_spec=grid_spec,
        out_shape=(jax.ShapeDtypeStruct((N_TOK_BLOCKS, 1, 2 * TM), I32),
                   jax.ShapeDtypeStruct((N_MOE_ROWS,), I32)),
        compiler_params=_cparams(("arbitrary",)), name="moe_invert",
    )(poff, ri4)


def _expert_kernel(be_ref, nv_ref, rt_ref, rtn_ref, xm_ref, gf_ref, wg_ref, wu_ref, wd_ref, y_ref,
                   xbuf, sems, wgu_scr, wd_scr):
    i = pl.program_id(0)
    slot = i % 2
    last = N_MOE_BLOCKS - 1
    nxt = jnp.minimum(i + 1, last)
    has_rows = nv_ref[i] > 0
    fetch_next = (i < last) & (nv_ref[nxt] > 0)
    changed = (i == 0) | (be_ref[i] != be_ref[jnp.maximum(i - 1, 0)])

    @pl.when((i == 0) & has_rows)
    def _():
        def start(r, c):
            _row_copy(xm_ref, rt_ref[0, 0, r], xbuf.at[0], r, sems.at[0]).start()
            return c
        lax.fori_loop(0, MOE_BLK, start, 0, unroll=8)

    @pl.when(changed)
    def _():
        wgu_scr[:, 0:EXPERT_FF] = wg_ref[0].astype(BF16)
        wgu_scr[:, EXPERT_FF:] = wu_ref[0].astype(BF16)
        wd_scr[...] = wd_ref[0].astype(BF16)

    def compute(prefetch):
        def wait(r, c):
            _row_copy(xm_ref, 0, xbuf.at[slot], 0, sems.at[slot]).wait()
            return c
        lax.fori_loop(0, MOE_BLK, wait, 0, unroll=8)
        h = _rms(xbuf[slot], gf_ref[...]).astype(BF16)
        n_chunk = 2 * EXPERT_FF // 256
        per = MOE_BLK // 2
        parts = []
        for c in range(n_chunk):
            parts.append(jnp.dot(h, wgu_scr[:, c * 256:(c + 1) * 256], preferred_element_type=F32))
            if prefetch and c < 2:
                for r in range(c * per, (c + 1) * per):
                    _row_copy(xm_ref, rtn_ref[0, 0, r], xbuf.at[1 - slot], r, sems.at[1 - slot]).start()
        g = jnp.concatenate(parts[:n_chunk // 2], axis=1)
        u = jnp.concatenate(parts[n_chunk // 2:], axis=1)
        act = (g * _sigmoid(g) * u).astype(BF16)
        y_ref[...] = jnp.dot(act, wd_scr[...], preferred_element_type=F32)

    @pl.when(has_rows & fetch_next)
    def _():
        compute(True)

    @pl.when(has_rows & jnp.logical_not(fetch_next))
    def _():
        compute(False)

    @pl.when(jnp.logical_not(has_rows))
    def _():
        y_ref[...] = jnp.zeros((MOE_BLK, D_MODEL), F32)


def _experts(block_e, nvalid, row_tok3, xm_all, g_ffn, w_g, w_u, w_d):
    smem_blk = lambda f: pl.BlockSpec((1, 1, MOE_BLK), f, memory_space=pltpu.SMEM)
    grid_spec = pltpu.PrefetchScalarGridSpec(
        num_scalar_prefetch=2, grid=(N_MOE_BLOCKS,),
        in_specs=[smem_blk(lambda i, be, nv: (i, 0, 0)),
                  smem_blk(lambda i, be, nv: (jnp.minimum(i + 1, N_MOE_BLOCKS - 1), 0, 0)),
                  pl.BlockSpec(memory_space=pl.ANY),
                  pl.BlockSpec((1, D_MODEL), lambda i, be, nv: (0, 0)),
                  pl.BlockSpec((1, D_MODEL, EXPERT_FF), lambda i, be, nv: (be[i], 0, 0)),
                  pl.BlockSpec((1, D_MODEL, EXPERT_FF), lambda i, be, nv: (be[i], 0, 0)),
                  pl.BlockSpec((1, EXPERT_FF, D_MODEL), lambda i, be, nv: (be[i], 0, 0))],
        out_specs=pl.BlockSpec((MOE_BLK, D_MODEL), lambda i, be, nv: (i, 0)),
        scratch_shapes=[pltpu.VMEM((2, MOE_BLK, D_MODEL), F32), pltpu.SemaphoreType.DMA((2,)),
                        pltpu.VMEM((D_MODEL, 2 * EXPERT_FF), BF16), pltpu.VMEM((EXPERT_FF, D_MODEL), BF16)])
    return pl.pallas_call(
        _expert_kernel, grid_spec=grid_spec,
        out_shape=jax.ShapeDtypeStruct((N_MOE_ROWS, D_MODEL), F32),
        compiler_params=_cparams(("arbitrary",)), name="moe_experts",
    )(block_e, nvalid, row_tok3, row_tok3, xm_all, g_ffn, w_g, w_u, w_d)


def _combine_kernel(dest_ref, destn_ref, rw_ref, xm_ref, g_ref, y_ref, o_ref, ybuf, sems, *, rows, nblk):
    i = pl.program_id(0)
    slot = i % 2

    def gather(d_ref, s, wait):
        def body(r, c):
            for j in range(2):
                cp = _row_copy(y_ref, 0 if wait else d_ref[0, 0, 2 * r + j], ybuf.at[s, j], r, sems.at[s])
                cp.wait() if wait else cp.start()
            return c
        lax.fori_loop(0, rows, body, 0, unroll=8)

    @pl.when(i == 0)
    def _():
        gather(dest_ref, 0, wait=False)

    @pl.when(i + 1 < nblk)
    def _():
        gather(destn_ref, 1 - slot, wait=False)

    gather(dest_ref, slot, wait=True)
    w = rw_ref[...]
    x = xm_ref[...] + w[:, 0:1] * ybuf[slot, 0] + w[:, 1:2] * ybuf[slot, 1]
    o_ref[...] = _rms(x, g_ref[...]).reshape(o_ref.shape)


def _combine(dest3, rw, xm, g, y_rows, rows, nblk, blk0, out_spec, out_shape):
    smem_blk = lambda f: pl.BlockSpec((1, 1, 2 * rows), f, memory_space=pltpu.SMEM)
    return pl.pallas_call(
        functools.partial(_combine_kernel, rows=rows, nblk=nblk), grid=(nblk,),
        in_specs=[smem_blk(lambda i: (i, 0, 0)),
                  smem_blk(lambda i: (jnp.minimum(i + 1, nblk - 1), 0, 0)),
                  pl.BlockSpec((rows, 128), lambda i: (blk0 + i, 0)),
                  pl.BlockSpec((rows, D_MODEL), lambda i: (blk0 + i, 0)),
                  pl.BlockSpec((1, D_MODEL), lambda i: (0, 0)),
                  pl.BlockSpec(memory_space=pl.ANY)],
        out_specs=out_spec, out_shape=out_shape,
        scratch_shapes=[pltpu.VMEM((2, 2, rows, D_MODEL), F32), pltpu.SemaphoreType.DMA((2,))],
        compiler_params=_cparams(("arbitrary",)), name="moe_combine",
    )(dest3, dest3, rw, xm, g, y_rows)


def kernel(x_prompt, x_sample, cache_k, cache_v, state_ssm_re, state_ssm_im, g_mix_norm, w_in, ssm_a_re, ssm_a_im, ssm_log_dt, ssm_b_re, ssm_b_im, ssm_c_re, ssm_c_im, ssm_d, w_glu, g_ssm_out, g_attn_out, w_out, g_ffn_norm, w_router_group, b_router_group, w_router_expert, b_router_expert, w_expert_gate, w_expert_up, w_expert_down, g_final):
    row = lambda a: a.reshape(1, -1)
    half = HEAD_DIM // 2
    inv = ROPE_THETA ** (-jnp.arange(half, dtype=F32) / half)
    inv = jnp.concatenate([inv, inv]).reshape(1, HEAD_DIM)

    lam_r, lam_i, lams_r, lams_i, bb_r, bb_i = _ssm_params(ssm_a_re[0], ssm_a_im[0], ssm_log_dt[0], ssm_b_re[0], ssm_b_im[0])
    bm = jnp.concatenate([_block_diag_in(bb_r), _block_diag_in(bb_i)], axis=-1)
    cm = jnp.concatenate([_block_diag_out(ssm_c_re[0]), -_block_diag_out(ssm_c_im[0])], axis=1)
    dskip = row(ssm_d[0])
    bcast8 = lambda a: jnp.broadcast_to(a.reshape(1, -1), (N_SEG, N_GROUPS * SSM_STATE))
    lr8, li8, lsr8, lsi8 = bcast8(lam_r), bcast8(lam_i), bcast8(lams_r), bcast8(lams_i)

    wr = jnp.concatenate([w_router_group[0], w_router_expert[0].reshape(D_MODEL, N_EXPERTS)], axis=1)
    wr = jnp.pad(wr, ((0, 0), (0, 128 - wr.shape[1])))
    br = jnp.pad(jnp.concatenate([b_router_group[0], b_router_expert[0].reshape(-1)]), (0, 128 - 36)).reshape(1, 128)

    x4 = x_prompt.reshape(BATCH, N_SEG, SEG_LEN, D_MODEL)
    u_il, q4, k4, v4, k_new, v_new = _inproj(x4, row(g_mix_norm[0]), inv, w_in[0].astype(BF16))
    bm16, cm16 = bm.astype(BF16), cm.astype(BF16)
    ends = _ssm_scan(u_il, bm16, lr8, li8, final=False)
    z_il, xend = _ssm_scan(u_il, bm16, lr8, li8, final=True, e=ends, lsr8=lsr8, lsi8=lsi8, cm=cm16, dskip=dskip)
    mix_ssm = _glu(z_il, w_glu[0].astype(BF16), row(g_ssm_out[0]))
    flat = lambda a: a.reshape(BATCH, SEQ, ATTN_WIDTH)
    o_attn = _attention(flat(q4), flat(k4), flat(v4))
    xm_all, lg_all = _outproj(
        x4, mix_ssm, o_attn.reshape(BATCH, N_SEG, SEG_LEN, ATTN_WIDTH), row(g_attn_out[0]),
        w_out[0].astype(BF16), row(g_ffn_norm[0]), wr.astype(BF16), br)

    xs = x_sample.reshape(DEC_BATCH, D_MODEL)
    proj = _s_inproj(xs, row(g_mix_norm[0]), w_in[0])
    us = proj[:, :SSM_WIDTH]
    qkv = proj[:, SSM_WIDTH:].reshape(DEC_BATCH, 3, N_HEADS, HEAD_DIM)
    zs, ns_r, ns_i = _s_ssm(us, state_ssm_re[0].reshape(DEC_BATCH, -1), state_ssm_im[0].reshape(DEC_BATCH, -1),
                            row(lam_r), row(lam_i), bm, cm, dskip)
    mix_ssm_s = _s_glu(zs, w_glu[0], row(g_ssm_out[0]))
    o_s, k_rot = _s_attention(qkv[:, 0], qkv[:, 1], qkv[:, 2], inv, cache_k[0], cache_v[0])
    xm_all, lg_all = _s_outproj(
        xm_all, lg_all, xs, mix_ssm_s, o_s.reshape(DEC_BATCH, ATTN_WIDTH), row(g_attn_out[0]),
        w_out[0], row(g_ffn_norm[0]), wr, br)

    ri, rw, cnt = _route(lg_all)
    counts = cnt[0, :N_EXPERTS].astype(I32)
    padded = (counts + MOE_BLK - 1) // MOE_BLK * MOE_BLK
    pend = jnp.cumsum(padded)
    poff = pend - padded
    blk_start = jnp.arange(N_MOE_BLOCKS, dtype=I32) * MOE_BLK
    block_e = jnp.minimum(jnp.sum(pend[None, :] <= blk_start[:, None], axis=1), N_EXPERTS - 1).astype(I32)
    nvalid = jnp.clip(counts[block_e] - (blk_start - poff[block_e]), 0, MOE_BLK).astype(I32)
    dest3, row_tok = _invert(poff, ri[:, 0:4].reshape(N_TOK_BLOCKS, 1, 4 * TM))
    y_rows = _experts(block_e, nvalid, row_tok.reshape(N_MOE_BLOCKS, 1, MOE_BLK), xm_all, row(g_ffn_norm[0]),
                      w_expert_gate[0], w_expert_up[0], w_expert_down[0])
    gfin = row(g_final)
    nkb = SEG_LEN // TK
    y_prompt = _combine(dest3, rw, xm_all, gfin, y_rows, TM, T_PROMPT // TM, 0,
                        pl.BlockSpec((1, N_SEG, TK, D_MODEL), lambda i: (i // nkb, 0, i % nkb, 0)),
                        jax.ShapeDtypeStruct((BATCH, N_SEG, SEG_LEN, D_MODEL), F32))
    dest_s = dest3[N_TOK_BLOCKS - 1:, :, :2 * DEC_BATCH]
    y_sample = _combine(dest_s, rw, xm_all, gfin, y_rows, DEC_BATCH, 1, T_PROMPT // DEC_BATCH,
                        pl.BlockSpec((DEC_BATCH, D_MODEL), lambda i: (0, 0)),
                        jax.ShapeDtypeStruct((DEC_BATCH, D_MODEL), F32))

    kv_shape = (1, BATCH, CACHE_LEN, N_HEADS, HEAD_DIM)
    st_p = lambda a: a[:, :, N_SEG - 1, :].reshape(1, BATCH, N_GROUPS, SSM_STATE)
    st_s = lambda a: a.reshape(1, DEC_BATCH, N_GROUPS, SSM_STATE)
    kvs = lambda a: a.reshape(1, DEC_BATCH, 1, N_HEADS, HEAD_DIM)
    return (y_prompt.reshape(BATCH, SEQ, D_MODEL), y_sample.reshape(DEC_BATCH, 1, D_MODEL),
            k_new.reshape(kv_shape), v_new.reshape(kv_shape),
            st_p(xend[..., :CH_ST]), st_p(xend[..., CH_ST:]),
            kvs(k_rot), kvs(qkv[:, 2]), st_s(ns_r), st_s(ns_i))
```

```python
import functools
import math

import jax
import jax.numpy as jnp
from jax import lax
from jax.experimental import pallas as pl
from jax.experimental.pallas import tpu as pltpu

F32 = jnp.float32
BF16 = jnp.bfloat16
I32 = jnp.int32
U32 = jnp.uint32
HIGHEST = lax.Precision.HIGHEST

D_MODEL = 2048
BATCH = 2
SEQ = 4096
DEC_BATCH = 32
PAST_LEN = 8192
CACHE_LEN = 2048
SSM_WIDTH = 1024
ATTN_WIDTH = 1024
SSM_GROUP = 16
N_GROUPS = 64
SSM_STATE = 64
HEAD_DIM = 128
N_HEADS = 8
D_IN = 4096
BRANCHES = ((128, 1), (512, 4), (2048, 16))
BAND = 128
ROPE_THETA = 10000.0
N_EXPERT_GROUPS = 4
EXPERTS_PER_GROUP = 8
N_EXPERTS = 32
EXPERT_FF = 512
NORM_EPS = 1e-6

N_SEG = 8
SEG_LEN = SEQ // N_SEG
N_CHUNK = 4
CH_IN = SSM_WIDTH // N_CHUNK
CH_ST = N_GROUPS * SSM_STATE // N_CHUNK
SSM_TK = 128
TM = 256
TK = TM // N_SEG
TK_SHIFT = TK.bit_length() - 1
LANE = 128
T_PROMPT = BATCH * SEQ
T_REAL = T_PROMPT + DEC_BATCH
T_PAD = T_PROMPT + TM
N_TOK_BLOCKS = T_PAD // TM
MOE_BLK = 256
N_MOE_BLOCKS = -(-(2 * T_REAL) // MOE_BLK) + N_EXPERTS
N_MOE_ROWS = N_MOE_BLOCKS * MOE_BLK
VMEM_LIMIT = 56 * 1024 * 1024
NEG = -1e30
LOG2_E = 1.4426950408889634
GATHER_PRIORITY = 1
GATHER_DEPTH = 2


def _cparams(sem, vmem=VMEM_LIMIT):
    return pltpu.CompilerParams(dimension_semantics=sem, vmem_limit_bytes=vmem)


def _rms(x, g):
    return x * lax.rsqrt(jnp.mean(x * x, axis=-1, keepdims=True) + NORM_EPS) * g


def _gelu_tanh(y):
    return 0.5 * y * (1.0 + jnp.tanh(0.7978845608028654 * (y + 0.044715 * (y * y * y))))


def _sigmoid(x):
    return 1.0 / (1.0 + jnp.exp(-x))


def _rope_tables(pos, inv):
    ang = pos * inv
    lane = lax.broadcasted_iota(I32, ang.shape, 1)
    return jnp.cos(ang), jnp.where(lane < HEAD_DIM // 2, -jnp.sin(ang), jnp.sin(ang))


def _rope_heads(x, cos, sin):
    outs = []
    for h in range(x.shape[1] // HEAD_DIM):
        xh = x[:, h * HEAD_DIM:(h + 1) * HEAD_DIM]
        outs.append(xh * cos + pltpu.roll(xh, HEAD_DIM // 2, 1) * sin)
    return jnp.concatenate(outs, axis=1)


def _ssm_param_kernel(are_ref, aim_ref, ldt_ref, btr_ref, bti_ref,
                      lr_ref, li_ref, lsr_ref, lsi_ref, bbr_ref, bbi_ref):
    ar, ai = are_ref[...], aim_ref[...]
    dt = jnp.exp(ldt_ref[...])
    er, ei = ar * dt, ai * dt
    mag = jnp.exp(er)
    lr, li = mag * jnp.cos(ei), mag * jnp.sin(ei)
    lr_ref[...] = lr
    li_ref[...] = li
    mag_s = jnp.exp(er * SEG_LEN)
    lsr_ref[...] = mag_s * jnp.cos(ei * SEG_LEN)
    lsi_ref[...] = mag_s * jnp.sin(ei * SEG_LEN)
    xr, xi = lr - 1.0, li
    den = ar * ar + ai * ai
    cr = (xr * ar + xi * ai) / den
    ci = (xi * ar - xr * ai) / den
    btr, bti = btr_ref[...], bti_ref[...]
    bbr_ref[...] = cr[:, None, :] * btr - ci[:, None, :] * bti
    bbi_ref[...] = cr[:, None, :] * bti + ci[:, None, :] * btr


def _ssm_params(a_re, a_im, log_dt, b_re, b_im):
    g, n, p = N_GROUPS, SSM_STATE, SSM_GROUP
    gn = jax.ShapeDtypeStruct((g, n), F32)
    gpn = jax.ShapeDtypeStruct((g, p, n), F32)
    return pl.pallas_call(
        _ssm_param_kernel, out_shape=(gn, gn, gn, gn, gpn, gpn), name="ssm_params",
    )(a_re, a_im, log_dt.reshape(g, 1), b_re.transpose(0, 2, 1), b_im.transpose(0, 2, 1))


def _block_diag_in(m):
    rows = m.reshape(N_CHUNK, CH_IN, SSM_STATE)
    r = lax.broadcasted_iota(I32, (CH_IN, CH_ST), 0) // SSM_GROUP
    c = lax.broadcasted_iota(I32, (CH_IN, CH_ST), 1) // SSM_STATE
    return jnp.where((r == c)[None], jnp.tile(rows, (1, 1, 16)), 0.0)


def _block_diag_out(c):
    rows = c.transpose(0, 2, 1).reshape(N_CHUNK, CH_ST, SSM_GROUP)
    r = lax.broadcasted_iota(I32, (CH_ST, CH_IN), 0) // SSM_STATE
    k = lax.broadcasted_iota(I32, (CH_ST, CH_IN), 1) // SSM_GROUP
    return jnp.where((r == k)[None], jnp.tile(rows, (1, 1, 16)), 0.0)


def _inproj_kernel(x_ref, g_ref, inv_ref, w_ref, u_ref, q_ref, k_ref, v_ref, kn_ref, vn_ref):
    kb = pl.program_id(1)
    h = _rms(x_ref[0].reshape(TM, D_MODEL), g_ref[...]).astype(BF16)
    rid = lax.broadcasted_iota(I32, (TM, 1), 0)
    pos = ((rid >> TK_SHIFT) * SEG_LEN + kb * TK + (rid & (TK - 1))).astype(F32)
    cos, sin = _rope_tables(pos, inv_ref[...])
    w = SSM_WIDTH
    u = jnp.dot(h, w_ref[:, 0:w], preferred_element_type=F32)
    for s in range(N_SEG):
        for c in range(w // LANE):
            u_ref[0, c, pl.ds(s, TK, stride=N_SEG), :] = u[s * TK:(s + 1) * TK, c * LANE:(c + 1) * LANE]
    q_ref[0] = _rope_heads(jnp.dot(h, w_ref[:, w:2 * w], preferred_element_type=F32), cos, sin).reshape(N_SEG, TK, w)
    k = _rope_heads(jnp.dot(h, w_ref[:, 2 * w:3 * w], preferred_element_type=F32), cos, sin)
    v = jnp.dot(h, w_ref[:, 3 * w:4 * w], preferred_element_type=F32)
    k_ref[0] = k.reshape(N_SEG, TK, w)
    v_ref[0] = v.reshape(N_SEG, TK, w)
    kn_ref[0] = k[TM // 2:].reshape(N_SEG // 2, TK, w)
    vn_ref[0] = v[TM // 2:].reshape(N_SEG // 2, TK, w)


def _inproj(x4, g, inv, w_bf16):
    const = lambda b, k: (0, 0)
    blk4 = lambda nseg, width: pl.BlockSpec((1, nseg, TK, width), lambda b, k: (b, 0, k, 0))
    qkv = jax.ShapeDtypeStruct((BATCH, N_SEG, SEG_LEN, ATTN_WIDTH), F32)
    kvn = jax.ShapeDtypeStruct((BATCH, N_SEG // 2, SEG_LEN, ATTN_WIDTH), F32)
    return pl.pallas_call(
        _inproj_kernel, grid=(BATCH, SEG_LEN // TK),
        in_specs=[blk4(N_SEG, D_MODEL), pl.BlockSpec((1, D_MODEL), const),
                  pl.BlockSpec((1, HEAD_DIM), const), pl.BlockSpec((D_MODEL, D_IN), const)],
        out_specs=[pl.BlockSpec((1, SSM_WIDTH // LANE, TM, LANE), lambda b, k: (b, 0, k, 0)),
                   blk4(N_SEG, ATTN_WIDTH), blk4(N_SEG, ATTN_WIDTH), blk4(N_SEG, ATTN_WIDTH),
                   blk4(N_SEG // 2, ATTN_WIDTH), blk4(N_SEG // 2, ATTN_WIDTH)],
        out_shape=(jax.ShapeDtypeStruct((BATCH, SSM_WIDTH // LANE, SEQ, LANE), F32), qkv, qkv, qkv, kvn, kvn),
        compiler_params=_cparams(("arbitrary",) * 2), name="inproj",
    )(x4, g, inv, w_bf16)


def _ssm_scan_kernel(*refs, final):
    if final:
        (u_ref, bm_ref, lr_ref, li_ref, e_ref, lsr_ref, lsi_ref, cm_ref, d_ref,
         z_ref, xend_ref, bu_scr, xr_scr, xi_scr) = refs
    else:
        u_ref, bm_ref, lr_ref, li_ref, xend_ref, bu_scr, xr_scr, xi_scr = refs
    kb = pl.program_id(2)

    @pl.when(kb == 0)
    def _():
        if final:
            lsr, lsi = lsr_ref[0:1, :], lsi_ref[0:1, :]
            seg_id = lax.broadcasted_iota(I32, (N_SEG, CH_ST), 0)
            pr = pi = jnp.zeros((1, CH_ST), F32)
            xr0 = xi0 = jnp.zeros((N_SEG, CH_ST), F32)
            for s in range(1, N_SEG):
                er = e_ref[0, 0, s - 1:s, 0:CH_ST]
                ei = e_ref[0, 0, s - 1:s, CH_ST:2 * CH_ST]
                pr, pi = er + lsr * pr - lsi * pi, ei + lsr * pi + lsi * pr
                xr0 = jnp.where(seg_id == s, pr, xr0)
                xi0 = jnp.where(seg_id == s, pi, xi0)
            xr_scr[...] = xr0
            xi_scr[...] = xi0
        else:
            xr_scr[...] = jnp.zeros((N_SEG, CH_ST), F32)
            xi_scr[...] = jnp.zeros((N_SEG, CH_ST), F32)

    u = jnp.concatenate([u_ref[0, c] for c in range(CH_IN // LANE)], axis=1)
    bu_scr[...] = jnp.dot(u.astype(BF16), bm_ref[0], preferred_element_type=F32)
    lr, li = lr_ref[...], li_ref[...]

    def step(k, carry):
        xr, xi = carry
        r0 = pl.multiple_of(k * N_SEG, N_SEG)
        bur = bu_scr[pl.ds(r0, N_SEG), 0:CH_ST]
        bui = bu_scr[pl.ds(r0, N_SEG), CH_ST:2 * CH_ST]
        nr = lr * xr - li * xi + bur
        ni = lr * xi + li * xr + bui
        if final:
            bu_scr[pl.ds(r0, N_SEG), 0:CH_ST] = nr
            bu_scr[pl.ds(r0, N_SEG), CH_ST:2 * CH_ST] = ni
        return nr, ni

    xr, xi = lax.fori_loop(0, SSM_TK, step, (xr_scr[...], xi_scr[...]), unroll=2)
    xr_scr[...] = xr
    xi_scr[...] = xi
    if final:
        y = jnp.dot(bu_scr[...].astype(BF16), cm_ref[0], preferred_element_type=F32)
        z = _gelu_tanh(y + d_ref[...] * u)
        for c in range(CH_IN // LANE):
            z_ref[0, c] = z[:, c * LANE:(c + 1) * LANE]

    @pl.when(kb == pl.num_programs(2) - 1)
    def _():
        xend_ref[0, 0] = jnp.concatenate([xr, xi], axis=1)


def _ssm_scan(u_perm, bm, lr8, li8, final, e=None, lsr8=None, lsi8=None, cm=None, dskip=None):
    rows = SSM_TK * N_SEG
    nkb = SEG_LEN // SSM_TK
    in_specs = [pl.BlockSpec((1, CH_IN // LANE, rows, LANE), lambda b, j, k: (b, j, k, 0)),
                pl.BlockSpec((1, CH_IN, 2 * CH_ST), lambda b, j, k: (j, 0, 0)),
                pl.BlockSpec((N_SEG, CH_ST), lambda b, j, k: (0, j)),
                pl.BlockSpec((N_SEG, CH_ST), lambda b, j, k: (0, j))]
    args = [u_perm, bm, lr8, li8]
    xend_spec = pl.BlockSpec((1, 1, N_SEG, 2 * CH_ST), lambda b, j, k: (b, j, 0, 0))
    xend_shape = jax.ShapeDtypeStruct((BATCH, N_CHUNK, N_SEG, 2 * CH_ST), F32)
    if final:
        in_specs += [xend_spec,
                     pl.BlockSpec((N_SEG, CH_ST), lambda b, j, k: (0, j)),
                     pl.BlockSpec((N_SEG, CH_ST), lambda b, j, k: (0, j)),
                     pl.BlockSpec((1, 2 * CH_ST, CH_IN), lambda b, j, k: (j, 0, 0)),
                     pl.BlockSpec((1, CH_IN), lambda b, j, k: (0, j))]
        args += [e, lsr8, lsi8, cm, dskip]
        out_specs = [pl.BlockSpec((1, CH_IN // LANE, rows, LANE), lambda b, j, k: (b, j, k, 0)), xend_spec]
        out_shape = (jax.ShapeDtypeStruct((BATCH, SSM_WIDTH // LANE, SEQ, LANE), F32), xend_shape)
    else:
        out_specs = xend_spec
        out_shape = xend_shape
    return pl.pallas_call(
        functools.partial(_ssm_scan_kernel, final=final), grid=(BATCH, N_CHUNK, nkb),
        in_specs=in_specs, out_specs=out_specs, out_shape=out_shape,
        scratch_shapes=[pltpu.VMEM((rows, 2 * CH_ST), F32),
                        pltpu.VMEM((N_SEG, CH_ST), F32), pltpu.VMEM((N_SEG, CH_ST), F32)],
        compiler_params=_cparams(("arbitrary",) * 3),
        name="ssm_scan_final" if final else "ssm_scan_ends",
    )(*args)


def _attn_kernel(q_ref, k_ref, v_ref, o_ref, o_scr, lse_scr, bias_scr):
    scale2 = HEAD_DIM ** -0.5 * LOG2_E
    qi = lax.broadcasted_iota(I32, (BAND, 2 * BAND), 0)
    kj = lax.broadcasted_iota(I32, (BAND, 2 * BAND), 1)
    cur_ok = (kj >= BAND) & (kj - BAND <= qi)
    prev_ok = (kj < BAND) & (kj >= qi)
    bias_scr[0] = jnp.where(cur_ok, 0.0, NEG)
    bias_scr[1] = jnp.where(cur_ok | prev_ok, 0.0, NEG)
    nt = (((1,), (1,)), ((), ()))
    for bi, (_, d) in enumerate(BRANCHES):
        nblk = SEQ // (BAND * d)
        shift = nblk.bit_length() - 1

        def tile(i, c, bi=bi, d=d, nblk=nblk, shift=shift):
            r = i >> shift
            ib = i & (nblk - 1)
            start = r + ib * (BAND * d)
            pstart = jnp.maximum(start - BAND * d, r)
            cur = pl.ds(start, BAND, stride=d)
            prev = pl.ds(pstart, BAND, stride=d)
            q = (q_ref[0, cur, :] * scale2).astype(BF16)
            k2 = jnp.concatenate([k_ref[0, prev, :], k_ref[0, cur, :]], axis=0).astype(BF16)
            v2 = jnp.concatenate([v_ref[0, prev, :], v_ref[0, cur, :]], axis=0).astype(BF16)
            s = lax.dot_general(q, k2, nt, preferred_element_type=F32) + bias_scr[jnp.minimum(ib, 1)]
            m = jnp.max(s, axis=-1, keepdims=True)
            p = jnp.exp2(s - m)
            l = jnp.sum(p, axis=-1, keepdims=True)
            o = jnp.dot(p.astype(BF16), v2, preferred_element_type=F32) * (1.0 / l)
            o_scr[bi, cur, :] = o
            lse_scr[bi, cur, :] = jnp.broadcast_to(m + jnp.log2(l), (BAND, HEAD_DIM))
            return c

        def softmax_pv(s, v, rows_out, bi=bi):
            m = jnp.max(s, axis=-1, keepdims=True)
            p = jnp.exp2(s - m)
            l = jnp.sum(p, axis=-1, keepdims=True)
            o_scr[bi, rows_out, :] = jnp.dot(p.astype(BF16), v, preferred_element_type=F32) * (1.0 / l)
            lse_scr[bi, rows_out, :] = jnp.broadcast_to(m + jnp.log2(l), (BAND, HEAD_DIM))

        def residue(r, c, d=d):
            sub = pl.ds(r, 2 * BAND, stride=d)
            q = (q_ref[0, sub, :] * scale2).astype(BF16)
            k = k_ref[0, sub, :].astype(BF16)
            v = v_ref[0, sub, :].astype(BF16)
            s0 = lax.dot_general(q[:BAND], k[:BAND], nt, preferred_element_type=F32) + bias_scr[1, :, BAND:]
            softmax_pv(s0, v[:BAND], pl.ds(r, BAND, stride=d))
            s1 = lax.dot_general(q[BAND:], k, nt, preferred_element_type=F32) + bias_scr[1]
            softmax_pv(s1, v, pl.ds(r + BAND * d, BAND, stride=d))
            return c

        if nblk == 2:
            lax.fori_loop(0, d, residue, 0, unroll=4)
        else:
            lax.fori_loop(0, SEQ // BAND, tile, 0, unroll=8)

    rows = 512

    def merge(i, c):
        sl = pl.ds(pl.multiple_of(i * rows, rows), rows)
        l0, l1, l2 = lse_scr[0, sl, :], lse_scr[1, sl, :], lse_scr[2, sl, :]
        m = jnp.maximum(jnp.maximum(l0, l1), l2)
        e0, e1, e2 = jnp.exp2(l0 - m), jnp.exp2(l1 - m), jnp.exp2(l2 - m)
        o_ref[0, sl, :] = ((e0 * o_scr[0, sl, :] + e1 * o_scr[1, sl, :] + e2 * o_scr[2, sl, :])
                           * (1.0 / (e0 + e1 + e2)))
        return c

    lax.fori_loop(0, SEQ // rows, merge, 0)


def _attention(q, k, v):
    spec = pl.BlockSpec((1, SEQ, HEAD_DIM), lambda b, h: (b, 0, h))
    return pl.pallas_call(
        _attn_kernel, grid=(BATCH, N_HEADS), in_specs=[spec, spec, spec], out_specs=spec,
        out_shape=jax.ShapeDtypeStruct((BATCH, SEQ, ATTN_WIDTH), F32),
        scratch_shapes=[pltpu.VMEM((3, SEQ, HEAD_DIM), F32), pltpu.VMEM((3, SEQ, HEAD_DIM), F32),
                        pltpu.VMEM((2, BAND, 2 * BAND), F32)],
        compiler_params=_cparams(("arbitrary",) * 2), name="dilated_attn",
    )(q, k, v)


def _outproj_kernel(x_ref, z_ref, o_ref, wglu_ref, gs_ref, ga_ref, w_ref, gf_ref, wr_ref, br_ref, xm_ref, lg_ref):
    is_prompt = pl.program_id(0) < T_PROMPT // TM

    @pl.when(is_prompt)
    def _():
        z = jnp.concatenate(
            [jnp.concatenate([z_ref[0, c, pl.ds(s, TK, stride=N_SEG), :] for c in range(SSM_WIDTH // LANE)], axis=1)
             for s in range(N_SEG)], axis=0)
        zz = z * _sigmoid(jnp.dot(z.astype(BF16), wglu_ref[...], preferred_element_type=F32))
        ms = _rms(zz, gs_ref[...]).astype(BF16)
        ma = _rms(o_ref[0].reshape(TM, ATTN_WIDTH), ga_ref[...]).astype(BF16)
        xm = (x_ref[0].reshape(TM, D_MODEL) + jnp.dot(ms, w_ref[0:SSM_WIDTH, :], preferred_element_type=F32)
              + jnp.dot(ma, w_ref[SSM_WIDTH:, :], preferred_element_type=F32))
        xm_ref[...] = xm
        h = _rms(xm, gf_ref[...]).astype(BF16)
        lg_ref[...] = jnp.dot(h, wr_ref[...], preferred_element_type=F32) + br_ref[...]

    @pl.when(jnp.logical_not(is_prompt))
    def _():
        xm_ref[...] = jnp.zeros((TM, D_MODEL), F32)
        lg_ref[...] = jnp.zeros((TM, 128), F32)


def _outproj(x4, z_slab, o_attn4, wglu_bf16, g_ssm, g_attn, w_bf16, g_ffn, wr_bf16, br):
    nkb = SEG_LEN // TK

    def split(i):
        i = jnp.minimum(i, T_PROMPT // TM - 1)
        return i // nkb, i % nkb

    def blk4(i):
        b, k = split(i)
        return (b, 0, k, 0)

    row = lambda i: (i, 0)
    const = lambda i: (0, 0)
    return pl.pallas_call(
        _outproj_kernel, grid=(N_TOK_BLOCKS,),
        in_specs=[pl.BlockSpec((1, N_SEG, TK, D_MODEL), blk4),
                  pl.BlockSpec((1, SSM_WIDTH // LANE, TM, LANE), blk4),
                  pl.BlockSpec((1, N_SEG, TK, ATTN_WIDTH), blk4),
                  pl.BlockSpec((SSM_WIDTH, SSM_WIDTH), const),
                  pl.BlockSpec((1, SSM_WIDTH), const),
                  pl.BlockSpec((1, ATTN_WIDTH), const),
                  pl.BlockSpec((D_MODEL, D_MODEL), const),
                  pl.BlockSpec((1, D_MODEL), const),
                  pl.BlockSpec((D_MODEL, 128), const),
                  pl.BlockSpec((1, 128), const)],
        out_specs=[pl.BlockSpec((TM, D_MODEL), row), pl.BlockSpec((TM, 128), row)],
        out_shape=(jax.ShapeDtypeStruct((T_PAD, D_MODEL), F32), jax.ShapeDtypeStruct((T_PAD, 128), F32)),
        compiler_params=_cparams(("arbitrary",)), name="outproj",
    )(x4, z_slab, o_attn4, wglu_bf16, g_ssm, g_attn, w_bf16, g_ffn, wr_bf16, br)


def _s_inproj_kernel(x_ref, g_ref, w_ref, o_ref):
    h = _rms(x_ref[...], g_ref[...])
    o_ref[...] = jnp.dot(h, w_ref[...], preferred_element_type=F32, precision=HIGHEST)


def _s_inproj(x, g, w):
    tn = 1024
    return pl.pallas_call(
        _s_inproj_kernel, grid=(D_IN // tn,),
        in_specs=[pl.BlockSpec((DEC_BATCH, D_MODEL), lambda j: (0, 0)),
                  pl.BlockSpec((1, D_MODEL), lambda j: (0, 0)),
                  pl.BlockSpec((D_MODEL, tn), lambda j: (0, j))],
        out_specs=pl.BlockSpec((DEC_BATCH, tn), lambda j: (0, j)),
        out_shape=jax.ShapeDtypeStruct((DEC_BATCH, D_IN), F32),
        compiler_params=_cparams(("arbitrary",)), name="s_inproj",
    )(x, g, w)


def _s_ssm_kernel(u_ref, sr_ref, si_ref, lr_ref, li_ref, bm_ref, cm_ref, d_ref, z_ref, nr_ref, ni_ref):
    u = u_ref[...]
    bu = jnp.dot(u, bm_ref[0], preferred_element_type=F32, precision=HIGHEST)
    lr, li = lr_ref[...], li_ref[...]
    xr, xi = sr_ref[...], si_ref[...]
    nr = lr * xr - li * xi + bu[:, :CH_ST]
    ni = lr * xi + li * xr + bu[:, CH_ST:]
    nr_ref[...] = nr
    ni_ref[...] = ni
    y = jnp.dot(jnp.concatenate([nr, ni], axis=1), cm_ref[0], preferred_element_type=F32, precision=HIGHEST)
    z_ref[...] = _gelu_tanh(y + d_ref[...] * u)


def _s_ssm(u, sr, si, lr1, li1, bm, cm, dskip):
    st = jax.ShapeDtypeStruct((DEC_BATCH, N_GROUPS * SSM_STATE), F32)
    return pl.pallas_call(
        _s_ssm_kernel, grid=(N_CHUNK,),
        in_specs=[pl.BlockSpec((DEC_BATCH, CH_IN), lambda j: (0, j)),
                  pl.BlockSpec((DEC_BATCH, CH_ST), lambda j: (0, j)),
                  pl.BlockSpec((DEC_BATCH, CH_ST), lambda j: (0, j)),
                  pl.BlockSpec((1, CH_ST), lambda j: (0, j)),
                  pl.BlockSpec((1, CH_ST), lambda j: (0, j)),
                  pl.BlockSpec((1, CH_IN, 2 * CH_ST), lambda j: (j, 0, 0)),
                  pl.BlockSpec((1, 2 * CH_ST, CH_IN), lambda j: (j, 0, 0)),
                  pl.BlockSpec((1, CH_IN), lambda j: (0, j))],
        out_specs=[pl.BlockSpec((DEC_BATCH, CH_IN), lambda j: (0, j)),
                   pl.BlockSpec((DEC_BATCH, CH_ST), lambda j: (0, j)),
                   pl.BlockSpec((DEC_BATCH, CH_ST), lambda j: (0, j))],
        out_shape=(jax.ShapeDtypeStruct((DEC_BATCH, SSM_WIDTH), F32), st, st),
        compiler_params=_cparams(("arbitrary",)), name="s_ssm",
    )(u, sr, si, lr1, li1, bm, cm, dskip)


def _s_glu_kernel(z_ref, w_ref, g_ref, o_ref):
    z = z_ref[...]
    zz = z * _sigmoid(jnp.dot(z, w_ref[...], preferred_element_type=F32, precision=HIGHEST))
    o_ref[...] = _rms(zz, g_ref[...])


def _s_glu(z, w, g):
    return pl.pallas_call(
        _s_glu_kernel, out_shape=jax.ShapeDtypeStruct((DEC_BATCH, SSM_WIDTH), F32),
        compiler_params=pltpu.CompilerParams(vmem_limit_bytes=VMEM_LIMIT), name="s_glu",
    )(z, w, g)


def _s_attn_kernel(q_ref, k_ref, v_ref, inv_ref, k1_ref, v1_ref, k4_ref, v4_ref, k16_ref, v16_ref,
                   o_ref, kr_ref):
    scale = HEAD_DIM ** -0.5
    pos = jnp.full((N_HEADS, 1), float(PAST_LEN), F32)
    cos, sin = _rope_tables(pos, inv_ref[...])
    q = q_ref[...] * cos + pltpu.roll(q_ref[...], HEAD_DIM // 2, 1) * sin
    kn = k_ref[...] * cos + pltpu.roll(k_ref[...], HEAD_DIM // 2, 1) * sin
    vn = v_ref[...]
    kr_ref[...] = kn
    s0 = jnp.sum(q * kn, axis=-1, keepdims=True) * scale
    outs, lses = [], []
    for kc_ref, vc_ref in ((k1_ref, v1_ref), (k4_ref, v4_ref), (k16_ref, v16_ref)):
        s = jnp.sum(kc_ref[...] * q[None], axis=-1, keepdims=True) * scale
        m = jnp.maximum(jnp.max(s, axis=0), s0)
        p = jnp.exp(s - m[None])
        p0 = jnp.exp(s0 - m)
        l = jnp.sum(p, axis=0) + p0
        outs.append((jnp.sum(p * vc_ref[...], axis=0) + p0 * vn) / l)
        lses.append(m + jnp.log(l))
    mm = jnp.maximum(jnp.maximum(lses[0], lses[1]), lses[2])
    es = [jnp.exp(x - mm) for x in lses]
    o_ref[...] = (es[0] * outs[0] + es[1] * outs[1] + es[2] * outs[2]) / (es[0] + es[1] + es[2])


def _s_attention(q, k, v, inv, cache_k, cache_v):
    hd = (N_HEADS, HEAD_DIM)
    tok = pl.BlockSpec((None,) + hd, lambda b: (b, 0, 0))
    args, specs = [], []
    for _, d in BRANCHES:
        nrow = CACHE_LEN // d
        last = nrow // BAND - 1
        spec = pl.BlockSpec((None, BAND, None) + hd, lambda b, last=last: (b, last, 0, 0, 0))
        for c in (cache_k, cache_v):
            args.append(c.reshape(DEC_BATCH, nrow, d, *hd))
            specs.append(spec)
    out = jax.ShapeDtypeStruct((DEC_BATCH,) + hd, F32)
    return pl.pallas_call(
        _s_attn_kernel, grid=(DEC_BATCH,),
        in_specs=[tok, tok, tok, pl.BlockSpec((1, HEAD_DIM), lambda b: (0, 0))] + specs,
        out_specs=[tok, tok], out_shape=(out, out),
        compiler_params=_cparams(("arbitrary",)), name="s_attn",
    )(q, k, v, inv, *args)


def _s_outproj_kernel(xa_ref, lga_ref, x_ref, ms_ref, o_ref, ga_ref, w_ref, gf_ref, wr_ref, br_ref,
                      xm_ref, lg_ref):
    del xa_ref, lga_ref
    mix = jnp.concatenate([ms_ref[...], _rms(o_ref[...], ga_ref[...])], axis=1)
    xm = x_ref[...] + jnp.dot(mix, w_ref[...], preferred_element_type=F32, precision=HIGHEST)
    h = _rms(xm, gf_ref[...])
    lg = jnp.dot(h, wr_ref[...], preferred_element_type=F32, precision=HIGHEST) + br_ref[...]
    pad = TM - DEC_BATCH
    xm_ref[...] = jnp.concatenate([xm, jnp.zeros((pad, D_MODEL), F32)], axis=0)
    lg_ref[...] = jnp.concatenate([lg, jnp.zeros((pad, 128), F32)], axis=0)


def _s_outproj(xm_all, lg_all, x, mix_ssm, o_attn, g_attn, w, g_ffn, wr, br):
    last = lambda i: (N_TOK_BLOCKS - 1, 0)
    full = lambda shape: pl.BlockSpec(shape, lambda i: (0, 0))
    any_spec = pl.BlockSpec(memory_space=pl.ANY)
    return pl.pallas_call(
        _s_outproj_kernel, grid=(1,),
        in_specs=[any_spec, any_spec,
                  full((DEC_BATCH, D_MODEL)), full((DEC_BATCH, SSM_WIDTH)), full((DEC_BATCH, ATTN_WIDTH)),
                  full((1, ATTN_WIDTH)), full((D_MODEL, D_MODEL)), full((1, D_MODEL)),
                  full((D_MODEL, 128)), full((1, 128))],
        out_specs=[pl.BlockSpec((TM, D_MODEL), last), pl.BlockSpec((TM, 128), last)],
        out_shape=(jax.ShapeDtypeStruct(xm_all.shape, F32), jax.ShapeDtypeStruct(lg_all.shape, F32)),
        input_output_aliases={0: 0, 1: 1},
        compiler_params=_cparams(("arbitrary",)), name="s_outproj",
    )(xm_all, lg_all, x, mix_ssm, o_attn, g_attn, w, g_ffn, wr, br)


def _route_kernel(lg_ref, ri_ref, rw_ref, cnt_ref, carry):
    i = pl.program_id(0)

    @pl.when(i == 0)
    def _():
        carry[...] = jnp.zeros((1, 128), F32)

    x = lg_ref[...]
    lane = lax.broadcasted_iota(I32, (TM, 128), 1)
    row = lax.broadcasted_iota(I32, (TM, 128), 0) + i * TM
    valid = row < T_REAL
    big = jnp.int32(1 << 20)
    gmask = lane < N_EXPERT_GROUPS
    lgm = jnp.where(gmask, x, NEG)
    m = jnp.max(lgm, axis=-1, keepdims=True)
    gate1 = 1.0 / jnp.sum(jnp.where(gmask, jnp.exp(lgm - m), 0.0), axis=-1, keepdims=True)
    grp = jnp.min(jnp.where(gmask & (lgm == m), lane, big), axis=-1, keepdims=True)
    lo = N_EXPERT_GROUPS + EXPERTS_PER_GROUP * grp
    emask = (lane >= lo) & (lane < lo + EXPERTS_PER_GROUP)
    le = jnp.where(emask, x, NEG)
    t1 = jnp.max(le, axis=-1, keepdims=True)
    i1 = jnp.min(jnp.where(emask & (le == t1), lane, big), axis=-1, keepdims=True)
    emask2 = emask & (lane != i1)
    le2 = jnp.where(emask2, x, NEG)
    t2 = jnp.max(le2, axis=-1, keepdims=True)
    i2 = jnp.min(jnp.where(emask2 & (le2 == t2), lane, big), axis=-1, keepdims=True)
    e21 = jnp.exp(t2 - t1)
    w1 = gate1 / (1.0 + e21)
    w2 = gate1 * e21 / (1.0 + e21)
    eid1, eid2 = i1 - N_EXPERT_GROUPS, i2 - N_EXPERT_GROUPS
    oh1 = jnp.where(valid & (lane == eid1), 1.0, 0.0)
    oh2 = jnp.where(valid & (lane == eid2), 1.0, 0.0)
    a = oh1 + oh2
    rr = lax.broadcasted_iota(I32, (TM, TM), 0)
    cc = lax.broadcasted_iota(I32, (TM, TM), 1)
    before = jnp.where(cc < rr, 1.0, 0.0).astype(BF16)
    pre = jnp.dot(before, a.astype(BF16), preferred_element_type=F32) + carry[...]
    rank1 = jnp.sum(oh1 * pre, axis=-1, keepdims=True).astype(I32)
    rank2 = jnp.sum(oh2 * pre, axis=-1, keepdims=True).astype(I32)
    carry[...] = carry[...] + jnp.sum(a, axis=0, keepdims=True)
    cnt_ref[...] = jnp.broadcast_to(carry[...], (8, 128))
    zi = jnp.zeros((TM, 128), I32)
    ri = jnp.where(lane == 0, eid1, jnp.where(lane == 1, eid2, jnp.where(lane == 2, rank1, jnp.where(lane == 3, rank2, zi))))
    ri_ref[...] = jnp.where(valid, ri, zi)
    rw = jnp.where(lane == 0, w1, jnp.where(lane == 1, w2, 0.0))
    rw_ref[...] = jnp.where(valid, rw, 0.0)


def _route(logits):
    blk = pl.BlockSpec((TM, 128), lambda i: (i, 0))
    return pl.pallas_call(
        _route_kernel, grid=(N_TOK_BLOCKS,), in_specs=[blk],
        out_specs=[blk, blk, pl.BlockSpec((8, 128), lambda i: (0, 0))],
        out_shape=(jax.ShapeDtypeStruct((T_PAD, 128), I32), jax.ShapeDtypeStruct((T_PAD, 128), F32),
                   jax.ShapeDtypeStruct((8, 128), F32)),
        scratch_shapes=[pltpu.VMEM((1, 128), F32)],
        compiler_params=_cparams(("arbitrary",)), name="route",
    )(logits)


def _row_copy(src, s, dst, d, sem):
    return pltpu.make_async_copy(src.at[pl.ds(s, 1)], dst.at[pl.ds(d, 1)], sem)


def _invert_kernel(poff_ref, ri_ref, dest_ref, rt_ref):
    i = pl.program_id(0)

    @pl.when(i == 0)
    def _():
        def clear(p, c):
            rt_ref[p] = 0
            return c
        lax.fori_loop(0, N_MOE_ROWS, clear, 0, unroll=32)

    def put(r, c):
        t = i * TM + r
        for s in range(2):
            d = poff_ref[ri_ref[0, 0, 4 * r + s]] + ri_ref[0, 0, 4 * r + 2 + s]
            dest_ref[0, 0, 2 * r + s] = d
            rt_ref[d] = t
        return c

    def blank(r, c):
        dest_ref[0, 0, 2 * r] = 0
        dest_ref[0, 0, 2 * r + 1] = 0
        return c

    @pl.when(i < T_PROMPT // TM)
    def _():
        lax.fori_loop(0, TM, put, 0, unroll=8)

    @pl.when(i == T_PROMPT // TM)
    def _():
        lax.fori_loop(0, DEC_BATCH, put, 0, unroll=8)
        lax.fori_loop(DEC_BATCH, TM, blank, 0, unroll=8)


def _invert(poff, ri4):
    grid_spec = pltpu.PrefetchScalarGridSpec(
        num_scalar_prefetch=1, grid=(N_TOK_BLOCKS,),
        in_specs=[pl.BlockSpec((1, 1, 4 * TM), lambda i, po: (i, 0, 0), memory_space=pltpu.SMEM)],
        out_specs=[pl.BlockSpec((1, 1, 2 * TM), lambda i, po: (i, 0, 0), memory_space=pltpu.SMEM),
                   pl.BlockSpec(memory_space=pltpu.SMEM)])
    return pl.pallas_call(
        _invert_kernel, grid_spec=grid_spec,
        out_shape=(jax.ShapeDtypeStruct((N_TOK_BLOCKS, 1, 2 * TM), I32),
                   jax.ShapeDtypeStruct((N_MOE_ROWS,), I32)),
        compiler_params=_cparams(("arbitrary",)), name="moe_invert",
    )(poff, ri4)


def _expert_kernel(fb_ref, nv_ref, rt_ref, xm_ref, gf_ref, wg_ref, wu_ref, wd_ref, y_ref,
                   xbuf, ybuf, gsem, ysem, wgu_scr, wd_scr):
    e = pl.program_id(0)
    first, end, total = fb_ref[e], fb_ref[e + 1], fb_ref[N_EXPERTS]

    def y_copy(g, slot):
        return pltpu.make_async_copy(ybuf.at[slot], y_ref.at[pl.ds(g * MOE_BLK, MOE_BLK)], ysem.at[slot])

    def gather(g, slot, wait):
        def body(r, c):
            if wait:
                _row_copy(xm_ref, 0, xbuf.at[slot], 0, gsem.at[slot]).wait()
            else:
                _row_copy(xm_ref, rt_ref[g * MOE_BLK + r], xbuf.at[slot], r,
                          gsem.at[slot]).start(priority=GATHER_PRIORITY)
            return c
        lax.fori_loop(0, nv_ref[g], body, 0)

    @pl.when(e == 0)
    def _():
        xbuf[...] = jnp.zeros(xbuf.shape, F32)
        for d in range(GATHER_DEPTH - 1):
            gather(d, d, wait=False)

    @pl.when(end > first)
    def _():
        wgu_scr[:, 0:EXPERT_FF] = wg_ref[0].astype(BF16)
        wgu_scr[:, EXPERT_FF:] = wu_ref[0].astype(BF16)
        wd_scr[...] = wd_ref[0].astype(BF16)

    def block(g, carry):
        slot = g % 2
        xslot = g % GATHER_DEPTH

        gather(g + GATHER_DEPTH - 1, (g + GATHER_DEPTH - 1) % GATHER_DEPTH, wait=False)
        gather(g, xslot, wait=True)
        h = _rms(xbuf[xslot], gf_ref[...]).astype(BF16)
        gu = jnp.dot(h, wgu_scr[...], preferred_element_type=F32)
        gate, up = gu[:, :EXPERT_FF], gu[:, EXPERT_FF:]
        act = (gate * _sigmoid(gate) * up).astype(BF16)
        y = jnp.dot(act, wd_scr[...], preferred_element_type=F32)

        @pl.when(g >= 2)
        def _():
            y_copy(g - 2, slot).wait()
        ybuf[slot] = y
        y_copy(g, slot).start()
        return carry

    lax.fori_loop(first, end, block, 0)

    @pl.when(e == N_EXPERTS - 1)
    def _():
        @pl.when(total >= 2)
        def _():
            y_copy(total - 2, total % 2).wait()

        @pl.when(total >= 1)
        def _():
            y_copy(total - 1, (total - 1) % 2).wait()

        ybuf[0] = jnp.zeros((MOE_BLK, D_MODEL), F32)

        def zero_start(g, c):
            y_copy(g, 0).start()
            return c

        def zero_wait(g, c):
            y_copy(g, 0).wait()
            return c

        lax.fori_loop(total, N_MOE_BLOCKS, zero_start, 0)
        lax.fori_loop(total, N_MOE_BLOCKS, zero_wait, 0)


def _experts(first_blk, nvalid, row_tok, xm_all, g_ffn, w_g, w_u, w_d):
    grid_spec = pltpu.PrefetchScalarGridSpec(
        num_scalar_prefetch=3, grid=(N_EXPERTS,),
        in_specs=[pl.BlockSpec(memory_space=pl.ANY),
                  pl.BlockSpec((1, D_MODEL), lambda e, fb, nv, rt: (0, 0)),
                  pl.BlockSpec((1, D_MODEL, EXPERT_FF), lambda e, fb, nv, rt: (e, 0, 0)),
                  pl.BlockSpec((1, D_MODEL, EXPERT_FF), lambda e, fb, nv, rt: (e, 0, 0)),
                  pl.BlockSpec((1, EXPERT_FF, D_MODEL), lambda e, fb, nv, rt: (e, 0, 0))],
        out_specs=pl.BlockSpec(memory_space=pl.ANY),
        scratch_shapes=[pltpu.VMEM((GATHER_DEPTH, MOE_BLK, D_MODEL), F32), pltpu.VMEM((2, MOE_BLK, D_MODEL), F32),
                        pltpu.SemaphoreType.DMA((GATHER_DEPTH,)), pltpu.SemaphoreType.DMA((2,)),
                        pltpu.VMEM((D_MODEL, 2 * EXPERT_FF), BF16), pltpu.VMEM((EXPERT_FF, D_MODEL), BF16)])
    return pl.pallas_call(
        _expert_kernel, grid_spec=grid_spec,
        out_shape=jax.ShapeDtypeStruct((N_MOE_ROWS, D_MODEL), F32),
        compiler_params=_cparams(("arbitrary",)), name="moe_experts",
    )(first_blk, nvalid, row_tok, xm_all, g_ffn, w_g, w_u, w_d)


def _combine_kernel(dest_ref, destn_ref, rw_ref, xm_ref, g_ref, y_ref, o_ref, ybuf, sems, *, rows, nblk):
    i = pl.program_id(0)
    slot = i % 2

    def gather(d_ref, s, wait):
        def body(r, c):
            for j in range(2):
                cp = _row_copy(y_ref, 0 if wait else d_ref[0, 0, 2 * r + j], ybuf.at[s, j], r, sems.at[s])
                cp.wait() if wait else cp.start(priority=j)
            return c
        lax.fori_loop(0, rows, body, 0, unroll=8)

    @pl.when(i == 0)
    def _():
        gather(dest_ref, 0, wait=False)

    @pl.when(i + 1 < nblk)
    def _():
        gather(destn_ref, 1 - slot, wait=False)

    gather(dest_ref, slot, wait=True)
    w = rw_ref[...]
    x = xm_ref[...] + w[:, 0:1] * ybuf[slot, 0] + w[:, 1:2] * ybuf[slot, 1]
    o_ref[...] = _rms(x, g_ref[...]).reshape(o_ref.shape)


def _combine(dest3, rw, xm, g, y_rows, rows, nblk, blk0, out_spec, out_shape):
    smem_blk = lambda f: pl.BlockSpec((1, 1, 2 * rows), f, memory_space=pltpu.SMEM)
    return pl.pallas_call(
        functools.partial(_combine_kernel, rows=rows, nblk=nblk), grid=(nblk,),
        in_specs=[smem_blk(lambda i: (i, 0, 0)),
                  smem_blk(lambda i: (jnp.minimum(i + 1, nblk - 1), 0, 0)),
                  pl.BlockSpec((rows, 128), lambda i: (blk0 + i, 0)),
                  pl.BlockSpec((rows, D_MODEL), lambda i: (blk0 + i, 0)),
                  pl.BlockSpec((1, D_MODEL), lambda i: (0, 0)),
                  pl.BlockSpec(memory_space=pl.ANY)],
        out_specs=out_spec, out_shape=out_shape,
        scratch_shapes=[pltpu.VMEM((2, 2, rows, D_MODEL), F32), pltpu.SemaphoreType.DMA((2,))],
        compiler_params=_cparams(("arbitrary",)), name="moe_combine",
    )(dest3, dest3, rw, xm, g, y_rows)


def kernel(x_prompt, x_sample, cache_k, cache_v, state_ssm_re, state_ssm_im, g_mix_norm, w_in, ssm_a_re, ssm_a_im, ssm_log_dt, ssm_b_re, ssm_b_im, ssm_c_re, ssm_c_im, ssm_d, w_glu, g_ssm_out, g_attn_out, w_out, g_ffn_norm, w_router_group, b_router_group, w_router_expert, b_router_expert, w_expert_gate, w_expert_up, w_expert_down, g_final):
    row = lambda a: a.reshape(1, -1)
    half = HEAD_DIM // 2
    inv = ROPE_THETA ** (-jnp.arange(half, dtype=F32) / half)
    inv = jnp.concatenate([inv, inv]).reshape(1, HEAD_DIM)

    lam_r, lam_i, lams_r, lams_i, bb_r, bb_i = _ssm_params(ssm_a_re[0], ssm_a_im[0], ssm_log_dt[0], ssm_b_re[0], ssm_b_im[0])
    bm = jnp.concatenate([_block_diag_in(bb_r), _block_diag_in(bb_i)], axis=-1)
    cm = jnp.concatenate([_block_diag_out(ssm_c_re[0]), -_block_diag_out(ssm_c_im[0])], axis=1)
    dskip = row(ssm_d[0])
    bcast8 = lambda a: jnp.broadcast_to(a.reshape(1, -1), (N_SEG, N_GROUPS * SSM_STATE))
    lr8, li8, lsr8, lsi8 = bcast8(lam_r), bcast8(lam_i), bcast8(lams_r), bcast8(lams_i)

    wr = jnp.concatenate([w_router_group[0], w_router_expert[0].reshape(D_MODEL, N_EXPERTS)], axis=1)
    wr = jnp.pad(wr, ((0, 0), (0, 128 - wr.shape[1])))
    br = jnp.pad(jnp.concatenate([b_router_group[0], b_router_expert[0].reshape(-1)]), (0, 128 - 36)).reshape(1, 128)

    x4 = x_prompt.reshape(BATCH, N_SEG, SEG_LEN, D_MODEL)
    u_il, q4, k4, v4, k_new, v_new = _inproj(x4, row(g_mix_norm[0]), inv, w_in[0].astype(BF16))
    bm16, cm16 = bm.astype(BF16), cm.astype(BF16)
    ends = _ssm_scan(u_il, bm16, lr8, li8, final=False)
    z_il, xend = _ssm_scan(u_il, bm16, lr8, li8, final=True, e=ends, lsr8=lsr8, lsi8=lsi8, cm=cm16, dskip=dskip)
    flat = lambda a: a.reshape(BATCH, SEQ, ATTN_WIDTH)
    o_attn = _attention(flat(q4), flat(k4), flat(v4))
    xm_all, lg_all = _outproj(
        x4, z_il, o_attn.reshape(BATCH, N_SEG, SEG_LEN, ATTN_WIDTH), w_glu[0].astype(BF16), row(g_ssm_out[0]),
        row(g_attn_out[0]), w_out[0].astype(BF16), row(g_ffn_norm[0]), wr.astype(BF16), br)

    xs = x_sample.reshape(DEC_BATCH, D_MODEL)
    proj = _s_inproj(xs, row(g_mix_norm[0]), w_in[0])
    us = proj[:, :SSM_WIDTH]
    qkv = proj[:, SSM_WIDTH:].reshape(DEC_BATCH, 3, N_HEADS, HEAD_DIM)
    zs, ns_r, ns_i = _s_ssm(us, state_ssm_re[0].reshape(DEC_BATCH, -1), state_ssm_im[0].reshape(DEC_BATCH, -1),
                            row(lam_r), row(lam_i), bm, cm, dskip)
    mix_ssm_s = _s_glu(zs, w_glu[0], row(g_ssm_out[0]))
    o_s, k_rot = _s_attention(qkv[:, 0], qkv[:, 1], qkv[:, 2], inv, cache_k[0], cache_v[0])
    xm_all, lg_all = _s_outproj(
        xm_all, lg_all, xs, mix_ssm_s, o_s.reshape(DEC_BATCH, ATTN_WIDTH), row(g_attn_out[0]),
        w_out[0], row(g_ffn_norm[0]), wr, br)

    ri, rw, cnt = _route(lg_all)
    counts = cnt[0, :N_EXPERTS].astype(I32)
    padded = (counts + MOE_BLK - 1) // MOE_BLK * MOE_BLK
    pend = jnp.cumsum(padded)
    poff = pend - padded
    first_blk = jnp.concatenate([poff, pend[-1:]]) // MOE_BLK
    blk_start = jnp.arange(N_MOE_BLOCKS + GATHER_DEPTH, dtype=I32) * MOE_BLK
    block_e = jnp.minimum(jnp.sum(pend[None, :] <= blk_start[:, None], axis=1), N_EXPERTS - 1)
    nvalid = jnp.clip(counts[block_e] - (blk_start - poff[block_e]), 0, MOE_BLK).astype(I32)
    dest3, row_tok = _invert(poff, ri[:, 0:4].reshape(N_TOK_BLOCKS, 1, 4 * TM))
    y_rows = _experts(first_blk, nvalid, row_tok, xm_all, row(g_ffn_norm[0]),
                      w_expert_gate[0], w_expert_up[0], w_expert_down[0])
    gfin = row(g_final)
    nkb = SEG_LEN // TK
    y_prompt = _combine(dest3, rw, xm_all, gfin, y_rows, TM, T_PROMPT // TM, 0,
                        pl.BlockSpec((1, N_SEG, TK, D_MODEL), lambda i: (i // nkb, 0, i % nkb, 0)),
                        jax.ShapeDtypeStruct((BATCH, N_SEG, SEG_LEN, D_MODEL), F32))
    dest_s = dest3[N_TOK_BLOCKS - 1:, :, :2 * DEC_BATCH]
    y_sample = _combine(dest_s, rw, xm_all, gfin, y_rows, DEC_BATCH, 1, T_PROMPT // DEC_BATCH,
                        pl.BlockSpec((DEC_BATCH, D_MODEL), lambda i: (0, 0)),
                        jax.ShapeDtypeStruct((DEC_BATCH, D_MODEL), F32))

    kv_shape = (1, BATCH, CACHE_LEN, N_HEADS, HEAD_DIM)
    st_p = lambda a: a[:, :, N_SEG - 1, :].reshape(1, BATCH, N_GROUPS, SSM_STATE)
    st_s = lambda a: a.reshape(1, DEC_BATCH, N_GROUPS, SSM_STATE)
    kvs = lambda a: a.reshape(1, DEC_BATCH, 1, N_HEADS, HEAD_DIM)
    return (y_prompt.reshape(BATCH, SEQ, D_MODEL), y_sample.reshape(DEC_BATCH, 1, D_MODEL),
            k_new.reshape(kv_shape), v_new.reshape(kv_shape),
            st_p(xend[..., :CH_ST]), st_p(xend[..., CH_ST:]),
            kvs(k_rot), kvs(qkv[:, 2]), st_s(ns_r), st_s(ns_i))
```

```python
import functools
import math

import jax
import jax.numpy as jnp
from jax import lax
from jax.experimental import pallas as pl
from jax.experimental.pallas import tpu as pltpu

F32 = jnp.float32
BF16 = jnp.bfloat16
I32 = jnp.int32
U32 = jnp.uint32
HIGHEST = lax.Precision.HIGHEST

D_MODEL = 2048
BATCH = 2
SEQ = 4096
DEC_BATCH = 32
PAST_LEN = 8192
CACHE_LEN = 2048
SSM_WIDTH = 1024
ATTN_WIDTH = 1024
SSM_GROUP = 16
N_GROUPS = 64
SSM_STATE = 64
HEAD_DIM = 128
N_HEADS = 8
D_IN = 4096
BRANCHES = ((128, 1), (512, 4), (2048, 16))
BAND = 128
ROPE_THETA = 10000.0
N_EXPERT_GROUPS = 4
EXPERTS_PER_GROUP = 8
N_EXPERTS = 32
EXPERT_FF = 512
NORM_EPS = 1e-6

N_SEG = 8
SEG_LEN = SEQ // N_SEG
N_CHUNK = 4
CH_IN = SSM_WIDTH // N_CHUNK
CH_ST = N_GROUPS * SSM_STATE // N_CHUNK
SSM_TK = 128
TM = 256
TK = TM // N_SEG
TK_SHIFT = TK.bit_length() - 1
LANE = 128
T_PROMPT = BATCH * SEQ
T_REAL = T_PROMPT + DEC_BATCH
T_PAD = T_PROMPT + TM
N_TOK_BLOCKS = T_PAD // TM
MOE_BLK = 256
N_MOE_BLOCKS = -(-(2 * T_REAL) // MOE_BLK) + N_EXPERTS
N_MOE_ROWS = N_MOE_BLOCKS * MOE_BLK
VMEM_LIMIT = 56 * 1024 * 1024
NEG = -1e30
LOG2_E = 1.4426950408889634
GATHER_PRIORITY = 1
GATHER_DEPTH = 2


def _cparams(sem, vmem=VMEM_LIMIT):
    return pltpu.CompilerParams(dimension_semantics=sem, vmem_limit_bytes=vmem)


def _rms(x, g):
    return x * lax.rsqrt(jnp.mean(x * x, axis=-1, keepdims=True) + NORM_EPS) * g


def _gelu_tanh(y):
    return 0.5 * y * (1.0 + jnp.tanh(0.7978845608028654 * (y + 0.044715 * (y * y * y))))


def _sigmoid(x):
    return 1.0 / (1.0 + jnp.exp(-x))


def _rope_tables(pos, inv):
    ang = pos * inv
    lane = lax.broadcasted_iota(I32, ang.shape, 1)
    return jnp.cos(ang), jnp.where(lane < HEAD_DIM // 2, -jnp.sin(ang), jnp.sin(ang))


def _rope_heads(x, cos, sin):
    outs = []
    for h in range(x.shape[1] // HEAD_DIM):
        xh = x[:, h * HEAD_DIM:(h + 1) * HEAD_DIM]
        outs.append(xh * cos + pltpu.roll(xh, HEAD_DIM // 2, 1) * sin)
    return jnp.concatenate(outs, axis=1)


def _ssm_param_kernel(are_ref, aim_ref, ldt_ref, btr_ref, bti_ref,
                      lr_ref, li_ref, lsr_ref, lsi_ref, bbr_ref, bbi_ref):
    ar, ai = are_ref[...], aim_ref[...]
    dt = jnp.exp(ldt_ref[...])
    er, ei = ar * dt, ai * dt
    mag = jnp.exp(er)
    lr, li = mag * jnp.cos(ei), mag * jnp.sin(ei)
    lr_ref[...] = lr
    li_ref[...] = li
    mag_s = jnp.exp(er * SEG_LEN)
    lsr_ref[...] = mag_s * jnp.cos(ei * SEG_LEN)
    lsi_ref[...] = mag_s * jnp.sin(ei * SEG_LEN)
    xr, xi = lr - 1.0, li
    den = ar * ar + ai * ai
    cr = (xr * ar + xi * ai) / den
    ci = (xi * ar - xr * ai) / den
    btr, bti = btr_ref[...], bti_ref[...]
    bbr_ref[...] = cr[:, None, :] * btr - ci[:, None, :] * bti
    bbi_ref[...] = cr[:, None, :] * bti + ci[:, None, :] * btr


def _ssm_params(a_re, a_im, log_dt, b_re, b_im):
    g, n, p = N_GROUPS, SSM_STATE, SSM_GROUP
    gn = jax.ShapeDtypeStruct((g, n), F32)
    gpn = jax.ShapeDtypeStruct((g, p, n), F32)
    return pl.pallas_call(
        _ssm_param_kernel, out_shape=(gn, gn, gn, gn, gpn, gpn), name="ssm_params",
    )(a_re, a_im, log_dt.reshape(g, 1), b_re.transpose(0, 2, 1), b_im.transpose(0, 2, 1))


def _block_diag_in(m):
    rows = m.reshape(N_CHUNK, CH_IN, SSM_STATE)
    r = lax.broadcasted_iota(I32, (CH_IN, CH_ST), 0) // SSM_GROUP
    c = lax.broadcasted_iota(I32, (CH_IN, CH_ST), 1) // SSM_STATE
    return jnp.where((r == c)[None], jnp.tile(rows, (1, 1, 16)), 0.0)


def _block_diag_out(c):
    rows = c.transpose(0, 2, 1).reshape(N_CHUNK, CH_ST, SSM_GROUP)
    r = lax.broadcasted_iota(I32, (CH_ST, CH_IN), 0) // SSM_STATE
    k = lax.broadcasted_iota(I32, (CH_ST, CH_IN), 1) // SSM_GROUP
    return jnp.where((r == k)[None], jnp.tile(rows, (1, 1, 16)), 0.0)


def _inproj_kernel(x_ref, g_ref, inv_ref, w_ref, u_ref, q_ref, k_ref, v_ref, kn_ref, vn_ref):
    kb = pl.program_id(1)
    h = _rms(x_ref[0].reshape(TM, D_MODEL), g_ref[...]).astype(BF16)
    rid = lax.broadcasted_iota(I32, (TM, 1), 0)
    pos = ((rid >> TK_SHIFT) * SEG_LEN + kb * TK + (rid & (TK - 1))).astype(F32)
    cos, sin = _rope_tables(pos, inv_ref[...])
    w = SSM_WIDTH
    u = jnp.dot(h, w_ref[:, 0:w], preferred_element_type=F32)
    for s in range(N_SEG):
        for c in range(w // LANE):
            u_ref[0, c, pl.ds(s, TK, stride=N_SEG), :] = u[s * TK:(s + 1) * TK, c * LANE:(c + 1) * LANE]
    q_ref[0] = _rope_heads(jnp.dot(h, w_ref[:, w:2 * w], preferred_element_type=F32), cos, sin).reshape(N_SEG, TK, w)
    k = _rope_heads(jnp.dot(h, w_ref[:, 2 * w:3 * w], preferred_element_type=F32), cos, sin)
    v = jnp.dot(h, w_ref[:, 3 * w:4 * w], preferred_element_type=F32)
    k_ref[0] = k.reshape(N_SEG, TK, w)
    v_ref[0] = v.reshape(N_SEG, TK, w)
    kn_ref[0] = k[TM // 2:].reshape(N_SEG // 2, TK, w)
    vn_ref[0] = v[TM // 2:].reshape(N_SEG // 2, TK, w)


def _inproj(x4, g, inv, w_bf16):
    const = lambda b, k: (0, 0)
    blk4 = lambda nseg, width: pl.BlockSpec((1, nseg, TK, width), lambda b, k: (b, 0, k, 0))
    qkv = jax.ShapeDtypeStruct((BATCH, N_SEG, SEG_LEN, ATTN_WIDTH), F32)
    kvn = jax.ShapeDtypeStruct((BATCH, N_SEG // 2, SEG_LEN, ATTN_WIDTH), F32)
    return pl.pallas_call(
        _inproj_kernel, grid=(BATCH, SEG_LEN // TK),
        in_specs=[blk4(N_SEG, D_MODEL), pl.BlockSpec((1, D_MODEL), const),
                  pl.BlockSpec((1, HEAD_DIM), const), pl.BlockSpec((D_MODEL, D_IN), const)],
        out_specs=[pl.BlockSpec((1, SSM_WIDTH // LANE, TM, LANE), lambda b, k: (b, 0, k, 0)),
                   blk4(N_SEG, ATTN_WIDTH), blk4(N_SEG, ATTN_WIDTH), blk4(N_SEG, ATTN_WIDTH),
                   blk4(N_SEG // 2, ATTN_WIDTH), blk4(N_SEG // 2, ATTN_WIDTH)],
        out_shape=(jax.ShapeDtypeStruct((BATCH, SSM_WIDTH // LANE, SEQ, LANE), F32), qkv, qkv, qkv, kvn, kvn),
        compiler_params=_cparams(("arbitrary",) * 2), name="inproj",
    )(x4, g, inv, w_bf16)


def _ssm_scan_kernel(*refs, final):
    if final:
        (u_ref, bm_ref, lr_ref, li_ref, e_ref, lsr_ref, lsi_ref, cm_ref, d_ref,
         z_ref, xend_ref, bu_scr, xr_scr, xi_scr) = refs
    else:
        u_ref, bm_ref, lr_ref, li_ref, xend_ref, bu_scr, xr_scr, xi_scr = refs
    kb = pl.program_id(2)

    @pl.when(kb == 0)
    def _():
        if final:
            lsr, lsi = lsr_ref[0:1, :], lsi_ref[0:1, :]
            seg_id = lax.broadcasted_iota(I32, (N_SEG, CH_ST), 0)
            pr = pi = jnp.zeros((1, CH_ST), F32)
            xr0 = xi0 = jnp.zeros((N_SEG, CH_ST), F32)
            for s in range(1, N_SEG):
                er = e_ref[0, 0, s - 1:s, 0:CH_ST]
                ei = e_ref[0, 0, s - 1:s, CH_ST:2 * CH_ST]
                pr, pi = er + lsr * pr - lsi * pi, ei + lsr * pi + lsi * pr
                xr0 = jnp.where(seg_id == s, pr, xr0)
                xi0 = jnp.where(seg_id == s, pi, xi0)
            xr_scr[...] = xr0
            xi_scr[...] = xi0
        else:
            xr_scr[...] = jnp.zeros((N_SEG, CH_ST), F32)
            xi_scr[...] = jnp.zeros((N_SEG, CH_ST), F32)

    u = jnp.concatenate([u_ref[0, c] for c in range(CH_IN // LANE)], axis=1)
    bu_scr[...] = jnp.dot(u.astype(BF16), bm_ref[0], preferred_element_type=F32)
    lr, li = lr_ref[...], li_ref[...]

    def step(k, carry):
        xr, xi = carry
        r0 = pl.multiple_of(k * N_SEG, N_SEG)
        bur = bu_scr[pl.ds(r0, N_SEG), 0:CH_ST]
        bui = bu_scr[pl.ds(r0, N_SEG), CH_ST:2 * CH_ST]
        nr = lr * xr - li * xi + bur
        ni = lr * xi + li * xr + bui
        if final:
            bu_scr[pl.ds(r0, N_SEG), 0:CH_ST] = nr
            bu_scr[pl.ds(r0, N_SEG), CH_ST:2 * CH_ST] = ni
        return nr, ni

    xr, xi = lax.fori_loop(0, SSM_TK, step, (xr_scr[...], xi_scr[...]), unroll=4)
    xr_scr[...] = xr
    xi_scr[...] = xi
    if final:
        y = jnp.dot(bu_scr[...].astype(BF16), cm_ref[0], preferred_element_type=F32)
        z = _gelu_tanh(y + d_ref[...] * u)
        for c in range(CH_IN // LANE):
            z_ref[0, c] = z[:, c * LANE:(c + 1) * LANE]

    @pl.when(kb == pl.num_programs(2) - 1)
    def _():
        xend_ref[0, 0] = jnp.concatenate([xr, xi], axis=1)


def _ssm_scan(u_perm, bm, lr8, li8, final, e=None, lsr8=None, lsi8=None, cm=None, dskip=None):
    rows = SSM_TK * N_SEG
    nkb = SEG_LEN // SSM_TK
    in_specs = [pl.BlockSpec((1, CH_IN // LANE, rows, LANE), lambda b, j, k: (b, j, k, 0)),
                pl.BlockSpec((1, CH_IN, 2 * CH_ST), lambda b, j, k: (j, 0, 0)),
                pl.BlockSpec((N_SEG, CH_ST), lambda b, j, k: (0, j)),
                pl.BlockSpec((N_SEG, CH_ST), lambda b, j, k: (0, j))]
    args = [u_perm, bm, lr8, li8]
    xend_spec = pl.BlockSpec((1, 1, N_SEG, 2 * CH_ST), lambda b, j, k: (b, j, 0, 0))
    xend_shape = jax.ShapeDtypeStruct((BATCH, N_CHUNK, N_SEG, 2 * CH_ST), F32)
    if final:
        in_specs += [xend_spec,
                     pl.BlockSpec((N_SEG, CH_ST), lambda b, j, k: (0, j)),
                     pl.BlockSpec((N_SEG, CH_ST), lambda b, j, k: (0, j)),
                     pl.BlockSpec((1, 2 * CH_ST, CH_IN), lambda b, j, k: (j, 0, 0)),
                     pl.BlockSpec((1, CH_IN), lambda b, j, k: (0, j))]
        args += [e, lsr8, lsi8, cm, dskip]
        out_specs = [pl.BlockSpec((1, CH_IN // LANE, rows, LANE), lambda b, j, k: (b, j, k, 0)), xend_spec]
        out_shape = (jax.ShapeDtypeStruct((BATCH, SSM_WIDTH // LANE, SEQ, LANE), F32), xend_shape)
    else:
        out_specs = xend_spec
        out_shape = xend_shape
    return pl.pallas_call(
        functools.partial(_ssm_scan_kernel, final=final), grid=(BATCH, N_CHUNK, nkb),
        in_specs=in_specs, out_specs=out_specs, out_shape=out_shape,
        scratch_shapes=[pltpu.VMEM((rows, 2 * CH_ST), F32),
                        pltpu.VMEM((N_SEG, CH_ST), F32), pltpu.VMEM((N_SEG, CH_ST), F32)],
        compiler_params=_cparams(("arbitrary",) * 3),
        name="ssm_scan_final" if final else "ssm_scan_ends",
    )(*args)


def _attn_kernel(q_ref, k_ref, v_ref, o_ref, o_scr, lse_scr, bias_scr):
    scale2 = HEAD_DIM ** -0.5 * LOG2_E
    qi = lax.broadcasted_iota(I32, (BAND, 2 * BAND), 0)
    kj = lax.broadcasted_iota(I32, (BAND, 2 * BAND), 1)
    cur_ok = (kj >= BAND) & (kj - BAND <= qi)
    prev_ok = (kj < BAND) & (kj >= qi)
    bias_scr[0] = jnp.where(cur_ok, 0.0, NEG)
    bias_scr[1] = jnp.where(cur_ok | prev_ok, 0.0, NEG)
    nt = (((1,), (1,)), ((), ()))
    for bi, (_, d) in enumerate(BRANCHES):
        nblk = SEQ // (BAND * d)
        shift = nblk.bit_length() - 1

        def tile(i, c, bi=bi, d=d, nblk=nblk, shift=shift):
            r = i >> shift
            ib = i & (nblk - 1)
            start = r + ib * (BAND * d)
            pstart = jnp.maximum(start - BAND * d, r)
            cur = pl.ds(start, BAND, stride=d)
            prev = pl.ds(pstart, BAND, stride=d)
            q = (q_ref[0, cur, :] * scale2).astype(BF16)
            k2 = jnp.concatenate([k_ref[0, prev, :], k_ref[0, cur, :]], axis=0).astype(BF16)
            v2 = jnp.concatenate([v_ref[0, prev, :], v_ref[0, cur, :]], axis=0).astype(BF16)
            s = lax.dot_general(q, k2, nt, preferred_element_type=F32) + bias_scr[jnp.minimum(ib, 1)]
            m = jnp.max(s, axis=-1, keepdims=True)
            p = jnp.exp2(s - m)
            l = jnp.sum(p, axis=-1, keepdims=True)
            o = jnp.dot(p.astype(BF16), v2, preferred_element_type=F32) * (1.0 / l)
            o_scr[bi, cur, :] = o
            lse_scr[bi, cur, :] = jnp.broadcast_to(m + jnp.log2(l), (BAND, HEAD_DIM))
            return c

        def softmax_pv(s, v, rows_out, bi=bi):
            m = jnp.max(s, axis=-1, keepdims=True)
            p = jnp.exp2(s - m)
            l = jnp.sum(p, axis=-1, keepdims=True)
            o_scr[bi, rows_out, :] = jnp.dot(p.astype(BF16), v, preferred_element_type=F32) * (1.0 / l)
            lse_scr[bi, rows_out, :] = jnp.broadcast_to(m + jnp.log2(l), (BAND, HEAD_DIM))

        def residue(r, c, d=d):
            sub = pl.ds(r, 2 * BAND, stride=d)
            q = (q_ref[0, sub, :] * scale2).astype(BF16)
            k = k_ref[0, sub, :].astype(BF16)
            v = v_ref[0, sub, :].astype(BF16)
            s0 = lax.dot_general(q[:BAND], k[:BAND], nt, preferred_element_type=F32) + bias_scr[1, :, BAND:]
            softmax_pv(s0, v[:BAND], pl.ds(r, BAND, stride=d))
            s1 = lax.dot_general(q[BAND:], k, nt, preferred_element_type=F32) + bias_scr[1]
            softmax_pv(s1, v, pl.ds(r + BAND * d, BAND, stride=d))
            return c

        if nblk == 2:
            lax.fori_loop(0, d, residue, 0, unroll=8)
        else:
            lax.fori_loop(0, SEQ // BAND, tile, 0, unroll=16)

    rows = 512

    def merge(i, c):
        sl = pl.ds(pl.multiple_of(i * rows, rows), rows)
        l0, l1, l2 = lse_scr[0, sl, :], lse_scr[1, sl, :], lse_scr[2, sl, :]
        m = jnp.maximum(jnp.maximum(l0, l1), l2)
        e0, e1, e2 = jnp.exp2(l0 - m), jnp.exp2(l1 - m), jnp.exp2(l2 - m)
        o_ref[0, sl, :] = ((e0 * o_scr[0, sl, :] + e1 * o_scr[1, sl, :] + e2 * o_scr[2, sl, :])
                           * (1.0 / (e0 + e1 + e2)))
        return c

    lax.fori_loop(0, SEQ // rows, merge, 0)


def _attention(q, k, v):
    spec = pl.BlockSpec((1, SEQ, HEAD_DIM), lambda b, h: (b, 0, h))
    return pl.pallas_call(
        _attn_kernel, grid=(BATCH, N_HEADS), in_specs=[spec, spec, spec], out_specs=spec,
        out_shape=jax.ShapeDtypeStruct((BATCH, SEQ, ATTN_WIDTH), F32),
        scratch_shapes=[pltpu.VMEM((3, SEQ, HEAD_DIM), F32), pltpu.VMEM((3, SEQ, HEAD_DIM), F32),
                        pltpu.VMEM((2, BAND, 2 * BAND), F32)],
        compiler_params=_cparams(("arbitrary",) * 2), name="dilated_attn",
    )(q, k, v)


def _outproj_kernel(x_ref, z_ref, o_ref, wglu_ref, gs_ref, ga_ref, w_ref, gf_ref, wr_ref, br_ref, xm_ref, lg_ref):
    is_prompt = pl.program_id(0) < T_PROMPT // TM

    @pl.when(is_prompt)
    def _():
        z = jnp.concatenate(
            [jnp.concatenate([z_ref[0, c, pl.ds(s, TK, stride=N_SEG), :] for c in range(SSM_WIDTH // LANE)], axis=1)
             for s in range(N_SEG)], axis=0)
        zz = z * _sigmoid(jnp.dot(z.astype(BF16), wglu_ref[...], preferred_element_type=F32))
        ms = _rms(zz, gs_ref[...]).astype(BF16)
        ma = _rms(o_ref[0].reshape(TM, ATTN_WIDTH), ga_ref[...]).astype(BF16)
        xm = (x_ref[0].reshape(TM, D_MODEL) + jnp.dot(ms, w_ref[0:SSM_WIDTH, :], preferred_element_type=F32)
              + jnp.dot(ma, w_ref[SSM_WIDTH:, :], preferred_element_type=F32))
        xm_ref[...] = xm
        h = _rms(xm, gf_ref[...]).astype(BF16)
        lg_ref[...] = jnp.dot(h, wr_ref[...], preferred_element_type=F32) + br_ref[...]

    @pl.when(jnp.logical_not(is_prompt))
    def _():
        xm_ref[...] = jnp.zeros((TM, D_MODEL), F32)
        lg_ref[...] = jnp.zeros((TM, 128), F32)


def _outproj(x4, z_slab, o_attn4, wglu_bf16, g_ssm, g_attn, w_bf16, g_ffn, wr_bf16, br):
    nkb = SEG_LEN // TK

    def split(i):
        i = jnp.minimum(i, T_PROMPT // TM - 1)
        return i // nkb, i % nkb

    def blk4(i):
        b, k = split(i)
        return (b, 0, k, 0)

    row = lambda i: (i, 0)
    const = lambda i: (0, 0)
    return pl.pallas_call(
        _outproj_kernel, grid=(N_TOK_BLOCKS,),
        in_specs=[pl.BlockSpec((1, N_SEG, TK, D_MODEL), blk4),
                  pl.BlockSpec((1, SSM_WIDTH // LANE, TM, LANE), blk4),
                  pl.BlockSpec((1, N_SEG, TK, ATTN_WIDTH), blk4),
                  pl.BlockSpec((SSM_WIDTH, SSM_WIDTH), const),
                  pl.BlockSpec((1, SSM_WIDTH), const),
                  pl.BlockSpec((1, ATTN_WIDTH), const),
                  pl.BlockSpec((D_MODEL, D_MODEL), const),
                  pl.BlockSpec((1, D_MODEL), const),
                  pl.BlockSpec((D_MODEL, 128), const),
                  pl.BlockSpec((1, 128), const)],
        out_specs=[pl.BlockSpec((TM, D_MODEL), row), pl.BlockSpec((TM, 128), row)],
        out_shape=(jax.ShapeDtypeStruct((T_PAD, D_MODEL), F32), jax.ShapeDtypeStruct((T_PAD, 128), F32)),
        compiler_params=_cparams(("arbitrary",)), name="outproj",
    )(x4, z_slab, o_attn4, wglu_bf16, g_ssm, g_attn, w_bf16, g_ffn, wr_bf16, br)


def _s_inproj_kernel(x_ref, g_ref, w_ref, o_ref):
    h = _rms(x_ref[...], g_ref[...])
    o_ref[...] = jnp.dot(h, w_ref[...], preferred_element_type=F32, precision=HIGHEST)


def _s_inproj(x, g, w):
    tn = 1024
    return pl.pallas_call(
        _s_inproj_kernel, grid=(D_IN // tn,),
        in_specs=[pl.BlockSpec((DEC_BATCH, D_MODEL), lambda j: (0, 0)),
                  pl.BlockSpec((1, D_MODEL), lambda j: (0, 0)),
                  pl.BlockSpec((D_MODEL, tn), lambda j: (0, j))],
        out_specs=pl.BlockSpec((DEC_BATCH, tn), lambda j: (0, j)),
        out_shape=jax.ShapeDtypeStruct((DEC_BATCH, D_IN), F32),
        compiler_params=_cparams(("arbitrary",)), name="s_inproj",
    )(x, g, w)


def _s_ssm_kernel(u_ref, sr_ref, si_ref, lr_ref, li_ref, bm_ref, cm_ref, d_ref, z_ref, nr_ref, ni_ref):
    u = u_ref[...]
    bu = jnp.dot(u, bm_ref[0], preferred_element_type=F32, precision=HIGHEST)
    lr, li = lr_ref[...], li_ref[...]
    xr, xi = sr_ref[...], si_ref[...]
    nr = lr * xr - li * xi + bu[:, :CH_ST]
    ni = lr * xi + li * xr + bu[:, CH_ST:]
    nr_ref[...] = nr
    ni_ref[...] = ni
    y = jnp.dot(jnp.concatenate([nr, ni], axis=1), cm_ref[0], preferred_element_type=F32, precision=HIGHEST)
    z_ref[...] = _gelu_tanh(y + d_ref[...] * u)


def _s_ssm(u, sr, si, lr1, li1, bm, cm, dskip):
    st = jax.ShapeDtypeStruct((DEC_BATCH, N_GROUPS * SSM_STATE), F32)
    return pl.pallas_call(
        _s_ssm_kernel, grid=(N_CHUNK,),
        in_specs=[pl.BlockSpec((DEC_BATCH, CH_IN), lambda j: (0, j)),
                  pl.BlockSpec((DEC_BATCH, CH_ST), lambda j: (0, j)),
                  pl.BlockSpec((DEC_BATCH, CH_ST), lambda j: (0, j)),
                  pl.BlockSpec((1, CH_ST), lambda j: (0, j)),
                  pl.BlockSpec((1, CH_ST), lambda j: (0, j)),
                  pl.BlockSpec((1, CH_IN, 2 * CH_ST), lambda j: (j, 0, 0)),
                  pl.BlockSpec((1, 2 * CH_ST, CH_IN), lambda j: (j, 0, 0)),
                  pl.BlockSpec((1, CH_IN), lambda j: (0, j))],
        out_specs=[pl.BlockSpec((DEC_BATCH, CH_IN), lambda j: (0, j)),
                   pl.BlockSpec((DEC_BATCH, CH_ST), lambda j: (0, j)),
                   pl.BlockSpec((DEC_BATCH, CH_ST), lambda j: (0, j))],
        out_shape=(jax.ShapeDtypeStruct((DEC_BATCH, SSM_WIDTH), F32), st, st),
        compiler_params=_cparams(("arbitrary",)), name="s_ssm",
    )(u, sr, si, lr1, li1, bm, cm, dskip)


def _s_glu_kernel(z_ref, w_ref, g_ref, o_ref):
    z = z_ref[...]
    zz = z * _sigmoid(jnp.dot(z, w_ref[...], preferred_element_type=F32, precision=HIGHEST))
    o_ref[...] = _rms(zz, g_ref[...])


def _s_glu(z, w, g):
    return pl.pallas_call(
        _s_glu_kernel, out_shape=jax.ShapeDtypeStruct((DEC_BATCH, SSM_WIDTH), F32),
        compiler_params=pltpu.CompilerParams(vmem_limit_bytes=VMEM_LIMIT), name="s_glu",
    )(z, w, g)


def _s_attn_kernel(q_ref, k_ref, v_ref, inv_ref, k1_ref, v1_ref, k4_ref, v4_ref, k16_ref, v16_ref,
                   o_ref, kr_ref):
    scale = HEAD_DIM ** -0.5
    pos = jnp.full((N_HEADS, 1), float(PAST_LEN), F32)
    cos, sin = _rope_tables(pos, inv_ref[...])
    q = q_ref[...] * cos + pltpu.roll(q_ref[...], HEAD_DIM // 2, 1) * sin
    kn = k_ref[...] * cos + pltpu.roll(k_ref[...], HEAD_DIM // 2, 1) * sin
    vn = v_ref[...]
    kr_ref[...] = kn
    s0 = jnp.sum(q * kn, axis=-1, keepdims=True) * scale
    outs, lses = [], []
    for kc_ref, vc_ref in ((k1_ref, v1_ref), (k4_ref, v4_ref), (k16_ref, v16_ref)):
        s = jnp.sum(kc_ref[...] * q[None], axis=-1, keepdims=True) * scale
        m = jnp.maximum(jnp.max(s, axis=0), s0)
        p = jnp.exp(s - m[None])
        p0 = jnp.exp(s0 - m)
        l = jnp.sum(p, axis=0) + p0
        outs.append((jnp.sum(p * vc_ref[...], axis=0) + p0 * vn) / l)
        lses.append(m + jnp.log(l))
    mm = jnp.maximum(jnp.maximum(lses[0], lses[1]), lses[2])
    es = [jnp.exp(x - mm) for x in lses]
    o_ref[...] = (es[0] * outs[0] + es[1] * outs[1] + es[2] * outs[2]) / (es[0] + es[1] + es[2])


def _s_attention(q, k, v, inv, cache_k, cache_v):
    hd = (N_HEADS, HEAD_DIM)
    tok = pl.BlockSpec((None,) + hd, lambda b: (b, 0, 0))
    args, specs = [], []
    for _, d in BRANCHES:
        nrow = CACHE_LEN // d
        last = nrow // BAND - 1
        spec = pl.BlockSpec((None, BAND, None) + hd, lambda b, last=last: (b, last, 0, 0, 0))
        for c in (cache_k, cache_v):
            args.append(c.reshape(DEC_BATCH, nrow, d, *hd))
            specs.append(spec)
    out = jax.ShapeDtypeStruct((DEC_BATCH,) + hd, F32)
    return pl.pallas_call(
        _s_attn_kernel, grid=(DEC_BATCH,),
        in_specs=[tok, tok, tok, pl.BlockSpec((1, HEAD_DIM), lambda b: (0, 0))] + specs,
        out_specs=[tok, tok], out_shape=(out, out),
        compiler_params=_cparams(("arbitrary",)), name="s_attn",
    )(q, k, v, inv, *args)


def _s_outproj_kernel(xa_ref, lga_ref, x_ref, ms_ref, o_ref, ga_ref, w_ref, gf_ref, wr_ref, br_ref,
                      xm_ref, lg_ref):
    del xa_ref, lga_ref
    mix = jnp.concatenate([ms_ref[...], _rms(o_ref[...], ga_ref[...])], axis=1)
    xm = x_ref[...] + jnp.dot(mix, w_ref[...], preferred_element_type=F32, precision=HIGHEST)
    h = _rms(xm, gf_ref[...])
    lg = jnp.dot(h, wr_ref[...], preferred_element_type=F32, precision=HIGHEST) + br_ref[...]
    pad = TM - DEC_BATCH
    xm_ref[...] = jnp.concatenate([xm, jnp.zeros((pad, D_MODEL), F32)], axis=0)
    lg_ref[...] = jnp.concatenate([lg, jnp.zeros((pad, 128), F32)], axis=0)


def _s_outproj(xm_all, lg_all, x, mix_ssm, o_attn, g_attn, w, g_ffn, wr, br):
    last = lambda i: (N_TOK_BLOCKS - 1, 0)
    full = lambda shape: pl.BlockSpec(shape, lambda i: (0, 0))
    any_spec = pl.BlockSpec(memory_space=pl.ANY)
    return pl.pallas_call(
        _s_outproj_kernel, grid=(1,),
        in_specs=[any_spec, any_spec,
                  full((DEC_BATCH, D_MODEL)), full((DEC_BATCH, SSM_WIDTH)), full((DEC_BATCH, ATTN_WIDTH)),
                  full((1, ATTN_WIDTH)), full((D_MODEL, D_MODEL)), full((1, D_MODEL)),
                  full((D_MODEL, 128)), full((1, 128))],
        out_specs=[pl.BlockSpec((TM, D_MODEL), last), pl.BlockSpec((TM, 128), last)],
        out_shape=(jax.ShapeDtypeStruct(xm_all.shape, F32), jax.ShapeDtypeStruct(lg_all.shape, F32)),
        input_output_aliases={0: 0, 1: 1},
        compiler_params=_cparams(("arbitrary",)), name="s_outproj",
    )(xm_all, lg_all, x, mix_ssm, o_attn, g_attn, w, g_ffn, wr, br)


def _route_kernel(lg_ref, ri_ref, rw_ref, cnt_ref, carry):
    i = pl.program_id(0)

    @pl.when(i == 0)
    def _():
        carry[...] = jnp.zeros((1, 128), F32)

    x = lg_ref[...]
    lane = lax.broadcasted_iota(I32, (TM, 128), 1)
    row = lax.broadcasted_iota(I32, (TM, 128), 0) + i * TM
    valid = row < T_REAL
    big = jnp.int32(1 << 20)
    gmask = lane < N_EXPERT_GROUPS
    lgm = jnp.where(gmask, x, NEG)
    m = jnp.max(lgm, axis=-1, keepdims=True)
    gate1 = 1.0 / jnp.sum(jnp.where(gmask, jnp.exp(lgm - m), 0.0), axis=-1, keepdims=True)
    grp = jnp.min(jnp.where(gmask & (lgm == m), lane, big), axis=-1, keepdims=True)
    lo = N_EXPERT_GROUPS + EXPERTS_PER_GROUP * grp
    emask = (lane >= lo) & (lane < lo + EXPERTS_PER_GROUP)
    le = jnp.where(emask, x, NEG)
    t1 = jnp.max(le, axis=-1, keepdims=True)
    i1 = jnp.min(jnp.where(emask & (le == t1), lane, big), axis=-1, keepdims=True)
    emask2 = emask & (lane != i1)
    le2 = jnp.where(emask2, x, NEG)
    t2 = jnp.max(le2, axis=-1, keepdims=True)
    i2 = jnp.min(jnp.where(emask2 & (le2 == t2), lane, big), axis=-1, keepdims=True)
    e21 = jnp.exp(t2 - t1)
    w1 = gate1 / (1.0 + e21)
    w2 = gate1 * e21 / (1.0 + e21)
    eid1, eid2 = i1 - N_EXPERT_GROUPS, i2 - N_EXPERT_GROUPS
    oh1 = jnp.where(valid & (lane == eid1), 1.0, 0.0)
    oh2 = jnp.where(valid & (lane == eid2), 1.0, 0.0)
    a = oh1 + oh2
    rr = lax.broadcasted_iota(I32, (TM, TM), 0)
    cc = lax.broadcasted_iota(I32, (TM, TM), 1)
    before = jnp.where(cc < rr, 1.0, 0.0).astype(BF16)
    pre = jnp.dot(before, a.astype(BF16), preferred_element_type=F32) + carry[...]
    rank1 = jnp.sum(oh1 * pre, axis=-1, keepdims=True).astype(I32)
    rank2 = jnp.sum(oh2 * pre, axis=-1, keepdims=True).astype(I32)
    carry[...] = carry[...] + jnp.sum(a, axis=0, keepdims=True)
    cnt_ref[...] = jnp.broadcast_to(carry[...], (8, 128))
    zi = jnp.zeros((TM, 128), I32)
    ri = jnp.where(lane == 0, eid1, jnp.where(lane == 1, eid2, jnp.where(lane == 2, rank1, jnp.where(lane == 3, rank2, zi))))
    ri_ref[...] = jnp.where(valid, ri, zi)
    rw = jnp.where(lane == 0, w1, jnp.where(lane == 1, w2, 0.0))
    rw_ref[...] = jnp.where(valid, rw, 0.0)


def _route(logits):
    blk = pl.BlockSpec((TM, 128), lambda i: (i, 0))
    return pl.pallas_call(
        _route_kernel, grid=(N_TOK_BLOCKS,), in_specs=[blk],
        out_specs=[blk, blk, pl.BlockSpec((8, 128), lambda i: (0, 0))],
        out_shape=(jax.ShapeDtypeStruct((T_PAD, 128), I32), jax.ShapeDtypeStruct((T_PAD, 128), F32),
                   jax.ShapeDtypeStruct((8, 128), F32)),
        scratch_shapes=[pltpu.VMEM((1, 128), F32)],
        compiler_params=_cparams(("arbitrary",)), name="route",
    )(logits)


def _row_copy(src, s, dst, d, sem):
    return pltpu.make_async_copy(src.at[pl.ds(s, 1)], dst.at[pl.ds(d, 1)], sem)


def _invert_kernel(poff_ref, ri_ref, dest_ref, rt_ref):
    i = pl.program_id(0)

    @pl.when(i == 0)
    def _():
        def clear(p, c):
            rt_ref[p] = 0
            return c
        lax.fori_loop(0, N_MOE_ROWS, clear, 0, unroll=32)

    def put(r, c):
        t = i * TM + r
        for s in range(2):
            d = poff_ref[ri_ref[0, 0, 4 * r + s]] + ri_ref[0, 0, 4 * r + 2 + s]
            dest_ref[0, 0, 2 * r + s] = d
            rt_ref[d] = t
        return c

    def blank(r, c):
        dest_ref[0, 0, 2 * r] = 0
        dest_ref[0, 0, 2 * r + 1] = 0
        return c

    @pl.when(i < T_PROMPT // TM)
    def _():
        lax.fori_loop(0, TM, put, 0, unroll=8)

    @pl.when(i == T_PROMPT // TM)
    def _():
        lax.fori_loop(0, DEC_BATCH, put, 0, unroll=8)
        lax.fori_loop(DEC_BATCH, TM, blank, 0, unroll=8)


def _invert(poff, ri4):
    grid_spec = pltpu.PrefetchScalarGridSpec(
        num_scalar_prefetch=1, grid=(N_TOK_BLOCKS,),
        in_specs=[pl.BlockSpec((1, 1, 4 * TM), lambda i, po: (i, 0, 0), memory_space=pltpu.SMEM)],
        out_specs=[pl.BlockSpec((1, 1, 2 * TM), lambda i, po: (i, 0, 0), memory_space=pltpu.SMEM),
                   pl.BlockSpec(memory_space=pltpu.SMEM)])
    return pl.pallas_call(
        _invert_kernel, grid_spec=grid_spec,
        out_shape=(jax.ShapeDtypeStruct((N_TOK_BLOCKS, 1, 2 * TM), I32),
                   jax.ShapeDtypeStruct((N_MOE_ROWS,), I32)),
        compiler_params=_cparams(("arbitrary",)), name="moe_invert",
    )(poff, ri4)


def _expert_kernel(fb_ref, nv_ref, rt_ref, xm_ref, gf_ref, wg_ref, wu_ref, wd_ref, y_ref,
                   xbuf, ybuf, gsem, ysem, wgu_scr, wd_scr):
    e = pl.program_id(0)
    first, end, total = fb_ref[e], fb_ref[e + 1], fb_ref[N_EXPERTS]

    def y_copy(g, slot):
        return pltpu.make_async_copy(ybuf.at[slot], y_ref.at[pl.ds(g * MOE_BLK, MOE_BLK)], ysem.at[slot])

    def gather(g, slot, wait):
        def body(r, c):
            if wait:
                _row_copy(xm_ref, 0, xbuf.at[slot], 0, gsem.at[slot]).wait()
            else:
                _row_copy(xm_ref, rt_ref[g * MOE_BLK + r], xbuf.at[slot], r,
                          gsem.at[slot]).start(priority=GATHER_PRIORITY)
            return c
        lax.fori_loop(0, nv_ref[g], body, 0)

    @pl.when(e == 0)
    def _():
        xbuf[...] = jnp.zeros(xbuf.shape, F32)
        for d in range(GATHER_DEPTH - 1):
            gather(d, d, wait=False)

    @pl.when(end > first)
    def _():
        wgu_scr[:, 0:EXPERT_FF] = wg_ref[0].astype(BF16)
        wgu_scr[:, EXPERT_FF:] = wu_ref[0].astype(BF16)
        wd_scr[...] = wd_ref[0].astype(BF16)

    def block(g, carry):
        slot = g % 2
        xslot = g % GATHER_DEPTH

        gather(g + GATHER_DEPTH - 1, (g + GATHER_DEPTH - 1) % GATHER_DEPTH, wait=False)
        gather(g, xslot, wait=True)
        h = _rms(xbuf[xslot], gf_ref[...]).astype(BF16)
        gu = jnp.dot(h, wgu_scr[...], preferred_element_type=F32)
        gate, up = gu[:, :EXPERT_FF], gu[:, EXPERT_FF:]
        act = (gate * _sigmoid(gate) * up).astype(BF16)
        y = jnp.dot(act, wd_scr[...], preferred_element_type=F32)

        @pl.when(g >= 2)
        def _():
            y_copy(g - 2, slot).wait()
        ybuf[slot] = y
        y_copy(g, slot).start()
        return carry

    lax.fori_loop(first, end, block, 0)

    @pl.when(e == N_EXPERTS - 1)
    def _():
        @pl.when(total >= 2)
        def _():
            y_copy(total - 2, total % 2).wait()

        @pl.when(total >= 1)
        def _():
            y_copy(total - 1, (total - 1) % 2).wait()

        ybuf[0] = jnp.zeros((MOE_BLK, D_MODEL), F32)

        def zero_start(g, c):
            y_copy(g, 0).start()
            return c

        def zero_wait(g, c):
            y_copy(g, 0).wait()
            return c

        lax.fori_loop(total, N_MOE_BLOCKS, zero_start, 0)
        lax.fori_loop(total, N_MOE_BLOCKS, zero_wait, 0)


def _experts(first_blk, nvalid, row_tok, xm_all, g_ffn, w_g, w_u, w_d):
    grid_spec = pltpu.PrefetchScalarGridSpec(
        num_scalar_prefetch=3, grid=(N_EXPERTS,),
        in_specs=[pl.BlockSpec(memory_space=pl.ANY),
                  pl.BlockSpec((1, D_MODEL), lambda e, fb, nv, rt: (0, 0)),
                  pl.BlockSpec((1, D_MODEL, EXPERT_FF), lambda e, fb, nv, rt: (e, 0, 0)),
                  pl.BlockSpec((1, D_MODEL, EXPERT_FF), lambda e, fb, nv, rt: (e, 0, 0)),
                  pl.BlockSpec((1, EXPERT_FF, D_MODEL), lambda e, fb, nv, rt: (e, 0, 0))],
        out_specs=pl.BlockSpec(memory_space=pl.ANY),
        scratch_shapes=[pltpu.VMEM((GATHER_DEPTH, MOE_BLK, D_MODEL), F32), pltpu.VMEM((2, MOE_BLK, D_MODEL), F32),
                        pltpu.SemaphoreType.DMA((GATHER_DEPTH,)), pltpu.SemaphoreType.DMA((2,)),
                        pltpu.VMEM((D_MODEL, 2 * EXPERT_FF), BF16), pltpu.VMEM((EXPERT_FF, D_MODEL), BF16)])
    return pl.pallas_call(
        _expert_kernel, grid_spec=grid_spec,
        out_shape=jax.ShapeDtypeStruct((N_MOE_ROWS, D_MODEL), F32),
        compiler_params=_cparams(("arbitrary",)), name="moe_experts",
    )(first_blk, nvalid, row_tok, xm_all, g_ffn, w_g, w_u, w_d)


def _combine_kernel(dest_ref, destn_ref, rw_ref, xm_ref, g_ref, y_ref, o_ref, ybuf, sems, *, rows, nblk):
    i = pl.program_id(0)
    slot = i % 2

    def gather(d_ref, s, wait):
        def body(r, c):
            for j in range(2):
                cp = _row_copy(y_ref, 0 if wait else d_ref[0, 0, 2 * r + j], ybuf.at[s, j], r, sems.at[s])
                cp.wait() if wait else cp.start(priority=j)
            return c
        lax.fori_loop(0, rows, body, 0, unroll=8)

    @pl.when(i == 0)
    def _():
        gather(dest_ref, 0, wait=False)

    @pl.when(i + 1 < nblk)
    def _():
        gather(destn_ref, 1 - slot, wait=False)

    gather(dest_ref, slot, wait=True)
    w = rw_ref[...]
    x = xm_ref[...] + w[:, 0:1] * ybuf[slot, 0] + w[:, 1:2] * ybuf[slot, 1]
    o_ref[...] = _rms(x, g_ref[...]).reshape(o_ref.shape)


def _combine(dest3, rw, xm, g, y_rows, rows, nblk, blk0, out_spec, out_shape):
    smem_blk = lambda f: pl.BlockSpec((1, 1, 2 * rows), f, memory_space=pltpu.SMEM)
    return pl.pallas_call(
        functools.partial(_combine_kernel, rows=rows, nblk=nblk), grid=(nblk,),
        in_specs=[smem_blk(lambda i: (i, 0, 0)),
                  smem_blk(lambda i: (jnp.minimum(i + 1, nblk - 1), 0, 0)),
                  pl.BlockSpec((rows, 128), lambda i: (blk0 + i, 0)),
                  pl.BlockSpec((rows, D_MODEL), lambda i: (blk0 + i, 0)),
                  pl.BlockSpec((1, D_MODEL), lambda i: (0, 0)),
                  pl.BlockSpec(memory_space=pl.ANY)],
        out_specs=out_spec, out_shape=out_shape,
        scratch_shapes=[pltpu.VMEM((2, 2, rows, D_MODEL), F32), pltpu.SemaphoreType.DMA((2,))],
        compiler_params=_cparams(("arbitrary",)), name="moe_combine",
    )(dest3, dest3, rw, xm, g, y_rows)


def kernel(x_prompt, x_sample, cache_k, cache_v, state_ssm_re, state_ssm_im, g_mix_norm, w_in, ssm_a_re, ssm_a_im, ssm_log_dt, ssm_b_re, ssm_b_im, ssm_c_re, ssm_c_im, ssm_d, w_glu, g_ssm_out, g_attn_out, w_out, g_ffn_norm, w_router_group, b_router_group, w_router_expert, b_router_expert, w_expert_gate, w_expert_up, w_expert_down, g_final):
    row = lambda a: a.reshape(1, -1)
    half = HEAD_DIM // 2
    inv = ROPE_THETA ** (-jnp.arange(half, dtype=F32) / half)
    inv = jnp.concatenate([inv, inv]).reshape(1, HEAD_DIM)

    lam_r, lam_i, lams_r, lams_i, bb_r, bb_i = _ssm_params(ssm_a_re[0], ssm_a_im[0], ssm_log_dt[0], ssm_b_re[0], ssm_b_im[0])
    bm = jnp.concatenate([_block_diag_in(bb_r), _block_diag_in(bb_i)], axis=-1)
    cm = jnp.concatenate([_block_diag_out(ssm_c_re[0]), -_block_diag_out(ssm_c_im[0])], axis=1)
    dskip = row(ssm_d[0])
    bcast8 = lambda a: jnp.broadcast_to(a.reshape(1, -1), (N_SEG, N_GROUPS * SSM_STATE))
    lr8, li8, lsr8, lsi8 = bcast8(lam_r), bcast8(lam_i), bcast8(lams_r), bcast8(lams_i)

    wr = jnp.concatenate([w_router_group[0], w_router_expert[0].reshape(D_MODEL, N_EXPERTS)], axis=1)
    wr = jnp.pad(wr, ((0, 0), (0, 128 - wr.shape[1])))
    br = jnp.pad(jnp.concatenate([b_router_group[0], b_router_expert[0].reshape(-1)]), (0, 128 - 36)).reshape(1, 128)

    x4 = x_prompt.reshape(BATCH, N_SEG, SEG_LEN, D_MODEL)
    u_il, q4, k4, v4, k_new, v_new = _inproj(x4, row(g_mix_norm[0]), inv, w_in[0].astype(BF16))
    bm16, cm16 = bm.astype(BF16), cm.astype(BF16)
    ends = _ssm_scan(u_il, bm16, lr8, li8, final=False)
    z_il, xend = _ssm_scan(u_il, bm16, lr8, li8, final=True, e=ends, lsr8=lsr8, lsi8=lsi8, cm=cm16, dskip=dskip)
    flat = lambda a: a.reshape(BATCH, SEQ, ATTN_WIDTH)
    o_attn = _attention(flat(q4), flat(k4), flat(v4))
    xm_all, lg_all = _outproj(
        x4, z_il, o_attn.reshape(BATCH, N_SEG, SEG_LEN, ATTN_WIDTH), w_glu[0].astype(BF16), row(g_ssm_out[0]),
        row(g_attn_out[0]), w_out[0].astype(BF16), row(g_ffn_norm[0]), wr.astype(BF16), br)

    xs = x_sample.reshape(DEC_BATCH, D_MODEL)
    proj = _s_inproj(xs, row(g_mix_norm[0]), w_in[0])
    us = proj[:, :SSM_WIDTH]
    qkv = proj[:, SSM_WIDTH:].reshape(DEC_BATCH, 3, N_HEADS, HEAD_DIM)
    zs, ns_r, ns_i = _s_ssm(us, state_ssm_re[0].reshape(DEC_BATCH, -1), state_ssm_im[0].reshape(DEC_BATCH, -1),
                            row(lam_r), row(lam_i), bm, cm, dskip)
    mix_ssm_s = _s_glu(zs, w_glu[0], row(g_ssm_out[0]))
    o_s, k_rot = _s_attention(qkv[:, 0], qkv[:, 1], qkv[:, 2], inv, cache_k[0], cache_v[0])
    xm_all, lg_all = _s_outproj(
        xm_all, lg_all, xs, mix_ssm_s, o_s.reshape(DEC_BATCH, ATTN_WIDTH), row(g_attn_out[0]),
        w_out[0], row(g_ffn_norm[0]), wr, br)

    ri, rw, cnt = _route(lg_all)
    counts = cnt[0, :N_EXPERTS].astype(I32)
    padded = (counts + MOE_BLK - 1) // MOE_BLK * MOE_BLK
    pend = jnp.cumsum(padded)
    poff = pend - padded
    first_blk = jnp.concatenate([poff, pend[-1:]]) // MOE_BLK
    blk_start = jnp.arange(N_MOE_BLOCKS + GATHER_DEPTH, dtype=I32) * MOE_BLK
    block_e = jnp.minimum(jnp.sum(pend[None, :] <= blk_start[:, None], axis=1), N_EXPERTS - 1)
    nvalid = jnp.clip(counts[block_e] - (blk_start - poff[block_e]), 0, MOE_BLK).astype(I32)
    dest3, row_tok = _invert(poff, ri[:, 0:4].reshape(N_TOK_BLOCKS, 1, 4 * TM))
    y_rows = _experts(first_blk, nvalid, row_tok, xm_all, row(g_ffn_norm[0]),
                      w_expert_gate[0], w_expert_up[0], w_expert_down[0])
    gfin = row(g_final)
    nkb = SEG_LEN // TK
    y_prompt = _combine(dest3, rw, xm_all, gfin, y_rows, TM, T_PROMPT // TM, 0,
                        pl.BlockSpec((1, N_SEG, TK, D_MODEL), lambda i: (i // nkb, 0, i % nkb, 0)),
                        jax.ShapeDtypeStruct((BATCH, N_SEG, SEG_LEN, D_MODEL), F32))
    dest_s = dest3[N_TOK_BLOCKS - 1:, :, :2 * DEC_BATCH]
    y_sample = _combine(dest_s, rw, xm_all, gfin, y_rows, DEC_BATCH, 1, T_PROMPT // DEC_BATCH,
                        pl.BlockSpec((DEC_BATCH, D_MODEL), lambda i: (0, 0)),
                        jax.ShapeDtypeStruct((DEC_BATCH, D_MODEL), F32))

    kv_shape = (1, BATCH, CACHE_LEN, N_HEADS, HEAD_DIM)
    st_p = lambda a: a[:, :, N_SEG - 1, :].reshape(1, BATCH, N_GROUPS, SSM_STATE)
    st_s = lambda a: a.reshape(1, DEC_BATCH, N_GROUPS, SSM_STATE)
    kvs = lambda a: a.reshape(1, DEC_BATCH, 1, N_HEADS, HEAD_DIM)
    return (y_prompt.reshape(BATCH, SEQ, D_MODEL), y_sample.reshape(DEC_BATCH, 1, D_MODEL),
            k_new.reshape(kv_shape), v_new.reshape(kv_shape),
            st_p(xend[..., :CH_ST]), st_p(xend[..., CH_ST:]),
            kvs(k_rot), kvs(qkv[:, 2]), st_s(ns_r), st_s(ns_i))
```

```python
import functools
import math

import jax
import jax.numpy as jnp
from jax import lax
from jax.experimental import pallas as pl
from jax.experimental.pallas import tpu as pltpu

F32 = jnp.float32
BF16 = jnp.bfloat16
I32 = jnp.int32
U32 = jnp.uint32
HIGHEST = lax.Precision.HIGHEST

D_MODEL = 2048
BATCH = 2
SEQ = 4096
DEC_BATCH = 32
PAST_LEN = 8192
CACHE_LEN = 2048
SSM_WIDTH = 1024
ATTN_WIDTH = 1024
SSM_GROUP = 16
N_GROUPS = 64
SSM_STATE = 64
HEAD_DIM = 128
N_HEADS = 8
D_IN = 4096
BRANCHES = ((128, 1), (512, 4), (2048, 16))
BAND = 128
ROPE_THETA = 10000.0
N_EXPERT_GROUPS = 4
EXPERTS_PER_GROUP = 8
N_EXPERTS = 32
EXPERT_FF = 512
NORM_EPS = 1e-6

N_SEG = 8
SEG_LEN = SEQ // N_SEG
N_CHUNK = 4
CH_IN = SSM_WIDTH // N_CHUNK
CH_ST = N_GROUPS * SSM_STATE // N_CHUNK
SSM_TK = 128
TM = 256
TK = TM // N_SEG
TK_SHIFT = TK.bit_length() - 1
LANE = 128
T_PROMPT = BATCH * SEQ
T_REAL = T_PROMPT + DEC_BATCH
T_PAD = T_PROMPT + TM
N_TOK_BLOCKS = T_PAD // TM
MOE_BLK = 256
N_MOE_BLOCKS = -(-(2 * T_REAL) // MOE_BLK) + N_EXPERTS
N_MOE_ROWS = N_MOE_BLOCKS * MOE_BLK
VMEM_LIMIT = 56 * 1024 * 1024
NEG = -1e30
LOG2_E = 1.4426950408889634
CAST_COLS = 512
GATHER_PRIORITY = 1
GATHER_DEPTH = 2


def _cparams(sem, vmem=VMEM_LIMIT):
    return pltpu.CompilerParams(dimension_semantics=sem, vmem_limit_bytes=vmem)


def _rms(x, g):
    return x * lax.rsqrt(jnp.mean(x * x, axis=-1, keepdims=True) + NORM_EPS) * g


def _gelu_tanh(y):
    return 0.5 * y * (1.0 + jnp.tanh(0.7978845608028654 * (y + 0.044715 * (y * y * y))))


def _sigmoid(x):
    return 1.0 / (1.0 + jnp.exp(-x))


def _rope_tables(pos, inv):
    ang = pos * inv
    lane = lax.broadcasted_iota(I32, ang.shape, 1)
    return jnp.cos(ang), jnp.where(lane < HEAD_DIM // 2, -jnp.sin(ang), jnp.sin(ang))


def _rope_heads(x, cos, sin):
    outs = []
    for h in range(x.shape[1] // HEAD_DIM):
        xh = x[:, h * HEAD_DIM:(h + 1) * HEAD_DIM]
        outs.append(xh * cos + pltpu.roll(xh, HEAD_DIM // 2, 1) * sin)
    return jnp.concatenate(outs, axis=1)


def _ssm_param_kernel(are_ref, aim_ref, ldt_ref, btr_ref, bti_ref,
                      lr_ref, li_ref, lsr_ref, lsi_ref, bbr_ref, bbi_ref):
    ar, ai = are_ref[...], aim_ref[...]
    dt = jnp.exp(ldt_ref[...])
    er, ei = ar * dt, ai * dt
    mag = jnp.exp(er)
    lr, li = mag * jnp.cos(ei), mag * jnp.sin(ei)
    lr_ref[...] = lr
    li_ref[...] = li
    mag_s = jnp.exp(er * SEG_LEN)
    lsr_ref[...] = mag_s * jnp.cos(ei * SEG_LEN)
    lsi_ref[...] = mag_s * jnp.sin(ei * SEG_LEN)
    xr, xi = lr - 1.0, li
    den = ar * ar + ai * ai
    cr = (xr * ar + xi * ai) / den
    ci = (xi * ar - xr * ai) / den
    btr, bti = btr_ref[...], bti_ref[...]
    bbr_ref[...] = cr[:, None, :] * btr - ci[:, None, :] * bti
    bbi_ref[...] = cr[:, None, :] * bti + ci[:, None, :] * btr


def _ssm_params(a_re, a_im, log_dt, b_re, b_im):
    g, n, p = N_GROUPS, SSM_STATE, SSM_GROUP
    gn = jax.ShapeDtypeStruct((g, n), F32)
    gpn = jax.ShapeDtypeStruct((g, p, n), F32)
    return pl.pallas_call(
        _ssm_param_kernel, out_shape=(gn, gn, gn, gn, gpn, gpn), name="ssm_params",
    )(a_re, a_im, log_dt.reshape(g, 1), b_re.transpose(0, 2, 1), b_im.transpose(0, 2, 1))


def _blockdiag_kernel(br_ref, bi_ref, cr_ref, ci_ref, bm_ref, bm16_ref, cm_ref, cm16_ref):
    bm_ref[...] = jnp.zeros(bm_ref.shape, F32)
    cm_ref[...] = jnp.zeros(cm_ref.shape, F32)
    gp, gl = 2 * SSM_GROUP, LANE
    keep_b = ((lax.broadcasted_iota(I32, (gp, gl), 0) // SSM_GROUP)
              == (lax.broadcasted_iota(I32, (gp, gl), 1) // SSM_STATE))
    for q in range(N_GROUPS // 2):
        j, p = divmod(q, 8)
        rows, cols = slice(gp * p, gp * (p + 1)), slice(gl * p, gl * (p + 1))
        bm_ref[j, rows, cols] = jnp.where(keep_b, br_ref[q], 0.0)
        bm_ref[j, rows, CH_ST + gl * p:CH_ST + gl * (p + 1)] = jnp.where(keep_b, bi_ref[q], 0.0)
    oct_rows = 8 * SSM_STATE
    keep_c = ((lax.broadcasted_iota(I32, (oct_rows, gl), 0) // SSM_STATE)
              == (lax.broadcasted_iota(I32, (oct_rows, gl), 1) // SSM_GROUP))
    for o in range(N_GROUPS // 8):
        j, half = divmod(o, 2)
        rows, cols = slice(oct_rows * half, oct_rows * (half + 1)), slice(gl * half, gl * (half + 1))
        cm_ref[j, rows, cols] = jnp.where(keep_c, cr_ref[o], 0.0)
        cm_ref[j, CH_ST + oct_rows * half:CH_ST + oct_rows * (half + 1), cols] = jnp.where(keep_c, -ci_ref[o], 0.0)
    bm16_ref[...] = bm_ref[...].astype(BF16)
    cm16_ref[...] = cm_ref[...].astype(BF16)


def _blockdiag(bb_r, bb_i, c_re, c_im):
    pair = lambda m: jnp.tile(m, (1, 1, 2)).reshape(N_GROUPS // 2, 2 * SSM_GROUP, LANE)
    octet = lambda c: jnp.tile(c.transpose(0, 2, 1), (1, 1, 8)).reshape(N_GROUPS // 8, 8 * SSM_STATE, LANE)
    bshape, cshape = (N_CHUNK, CH_IN, 2 * CH_ST), (N_CHUNK, 2 * CH_ST, CH_IN)
    return pl.pallas_call(
        _blockdiag_kernel,
        out_shape=(jax.ShapeDtypeStruct(bshape, F32), jax.ShapeDtypeStruct(bshape, BF16),
                   jax.ShapeDtypeStruct(cshape, F32), jax.ShapeDtypeStruct(cshape, BF16)),
        compiler_params=pltpu.CompilerParams(vmem_limit_bytes=VMEM_LIMIT), name="ssm_blockdiag",
    )(pair(bb_r), pair(bb_i), octet(c_re), octet(c_im))


def _load_weight_bf16(w_hbm, w16, stage, sem):
    n = w_hbm.shape[1] // CAST_COLS

    def chunk(c):
        return pltpu.make_async_copy(w_hbm.at[:, pl.ds(c * CAST_COLS, CAST_COLS)], stage.at[c % 2], sem.at[c % 2])

    chunk(0).start()
    for c in range(n):
        if c + 1 < n:
            chunk(c + 1).start()
        chunk(c).wait()
        w16[:, c * CAST_COLS:(c + 1) * CAST_COLS] = stage[c % 2].astype(BF16)


def _inproj_kernel(x_ref, g_ref, inv_ref, w_hbm, u_ref, q_ref, k_ref, v_ref, kn_ref, vn_ref, w_ref, stage, sem):
    kb = pl.program_id(1)

    @pl.when((pl.program_id(0) == 0) & (kb == 0))
    def _():
        _load_weight_bf16(w_hbm, w_ref, stage, sem)

    h = _rms(x_ref[0].reshape(TM, D_MODEL), g_ref[...]).astype(BF16)
    rid = lax.broadcasted_iota(I32, (TM, 1), 0)
    pos = ((rid >> TK_SHIFT) * SEG_LEN + kb * TK + (rid & (TK - 1))).astype(F32)
    cos, sin = _rope_tables(pos, inv_ref[...])
    w = SSM_WIDTH
    u = jnp.dot(h, w_ref[:, 0:w], preferred_element_type=F32)
    for s in range(N_SEG):
        for c in range(w // LANE):
            u_ref[0, c, pl.ds(s, TK, stride=N_SEG), :] = u[s * TK:(s + 1) * TK, c * LANE:(c + 1) * LANE]
    q_ref[0] = _rope_heads(jnp.dot(h, w_ref[:, w:2 * w], preferred_element_type=F32), cos, sin).reshape(N_SEG, TK, w)
    k = _rope_heads(jnp.dot(h, w_ref[:, 2 * w:3 * w], preferred_element_type=F32), cos, sin)
    v = jnp.dot(h, w_ref[:, 3 * w:4 * w], preferred_element_type=F32)
    k_ref[0] = k.reshape(N_SEG, TK, w)
    v_ref[0] = v.reshape(N_SEG, TK, w)
    kn_ref[0] = k[TM // 2:].reshape(N_SEG // 2, TK, w)
    vn_ref[0] = v[TM // 2:].reshape(N_SEG // 2, TK, w)


def _inproj(x4, g, inv, w):
    const = lambda b, k: (0, 0)
    blk4 = lambda nseg, width: pl.BlockSpec((1, nseg, TK, width), lambda b, k: (b, 0, k, 0))
    qkv = jax.ShapeDtypeStruct((BATCH, N_SEG, SEG_LEN, ATTN_WIDTH), F32)
    kvn = jax.ShapeDtypeStruct((BATCH, N_SEG // 2, SEG_LEN, ATTN_WIDTH), F32)
    return pl.pallas_call(
        _inproj_kernel, grid=(BATCH, SEG_LEN // TK),
        in_specs=[blk4(N_SEG, D_MODEL), pl.BlockSpec((1, D_MODEL), const),
                  pl.BlockSpec((1, HEAD_DIM), const), pl.BlockSpec(memory_space=pl.ANY)],
        out_specs=[pl.BlockSpec((1, SSM_WIDTH // LANE, TM, LANE), lambda b, k: (b, 0, k, 0)),
                   blk4(N_SEG, ATTN_WIDTH), blk4(N_SEG, ATTN_WIDTH), blk4(N_SEG, ATTN_WIDTH),
                   blk4(N_SEG // 2, ATTN_WIDTH), blk4(N_SEG // 2, ATTN_WIDTH)],
        out_shape=(jax.ShapeDtypeStruct((BATCH, SSM_WIDTH // LANE, SEQ, LANE), F32), qkv, qkv, qkv, kvn, kvn),
        scratch_shapes=[pltpu.VMEM((D_MODEL, D_IN), BF16), pltpu.VMEM((2, D_MODEL, CAST_COLS), F32),
                        pltpu.SemaphoreType.DMA((2,))],
        compiler_params=_cparams(("arbitrary",) * 2), name="inproj",
    )(x4, g, inv, w)


def _ssm_scan_kernel(*refs, final):
    if final:
        (u_ref, bm_ref, lr_ref, li_ref, e_ref, lsr_ref, lsi_ref, cm_ref, d_ref,
         z_ref, xend_ref, bu_scr, xr_scr, xi_scr) = refs
    else:
        u_ref, bm_ref, lr_ref, li_ref, xend_ref, bu_scr, xr_scr, xi_scr = refs
    kb = pl.program_id(2)

    @pl.when(kb == 0)
    def _():
        if final:
            lsr, lsi = lsr_ref[0:1, :], lsi_ref[0:1, :]
            seg_id = lax.broadcasted_iota(I32, (N_SEG, CH_ST), 0)
            pr = pi = jnp.zeros((1, CH_ST), F32)
            xr0 = xi0 = jnp.zeros((N_SEG, CH_ST), F32)
            for s in range(1, N_SEG):
                er = e_ref[0, 0, s - 1:s, 0:CH_ST]
                ei = e_ref[0, 0, s - 1:s, CH_ST:2 * CH_ST]
                pr, pi = er + lsr * pr - lsi * pi, ei + lsr * pi + lsi * pr
                xr0 = jnp.where(seg_id == s, pr, xr0)
                xi0 = jnp.where(seg_id == s, pi, xi0)
            xr_scr[...] = xr0
            xi_scr[...] = xi0
        else:
            xr_scr[...] = jnp.zeros((N_SEG, CH_ST), F32)
            xi_scr[...] = jnp.zeros((N_SEG, CH_ST), F32)

    u = jnp.concatenate([u_ref[0, c] for c in range(CH_IN // LANE)], axis=1)
    bu_scr[...] = jnp.dot(u.astype(BF16), bm_ref[0], preferred_element_type=F32)
    lr, li = lr_ref[...], li_ref[...]

    def step(k, carry):
        xr, xi = carry
        r0 = pl.multiple_of(k * N_SEG, N_SEG)
        bur = bu_scr[pl.ds(r0, N_SEG), 0:CH_ST]
        bui = bu_scr[pl.ds(r0, N_SEG), CH_ST:2 * CH_ST]
        nr = lr * xr - li * xi + bur
        ni = lr * xi + li * xr + bui
        if final:
            bu_scr[pl.ds(r0, N_SEG), 0:CH_ST] = nr
            bu_scr[pl.ds(r0, N_SEG), CH_ST:2 * CH_ST] = ni
        return nr, ni

    xr, xi = lax.fori_loop(0, SSM_TK, step, (xr_scr[...], xi_scr[...]), unroll=4)
    xr_scr[...] = xr
    xi_scr[...] = xi
    if final:
        y = jnp.dot(bu_scr[...].astype(BF16), cm_ref[0], preferred_element_type=F32)
        z = _gelu_tanh(y + d_ref[...] * u)
        for c in range(CH_IN // LANE):
            z_ref[0, c] = z[:, c * LANE:(c + 1) * LANE]

    @pl.when(kb == pl.num_programs(2) - 1)
    def _():
        xend_ref[0, 0] = jnp.concatenate([xr, xi], axis=1)


def _ssm_scan(u_perm, bm, lr8, li8, final, e=None, lsr8=None, lsi8=None, cm=None, dskip=None):
    rows = SSM_TK * N_SEG
    nkb = SEG_LEN // SSM_TK
    in_specs = [pl.BlockSpec((1, CH_IN // LANE, rows, LANE), lambda b, j, k: (b, j, k, 0)),
                pl.BlockSpec((1, CH_IN, 2 * CH_ST), lambda b, j, k: (j, 0, 0)),
                pl.BlockSpec((N_SEG, CH_ST), lambda b, j, k: (0, j)),
                pl.BlockSpec((N_SEG, CH_ST), lambda b, j, k: (0, j))]
    args = [u_perm, bm, lr8, li8]
    xend_spec = pl.BlockSpec((1, 1, N_SEG, 2 * CH_ST), lambda b, j, k: (b, j, 0, 0))
    xend_shape = jax.ShapeDtypeStruct((BATCH, N_CHUNK, N_SEG, 2 * CH_ST), F32)
    if final:
        in_specs += [xend_spec,
                     pl.BlockSpec((N_SEG, CH_ST), lambda b, j, k: (0, j)),
                     pl.BlockSpec((N_SEG, CH_ST), lambda b, j, k: (0, j)),
                     pl.BlockSpec((1, 2 * CH_ST, CH_IN), lambda b, j, k: (j, 0, 0)),
                     pl.BlockSpec((1, CH_IN), lambda b, j, k: (0, j))]
        args += [e, lsr8, lsi8, cm, dskip]
        out_specs = [pl.BlockSpec((1, CH_IN // LANE, rows, LANE), lambda b, j, k: (b, j, k, 0)), xend_spec]
        out_shape = (jax.ShapeDtypeStruct((BATCH, SSM_WIDTH // LANE, SEQ, LANE), F32), xend_shape)
    else:
        out_specs = xend_spec
        out_shape = xend_shape
    return pl.pallas_call(
        functools.partial(_ssm_scan_kernel, final=final), grid=(BATCH, N_CHUNK, nkb),
        in_specs=in_specs, out_specs=out_specs, out_shape=out_shape,
        scratch_shapes=[pltpu.VMEM((rows, 2 * CH_ST), F32),
                        pltpu.VMEM((N_SEG, CH_ST), F32), pltpu.VMEM((N_SEG, CH_ST), F32)],
        compiler_params=_cparams(("arbitrary",) * 3),
        name="ssm_scan_final" if final else "ssm_scan_ends",
    )(*args)


def _attn_kernel(q_ref, k_ref, v_ref, o_ref, o_scr, lse_scr, bias_scr):
    scale2 = HEAD_DIM ** -0.5 * LOG2_E
    qi = lax.broadcasted_iota(I32, (BAND, 2 * BAND), 0)
    kj = lax.broadcasted_iota(I32, (BAND, 2 * BAND), 1)
    cur_ok = (kj >= BAND) & (kj - BAND <= qi)
    prev_ok = (kj < BAND) & (kj >= qi)
    bias_scr[0] = jnp.where(cur_ok, 0.0, NEG)
    bias_scr[1] = jnp.where(cur_ok | prev_ok, 0.0, NEG)
    nt = (((1,), (1,)), ((), ()))
    for bi, (_, d) in enumerate(BRANCHES):
        nblk = SEQ // (BAND * d)
        shift = nblk.bit_length() - 1

        def tile(i, c, bi=bi, d=d, nblk=nblk, shift=shift):
            r = i >> shift
            ib = i & (nblk - 1)
            start = r + ib * (BAND * d)
            pstart = jnp.maximum(start - BAND * d, r)
            cur = pl.ds(start, BAND, stride=d)
            prev = pl.ds(pstart, BAND, stride=d)
            q = (q_ref[0, cur, :] * scale2).astype(BF16)
            k2 = jnp.concatenate([k_ref[0, prev, :], k_ref[0, cur, :]], axis=0).astype(BF16)
            v2 = jnp.concatenate([v_ref[0, prev, :], v_ref[0, cur, :]], axis=0).astype(BF16)
            s = lax.dot_general(q, k2, nt, preferred_element_type=F32) + bias_scr[jnp.minimum(ib, 1)]
            m = jnp.max(s, axis=-1, keepdims=True)
            p = jnp.exp2(s - m)
            l = jnp.sum(p, axis=-1, keepdims=True)
            o = jnp.dot(p.astype(BF16), v2, preferred_element_type=F32) * (1.0 / l)
            o_scr[bi, cur, :] = o
            lse_scr[bi, cur, :] = jnp.broadcast_to(m + jnp.log2(l), (BAND, HEAD_DIM))
            return c

        def softmax_pv(s, v, rows_out, bi=bi):
            m = jnp.max(s, axis=-1, keepdims=True)
            p = jnp.exp2(s - m)
            l = jnp.sum(p, axis=-1, keepdims=True)
            o_scr[bi, rows_out, :] = jnp.dot(p.astype(BF16), v, preferred_element_type=F32) * (1.0 / l)
            lse_scr[bi, rows_out, :] = jnp.broadcast_to(m + jnp.log2(l), (BAND, HEAD_DIM))

        def residue(r, c, d=d):
            sub = pl.ds(r, 2 * BAND, stride=d)
            q = (q_ref[0, sub, :] * scale2).astype(BF16)
            k = k_ref[0, sub, :].astype(BF16)
            v = v_ref[0, sub, :].astype(BF16)
            s0 = lax.dot_general(q[:BAND], k[:BAND], nt, preferred_element_type=F32) + bias_scr[1, :, BAND:]
            softmax_pv(s0, v[:BAND], pl.ds(r, BAND, stride=d))
            s1 = lax.dot_general(q[BAND:], k, nt, preferred_element_type=F32) + bias_scr[1]
            softmax_pv(s1, v, pl.ds(r + BAND * d, BAND, stride=d))
            return c

        if nblk == 2:
            lax.fori_loop(0, d, residue, 0, unroll=16)
        else:
            lax.fori_loop(0, SEQ // BAND, tile, 0, unroll=32)

    rows = 512

    def merge(i, c):
        sl = pl.ds(pl.multiple_of(i * rows, rows), rows)
        l0, l1, l2 = lse_scr[0, sl, :], lse_scr[1, sl, :], lse_scr[2, sl, :]
        m = jnp.maximum(jnp.maximum(l0, l1), l2)
        e0, e1, e2 = jnp.exp2(l0 - m), jnp.exp2(l1 - m), jnp.exp2(l2 - m)
        o_ref[0, sl, :] = ((e0 * o_scr[0, sl, :] + e1 * o_scr[1, sl, :] + e2 * o_scr[2, sl, :])
                           * (1.0 / (e0 + e1 + e2)))
        return c

    lax.fori_loop(0, SEQ // rows, merge, 0)


def _attention(q, k, v):
    spec = pl.BlockSpec((1, SEQ, HEAD_DIM), lambda b, h: (b, 0, h))
    return pl.pallas_call(
        _attn_kernel, grid=(BATCH, N_HEADS), in_specs=[spec, spec, spec], out_specs=spec,
        out_shape=jax.ShapeDtypeStruct((BATCH, SEQ, ATTN_WIDTH), F32),
        scratch_shapes=[pltpu.VMEM((3, SEQ, HEAD_DIM), F32), pltpu.VMEM((3, SEQ, HEAD_DIM), F32),
                        pltpu.VMEM((2, BAND, 2 * BAND), F32)],
        compiler_params=_cparams(("arbitrary",) * 2), name="dilated_attn",
    )(q, k, v)


def _outproj_kernel(x_ref, z_ref, o_ref, wglu_hbm, gs_ref, ga_ref, w_hbm, gf_ref, wr_ref, br_ref, xm_ref, lg_ref,
                    wglu_ref, w_ref, stage, sem):
    is_prompt = pl.program_id(0) < T_PROMPT // TM

    @pl.when(pl.program_id(0) == 0)
    def _():
        _load_weight_bf16(wglu_hbm, wglu_ref, stage.at[:, 0:SSM_WIDTH], sem)
        _load_weight_bf16(w_hbm, w_ref, stage, sem)

    @pl.when(is_prompt)
    def _():
        z = jnp.concatenate(
            [jnp.concatenate([z_ref[0, c, pl.ds(s, TK, stride=N_SEG), :] for c in range(SSM_WIDTH // LANE)], axis=1)
             for s in range(N_SEG)], axis=0)
        zz = z * _sigmoid(jnp.dot(z.astype(BF16), wglu_ref[...], preferred_element_type=F32))
        ms = _rms(zz, gs_ref[...]).astype(BF16)
        ma = _rms(o_ref[0].reshape(TM, ATTN_WIDTH), ga_ref[...]).astype(BF16)
        xm = (x_ref[0].reshape(TM, D_MODEL) + jnp.dot(ms, w_ref[0:SSM_WIDTH, :], preferred_element_type=F32)
              + jnp.dot(ma, w_ref[SSM_WIDTH:, :], preferred_element_type=F32))
        xm_ref[...] = xm
        h = _rms(xm, gf_ref[...]).astype(BF16)
        lg_ref[...] = jnp.dot(h, wr_ref[...], preferred_element_type=F32) + br_ref[...]

    @pl.when(jnp.logical_not(is_prompt))
    def _():
        xm_ref[...] = jnp.zeros((TM, D_MODEL), F32)
        lg_ref[...] = jnp.zeros((TM, 128), F32)


def _outproj(x4, z_slab, o_attn4, wglu, g_ssm, g_attn, w, g_ffn, wr_bf16, br):
    nkb = SEG_LEN // TK

    def split(i):
        i = jnp.minimum(i, T_PROMPT // TM - 1)
        return i // nkb, i % nkb

    def blk4(i):
        b, k = split(i)
        return (b, 0, k, 0)

    row = lambda i: (i, 0)
    const = lambda i: (0, 0)
    return pl.pallas_call(
        _outproj_kernel, grid=(N_TOK_BLOCKS,),
        in_specs=[pl.BlockSpec((1, N_SEG, TK, D_MODEL), blk4),
                  pl.BlockSpec((1, SSM_WIDTH // LANE, TM, LANE), blk4),
                  pl.BlockSpec((1, N_SEG, TK, ATTN_WIDTH), blk4),
                  pl.BlockSpec(memory_space=pl.ANY),
                  pl.BlockSpec((1, SSM_WIDTH), const),
                  pl.BlockSpec((1, ATTN_WIDTH), const),
                  pl.BlockSpec(memory_space=pl.ANY),
                  pl.BlockSpec((1, D_MODEL), const),
                  pl.BlockSpec((D_MODEL, 128), const),
                  pl.BlockSpec((1, 128), const)],
        out_specs=[pl.BlockSpec((TM, D_MODEL), row), pl.BlockSpec((TM, 128), row)],
        out_shape=(jax.ShapeDtypeStruct((T_PAD, D_MODEL), F32), jax.ShapeDtypeStruct((T_PAD, 128), F32)),
        scratch_shapes=[pltpu.VMEM((SSM_WIDTH, SSM_WIDTH), BF16), pltpu.VMEM((D_MODEL, D_MODEL), BF16),
                        pltpu.VMEM((2, D_MODEL, CAST_COLS), F32), pltpu.SemaphoreType.DMA((2,))],
        compiler_params=_cparams(("arbitrary",)), name="outproj",
    )(x4, z_slab, o_attn4, wglu, g_ssm, g_attn, w, g_ffn, wr_bf16, br)


def _s_inproj_kernel(x_ref, g_ref, w_ref, o_ref):
    h = _rms(x_ref[...], g_ref[...])
    o_ref[...] = jnp.dot(h, w_ref[...], preferred_element_type=F32, precision=HIGHEST)


def _s_inproj(x, g, w):
    tn = 1024
    return pl.pallas_call(
        _s_inproj_kernel, grid=(D_IN // tn,),
        in_specs=[pl.BlockSpec((DEC_BATCH, D_MODEL), lambda j: (0, 0)),
                  pl.BlockSpec((1, D_MODEL), lambda j: (0, 0)),
                  pl.BlockSpec((D_MODEL, tn), lambda j: (0, j))],
        out_specs=pl.BlockSpec((DEC_BATCH, tn), lambda j: (0, j)),
        out_shape=jax.ShapeDtypeStruct((DEC_BATCH, D_IN), F32),
        compiler_params=_cparams(("arbitrary",)), name="s_inproj",
    )(x, g, w)


def _s_ssm_kernel(u_ref, sr_ref, si_ref, lr_ref, li_ref, bm_ref, cm_ref, d_ref, z_ref, nr_ref, ni_ref):
    u = u_ref[...]
    bu = jnp.dot(u, bm_ref[0], preferred_element_type=F32, precision=HIGHEST)
    lr, li = lr_ref[...], li_ref[...]
    xr, xi = sr_ref[...], si_ref[...]
    nr = lr * xr - li * xi + bu[:, :CH_ST]
    ni = lr * xi + li * xr + bu[:, CH_ST:]
    nr_ref[...] = nr
    ni_ref[...] = ni
    y = jnp.dot(jnp.concatenate([nr, ni], axis=1), cm_ref[0], preferred_element_type=F32, precision=HIGHEST)
    z_ref[...] = _gelu_tanh(y + d_ref[...] * u)


def _s_ssm(u, sr, si, lr1, li1, bm, cm, dskip):
    st = jax.ShapeDtypeStruct((DEC_BATCH, N_GROUPS * SSM_STATE), F32)
    return pl.pallas_call(
        _s_ssm_kernel, grid=(N_CHUNK,),
        in_specs=[pl.BlockSpec((DEC_BATCH, CH_IN), lambda j: (0, j)),
                  pl.BlockSpec((DEC_BATCH, CH_ST), lambda j: (0, j)),
                  pl.BlockSpec((DEC_BATCH, CH_ST), lambda j: (0, j)),
                  pl.BlockSpec((1, CH_ST), lambda j: (0, j)),
                  pl.BlockSpec((1, CH_ST), lambda j: (0, j)),
                  pl.BlockSpec((1, CH_IN, 2 * CH_ST), lambda j: (j, 0, 0)),
                  pl.BlockSpec((1, 2 * CH_ST, CH_IN), lambda j: (j, 0, 0)),
                  pl.BlockSpec((1, CH_IN), lambda j: (0, j))],
        out_specs=[pl.BlockSpec((DEC_BATCH, CH_IN), lambda j: (0, j)),
                   pl.BlockSpec((DEC_BATCH, CH_ST), lambda j: (0, j)),
                   pl.BlockSpec((DEC_BATCH, CH_ST), lambda j: (0, j))],
        out_shape=(jax.ShapeDtypeStruct((DEC_BATCH, SSM_WIDTH), F32), st, st),
        compiler_params=_cparams(("arbitrary",)), name="s_ssm",
    )(u, sr, si, lr1, li1, bm, cm, dskip)


def _s_glu_kernel(z_ref, w_ref, g_ref, o_ref):
    z = z_ref[...]
    zz = z * _sigmoid(jnp.dot(z, w_ref[...], preferred_element_type=F32, precision=HIGHEST))
    o_ref[...] = _rms(zz, g_ref[...])


def _s_glu(z, w, g):
    return pl.pallas_call(
        _s_glu_kernel, out_shape=jax.ShapeDtypeStruct((DEC_BATCH, SSM_WIDTH), F32),
        compiler_params=pltpu.CompilerParams(vmem_limit_bytes=VMEM_LIMIT), name="s_glu",
    )(z, w, g)


def _s_attn_kernel(q_ref, k_ref, v_ref, inv_ref, k1_ref, v1_ref, k4_ref, v4_ref, k16_ref, v16_ref,
                   o_ref, kr_ref):
    scale = HEAD_DIM ** -0.5
    pos = jnp.full((N_HEADS, 1), float(PAST_LEN), F32)
    cos, sin = _rope_tables(pos, inv_ref[...])
    q = q_ref[...] * cos + pltpu.roll(q_ref[...], HEAD_DIM // 2, 1) * sin
    kn = k_ref[...] * cos + pltpu.roll(k_ref[...], HEAD_DIM // 2, 1) * sin
    vn = v_ref[...]
    kr_ref[...] = kn
    s0 = jnp.sum(q * kn, axis=-1, keepdims=True) * scale
    outs, lses = [], []
    for kc_ref, vc_ref in ((k1_ref, v1_ref), (k4_ref, v4_ref), (k16_ref, v16_ref)):
        s = jnp.sum(kc_ref[...] * q[None], axis=-1, keepdims=True) * scale
        m = jnp.maximum(jnp.max(s, axis=0), s0)
        p = jnp.exp(s - m[None])
        p0 = jnp.exp(s0 - m)
        l = jnp.sum(p, axis=0) + p0
        outs.append((jnp.sum(p * vc_ref[...], axis=0) + p0 * vn) / l)
        lses.append(m + jnp.log(l))
    mm = jnp.maximum(jnp.maximum(lses[0], lses[1]), lses[2])
    es = [jnp.exp(x - mm) for x in lses]
    o_ref[...] = (es[0] * outs[0] + es[1] * outs[1] + es[2] * outs[2]) / (es[0] + es[1] + es[2])


def _s_attention(q, k, v, inv, cache_k, cache_v):
    hd = (N_HEADS, HEAD_DIM)
    tok = pl.BlockSpec((None,) + hd, lambda b: (b, 0, 0))
    args, specs = [], []
    for _, d in BRANCHES:
        nrow = CACHE_LEN // d
        last = nrow // BAND - 1
        spec = pl.BlockSpec((None, BAND, None) + hd, lambda b, last=last: (b, last, 0, 0, 0))
        for c in (cache_k, cache_v):
            args.append(c.reshape(DEC_BATCH, nrow, d, *hd))
            specs.append(spec)
    out = jax.ShapeDtypeStruct((DEC_BATCH,) + hd, F32)
    return pl.pallas_call(
        _s_attn_kernel, grid=(DEC_BATCH,),
        in_specs=[tok, tok, tok, pl.BlockSpec((1, HEAD_DIM), lambda b: (0, 0))] + specs,
        out_specs=[tok, tok], out_shape=(out, out),
        compiler_params=_cparams(("arbitrary",)), name="s_attn",
    )(q, k, v, inv, *args)


def _s_outproj_kernel(xa_ref, lga_ref, x_ref, ms_ref, o_ref, ga_ref, w_ref, gf_ref, wr_ref, br_ref,
                      xm_ref, lg_ref):
    del xa_ref, lga_ref
    mix = jnp.concatenate([ms_ref[...], _rms(o_ref[...], ga_ref[...])], axis=1)
    xm = x_ref[...] + jnp.dot(mix, w_ref[...], preferred_element_type=F32, precision=HIGHEST)
    h = _rms(xm, gf_ref[...])
    lg = jnp.dot(h, wr_ref[...], preferred_element_type=F32, precision=HIGHEST) + br_ref[...]
    pad = TM - DEC_BATCH
    xm_ref[...] = jnp.concatenate([xm, jnp.zeros((pad, D_MODEL), F32)], axis=0)
    lg_ref[...] = jnp.concatenate([lg, jnp.zeros((pad, 128), F32)], axis=0)


def _s_outproj(xm_all, lg_all, x, mix_ssm, o_attn, g_attn, w, g_ffn, wr, br):
    last = lambda i: (N_TOK_BLOCKS - 1, 0)
    full = lambda shape: pl.BlockSpec(shape, lambda i: (0, 0))
    any_spec = pl.BlockSpec(memory_space=pl.ANY)
    return pl.pallas_call(
        _s_outproj_kernel, grid=(1,),
        in_specs=[any_spec, any_spec,
                  full((DEC_BATCH, D_MODEL)), full((DEC_BATCH, SSM_WIDTH)), full((DEC_BATCH, ATTN_WIDTH)),
                  full((1, ATTN_WIDTH)), full((D_MODEL, D_MODEL)), full((1, D_MODEL)),
                  full((D_MODEL, 128)), full((1, 128))],
        out_specs=[pl.BlockSpec((TM, D_MODEL), last), pl.BlockSpec((TM, 128), last)],
        out_shape=(jax.ShapeDtypeStruct(xm_all.shape, F32), jax.ShapeDtypeStruct(lg_all.shape, F32)),
        input_output_aliases={0: 0, 1: 1},
        compiler_params=_cparams(("arbitrary",)), name="s_outproj",
    )(xm_all, lg_all, x, mix_ssm, o_attn, g_attn, w, g_ffn, wr, br)


def _route_kernel(lg_ref, ri_ref, rw_ref, cnt_ref, carry):
    i = pl.program_id(0)

    @pl.when(i == 0)
    def _():
        carry[...] = jnp.zeros((1, 128), F32)

    x = lg_ref[...]
    lane = lax.broadcasted_iota(I32, (TM, 128), 1)
    row = lax.broadcasted_iota(I32, (TM, 128), 0) + i * TM
    valid = row < T_REAL
    big = jnp.int32(1 << 20)
    gmask = lane < N_EXPERT_GROUPS
    lgm = jnp.where(gmask, x, NEG)
    m = jnp.max(lgm, axis=-1, keepdims=True)
    gate1 = 1.0 / jnp.sum(jnp.where(gmask, jnp.exp(lgm - m), 0.0), axis=-1, keepdims=True)
    grp = jnp.min(jnp.where(gmask & (lgm == m), lane, big), axis=-1, keepdims=True)
    lo = N_EXPERT_GROUPS + EXPERTS_PER_GROUP * grp
    emask = (lane >= lo) & (lane < lo + EXPERTS_PER_GROUP)
    le = jnp.where(emask, x, NEG)
    t1 = jnp.max(le, axis=-1, keepdims=True)
    i1 = jnp.min(jnp.where(emask & (le == t1), lane, big), axis=-1, keepdims=True)
    emask2 = emask & (lane != i1)
    le2 = jnp.where(emask2, x, NEG)
    t2 = jnp.max(le2, axis=-1, keepdims=True)
    i2 = jnp.min(jnp.where(emask2 & (le2 == t2), lane, big), axis=-1, keepdims=True)
    e21 = jnp.exp(t2 - t1)
    w1 = gate1 / (1.0 + e21)
    w2 = gate1 * e21 / (1.0 + e21)
    eid1, eid2 = i1 - N_EXPERT_GROUPS, i2 - N_EXPERT_GROUPS
    oh1 = jnp.where(valid & (lane == eid1), 1.0, 0.0)
    oh2 = jnp.where(valid & (lane == eid2), 1.0, 0.0)
    a = oh1 + oh2
    rr = lax.broadcasted_iota(I32, (TM, TM), 0)
    cc = lax.broadcasted_iota(I32, (TM, TM), 1)
    before = jnp.where(cc < rr, 1.0, 0.0).astype(BF16)
    pre = jnp.dot(before, a.astype(BF16), preferred_element_type=F32) + carry[...]
    rank1 = jnp.sum(oh1 * pre, axis=-1, keepdims=True).astype(I32)
    rank2 = jnp.sum(oh2 * pre, axis=-1, keepdims=True).astype(I32)
    carry[...] = carry[...] + jnp.sum(a, axis=0, keepdims=True)
    cnt_ref[...] = jnp.broadcast_to(carry[...], (8, 128))
    zi = jnp.zeros((TM, 128), I32)
    ri = jnp.where(lane == 0, eid1, jnp.where(lane == 1, eid2, jnp.where(lane == 2, rank1, jnp.where(lane == 3, rank2, zi))))
    ri_ref[...] = jnp.where(valid, ri, zi)
    rw = jnp.where(lane == 0, w1, jnp.where(lane == 1, w2, 0.0))
    rw_ref[...] = jnp.where(valid, rw, 0.0)


def _route(logits):
    blk = pl.BlockSpec((TM, 128), lambda i: (i, 0))
    return pl.pallas_call(
        _route_kernel, grid=(N_TOK_BLOCKS,), in_specs=[blk],
        out_specs=[blk, blk, pl.BlockSpec((8, 128), lambda i: (0, 0))],
        out_shape=(jax.ShapeDtypeStruct((T_PAD, 128), I32), jax.ShapeDtypeStruct((T_PAD, 128), F32),
                   jax.ShapeDtypeStruct((8, 128), F32)),
        scratch_shapes=[pltpu.VMEM((1, 128), F32)],
        compiler_params=_cparams(("arbitrary",)), name="route",
    )(logits)


def _row_copy(src, s, dst, d, sem):
    return pltpu.make_async_copy(src.at[pl.ds(s, 1)], dst.at[pl.ds(d, 1)], sem)


def _invert_kernel(poff_ref, ri_ref, dest_ref, rt_ref):
    i = pl.program_id(0)

    @pl.when(i == 0)
    def _():
        def clear(p, c):
            rt_ref[p] = 0
            return c
        lax.fori_loop(0, N_MOE_ROWS, clear, 0, unroll=32)

    def put(r, c):
        t = i * TM + r
        for s in range(2):
            d = poff_ref[ri_ref[0, 0, 4 * r + s]] + ri_ref[0, 0, 4 * r + 2 + s]
            dest_ref[0, 0, 2 * r + s] = d
            rt_ref[d] = t
        return c

    def blank(r, c):
        dest_ref[0, 0, 2 * r] = 0
        dest_ref[0, 0, 2 * r + 1] = 0
        return c

    @pl.when(i < T_PROMPT // TM)
    def _():
        lax.fori_loop(0, TM, put, 0, unroll=8)

    @pl.when(i == T_PROMPT // TM)
    def _():
        lax.fori_loop(0, DEC_BATCH, put, 0, unroll=8)
        lax.fori_loop(DEC_BATCH, TM, blank, 0, unroll=8)


def _invert(poff, ri4):
    grid_spec = pltpu.PrefetchScalarGridSpec(
        num_scalar_prefetch=1, grid=(N_TOK_BLOCKS,),
        in_specs=[pl.BlockSpec((1, 1, 4 * TM), lambda i, po: (i, 0, 0), memory_space=pltpu.SMEM)],
        out_specs=[pl.BlockSpec((1, 1, 2 * TM), lambda i, po: (i, 0, 0), memory_space=pltpu.SMEM),
                   pl.BlockSpec(memory_space=pltpu.SMEM)])
    return pl.pallas_call(
        _invert_kernel, grid_spec=grid_spec,
        out_shape=(jax.ShapeDtypeStruct((N_TOK_BLOCKS, 1, 2 * TM), I32),
                   jax.ShapeDtypeStruct((N_MOE_ROWS,), I32)),
        compiler_params=_cparams(("arbitrary",)), name="moe_invert",
    )(poff, ri4)


def _expert_kernel(fb_ref, nv_ref, rt_ref, xm_ref, gf_ref, wg_ref, wu_ref, wd_ref, y_ref,
                   xbuf, ybuf, gsem, ysem, wgu_scr, wd_scr):
    e = pl.program_id(0)
    first, end, total = fb_ref[e], fb_ref[e + 1], fb_ref[N_EXPERTS]

    def y_copy(g, slot):
        return pltpu.make_async_copy(ybuf.at[slot], y_ref.at[pl.ds(g * MOE_BLK, MOE_BLK)], ysem.at[slot])

    def gather(g, slot, wait):
        def body(r, c):
            if wait:
                _row_copy(xm_ref, 0, xbuf.at[slot], 0, gsem.at[slot]).wait()
            else:
                _row_copy(xm_ref, rt_ref[g * MOE_BLK + r], xbuf.at[slot], r,
                          gsem.at[slot]).start(priority=GATHER_PRIORITY)
            return c
        lax.fori_loop(0, nv_ref[g], body, 0)

    @pl.when(e == 0)
    def _():
        xbuf[...] = jnp.zeros(xbuf.shape, F32)
        for d in range(GATHER_DEPTH - 1):
            gather(d, d, wait=False)

    @pl.when(end > first)
    def _():
        wgu_scr[:, 0:EXPERT_FF] = wg_ref[0].astype(BF16)
        wgu_scr[:, EXPERT_FF:] = wu_ref[0].astype(BF16)
        wd_scr[...] = wd_ref[0].astype(BF16)

    def block(g, carry):
        slot = g % 2
        xslot = g % GATHER_DEPTH

        gather(g + GATHER_DEPTH - 1, (g + GATHER_DEPTH - 1) % GATHER_DEPTH, wait=False)
        gather(g, xslot, wait=True)
        h = _rms(xbuf[xslot], gf_ref[...]).astype(BF16)
        gu = jnp.dot(h, wgu_scr[...], preferred_element_type=F32)
        gate, up = gu[:, :EXPERT_FF], gu[:, EXPERT_FF:]
        act = (gate * _sigmoid(gate) * up).astype(BF16)
        y = jnp.dot(act, wd_scr[...], preferred_element_type=F32)

        @pl.when(g >= 2)
        def _():
            y_copy(g - 2, slot).wait()
        ybuf[slot] = y
        y_copy(g, slot).start()
        return carry

    lax.fori_loop(first, end, block, 0)

    @pl.when(e == N_EXPERTS - 1)
    def _():
        @pl.when(total >= 2)
        def _():
            y_copy(total - 2, total % 2).wait()

        @pl.when(total >= 1)
        def _():
            y_copy(total - 1, (total - 1) % 2).wait()

        ybuf[0] = jnp.zeros((MOE_BLK, D_MODEL), F32)

        def zero_start(g, c):
            y_copy(g, 0).start()
            return c

        def zero_wait(g, c):
            y_copy(g, 0).wait()
            return c

        lax.fori_loop(total, N_MOE_BLOCKS, zero_start, 0)
        lax.fori_loop(total, N_MOE_BLOCKS, zero_wait, 0)


def _experts(first_blk, nvalid, row_tok, xm_all, g_ffn, w_g, w_u, w_d):
    grid_spec = pltpu.PrefetchScalarGridSpec(
        num_scalar_prefetch=3, grid=(N_EXPERTS,),
        in_specs=[pl.BlockSpec(memory_space=pl.ANY),
                  pl.BlockSpec((1, D_MODEL), lambda e, fb, nv, rt: (0, 0)),
                  pl.BlockSpec((1, D_MODEL, EXPERT_FF), lambda e, fb, nv, rt: (e, 0, 0)),
                  pl.BlockSpec((1, D_MODEL, EXPERT_FF), lambda e, fb, nv, rt: (e, 0, 0)),
                  pl.BlockSpec((1, EXPERT_FF, D_MODEL), lambda e, fb, nv, rt: (e, 0, 0))],
        out_specs=pl.BlockSpec(memory_space=pl.ANY),
        scratch_shapes=[pltpu.VMEM((GATHER_DEPTH, MOE_BLK, D_MODEL), F32), pltpu.VMEM((2, MOE_BLK, D_MODEL), F32),
                        pltpu.SemaphoreType.DMA((GATHER_DEPTH,)), pltpu.SemaphoreType.DMA((2,)),
                        pltpu.VMEM((D_MODEL, 2 * EXPERT_FF), BF16), pltpu.VMEM((EXPERT_FF, D_MODEL), BF16)])
    return pl.pallas_call(
        _expert_kernel, grid_spec=grid_spec,
        out_shape=jax.ShapeDtypeStruct((N_MOE_ROWS, D_MODEL), F32),
        compiler_params=_cparams(("arbitrary",)), name="moe_experts",
    )(first_blk, nvalid, row_tok, xm_all, g_ffn, w_g, w_u, w_d)


def _combine_kernel(dest_ref, destn_ref, rw_ref, xm_ref, g_ref, y_ref, o_ref, ybuf, sems, *, rows, nblk):
    i = pl.program_id(0)
    slot = i % 2

    def gather(d_ref, s, wait):
        def body(r, c):
            for j in range(2):
                cp = _row_copy(y_ref, 0 if wait else d_ref[0, 0, 2 * r + j], ybuf.at[s, j], r, sems.at[s])
                cp.wait() if wait else cp.start(priority=j)
            return c
        lax.fori_loop(0, rows, body, 0, unroll=8)

    @pl.when(i == 0)
    def _():
        gather(dest_ref, 0, wait=False)

    @pl.when(i + 1 < nblk)
    def _():
        gather(destn_ref, 1 - slot, wait=False)

    gather(dest_ref, slot, wait=True)
    w = rw_ref[...]
    x = xm_ref[...] + w[:, 0:1] * ybuf[slot, 0] + w[:, 1:2] * ybuf[slot, 1]
    o_ref[...] = _rms(x, g_ref[...]).reshape(o_ref.shape)


def _combine(dest3, rw, xm, g, y_rows, rows, nblk, blk0, out_spec, out_shape):
    smem_blk = lambda f: pl.BlockSpec((1, 1, 2 * rows), f, memory_space=pltpu.SMEM)
    return pl.pallas_call(
        functools.partial(_combine_kernel, rows=rows, nblk=nblk), grid=(nblk,),
        in_specs=[smem_blk(lambda i: (i, 0, 0)),
                  smem_blk(lambda i: (jnp.minimum(i + 1, nblk - 1), 0, 0)),
                  pl.BlockSpec((rows, 128), lambda i: (blk0 + i, 0)),
                  pl.BlockSpec((rows, D_MODEL), lambda i: (blk0 + i, 0)),
                  pl.BlockSpec((1, D_MODEL), lambda i: (0, 0)),
                  pl.BlockSpec(memory_space=pl.ANY)],
        out_specs=out_spec, out_shape=out_shape,
        scratch_shapes=[pltpu.VMEM((2, 2, rows, D_MODEL), F32), pltpu.SemaphoreType.DMA((2,))],
        compiler_params=_cparams(("arbitrary",)), name="moe_combine",
    )(dest3, dest3, rw, xm, g, y_rows)


def kernel(x_prompt, x_sample, cache_k, cache_v, state_ssm_re, state_ssm_im, g_mix_norm, w_in, ssm_a_re, ssm_a_im, ssm_log_dt, ssm_b_re, ssm_b_im, ssm_c_re, ssm_c_im, ssm_d, w_glu, g_ssm_out, g_attn_out, w_out, g_ffn_norm, w_router_group, b_router_group, w_router_expert, b_router_expert, w_expert_gate, w_expert_up, w_expert_down, g_final):
    row = lambda a: a.reshape(1, -1)
    half = HEAD_DIM // 2
    inv = ROPE_THETA ** (-jnp.arange(half, dtype=F32) / half)
    inv = jnp.concatenate([inv, inv]).reshape(1, HEAD_DIM)

    lam_r, lam_i, lams_r, lams_i, bb_r, bb_i = _ssm_params(ssm_a_re[0], ssm_a_im[0], ssm_log_dt[0], ssm_b_re[0], ssm_b_im[0])
    bm, bm16, cm, cm16 = _blockdiag(bb_r, bb_i, ssm_c_re[0], ssm_c_im[0])
    dskip = row(ssm_d[0])
    bcast8 = lambda a: jnp.broadcast_to(a.reshape(1, -1), (N_SEG, N_GROUPS * SSM_STATE))
    lr8, li8, lsr8, lsi8 = bcast8(lam_r), bcast8(lam_i), bcast8(lams_r), bcast8(lams_i)

    wr = jnp.concatenate([w_router_group[0], w_router_expert[0].reshape(D_MODEL, N_EXPERTS)], axis=1)
    wr = jnp.pad(wr, ((0, 0), (0, 128 - wr.shape[1])))
    br = jnp.pad(jnp.concatenate([b_router_group[0], b_router_expert[0].reshape(-1)]), (0, 128 - 36)).reshape(1, 128)

    x4 = x_prompt.reshape(BATCH, N_SEG, SEG_LEN, D_MODEL)
    u_il, q4, k4, v4, k_new, v_new = _inproj(x4, row(g_mix_norm[0]), inv, w_in[0])
    ends = _ssm_scan(u_il, bm16, lr8, li8, final=False)
    z_il, xend = _ssm_scan(u_il, bm16, lr8, li8, final=True, e=ends, lsr8=lsr8, lsi8=lsi8, cm=cm16, dskip=dskip)
    flat = lambda a: a.reshape(BATCH, SEQ, ATTN_WIDTH)
    o_attn = _attention(flat(q4), flat(k4), flat(v4))
    xm_all, lg_all = _outproj(
        x4, z_il, o_attn.reshape(BATCH, N_SEG, SEG_LEN, ATTN_WIDTH), w_glu[0], row(g_ssm_out[0]),
        row(g_attn_out[0]), w_out[0], row(g_ffn_norm[0]), wr.astype(BF16), br)

    xs = x_sample.reshape(DEC_BATCH, D_MODEL)
    proj = _s_inproj(xs, row(g_mix_norm[0]), w_in[0])
    us = proj[:, :SSM_WIDTH]
    qkv = proj[:, SSM_WIDTH:].reshape(DEC_BATCH, 3, N_HEADS, HEAD_DIM)
    zs, ns_r, ns_i = _s_ssm(us, state_ssm_re[0].reshape(DEC_BATCH, -1), state_ssm_im[0].reshape(DEC_BATCH, -1),
                            row(lam_r), row(lam_i), bm, cm, dskip)
    mix_ssm_s = _s_glu(zs, w_glu[0], row(g_ssm_out[0]))
    o_s, k_rot = _s_attention(qkv[:, 0], qkv[:, 1], qkv[:, 2], inv, cache_k[0], cache_v[0])
    xm_all, lg_all = _s_outproj(
        xm_all, lg_all, xs, mix_ssm_s, o_s.reshape(DEC_BATCH, ATTN_WIDTH), row(g_attn_out[0]),
        w_out[0], row(g_ffn_norm[0]), wr, br)

    ri, rw, cnt = _route(lg_all)
    counts = cnt[0, :N_EXPERTS].astype(I32)
    padded = (counts + MOE_BLK - 1) // MOE_BLK * MOE_BLK
    pend = jnp.cumsum(padded)
    poff = pend - padded
    first_blk = jnp.concatenate([poff, pend[-1:]]) // MOE_BLK
    blk_start = jnp.arange(N_MOE_BLOCKS + GATHER_DEPTH, dtype=I32) * MOE_BLK
    block_e = jnp.minimum(jnp.sum(pend[None, :] <= blk_start[:, None], axis=1), N_EXPERTS - 1)
    nvalid = jnp.clip(counts[block_e] - (blk_start - poff[block_e]), 0, MOE_BLK).astype(I32)
    dest3, row_tok = _invert(poff, ri[:, 0:4].reshape(N_TOK_BLOCKS, 1, 4 * TM))
    y_rows = _experts(first_blk, nvalid, row_tok, xm_all, row(g_ffn_norm[0]),
                      w_expert_gate[0], w_expert_up[0], w_expert_down[0])
    gfin = row(g_final)
    nkb = SEG_LEN // TK
    y_prompt = _combine(dest3, rw, xm_all, gfin, y_rows, TM, T_PROMPT // TM, 0,
                        pl.BlockSpec((1, N_SEG, TK, D_MODEL), lambda i: (i // nkb, 0, i % nkb, 0)),
                        jax.ShapeDtypeStruct((BATCH, N_SEG, SEG_LEN, D_MODEL), F32))
    dest_s = dest3[N_TOK_BLOCKS - 1:, :, :2 * DEC_BATCH]
    y_sample = _combine(dest_s, rw, xm_all, gfin, y_rows, DEC_BATCH, 1, T_PROMPT // DEC_BATCH,
                        pl.BlockSpec((DEC_BATCH, D_MODEL), lambda i: (0, 0)),
                        jax.ShapeDtypeStruct((DEC_BATCH, D_MODEL), F32))

    kv_shape = (1, BATCH, CACHE_LEN, N_HEADS, HEAD_DIM)
    st_p = lambda a: a[:, :, N_SEG - 1, :].reshape(1, BATCH, N_GROUPS, SSM_STATE)
    st_s = lambda a: a.reshape(1, DEC_BATCH, N_GROUPS, SSM_STATE)
    kvs = lambda a: a.reshape(1, DEC_BATCH, 1, N_HEADS, HEAD_DIM)
    return (y_prompt.reshape(BATCH, SEQ, D_MODEL), y_sample.reshape(DEC_BATCH, 1, D_MODEL),
            k_new.reshape(kv_shape), v_new.reshape(kv_shape),
            st_p(xend[..., :CH_ST]), st_p(xend[..., CH_ST:]),
            kvs(k_rot), kvs(qkv[:, 2]), st_s(ns_r), st_s(ns_i))
```

```python
import functools
import math

import jax
import jax.numpy as jnp
from jax import lax
from jax.experimental import pallas as pl
from jax.experimental.pallas import tpu as pltpu

F32 = jnp.float32
BF16 = jnp.bfloat16
I32 = jnp.int32
U32 = jnp.uint32
HIGHEST = lax.Precision.HIGHEST

D_MODEL = 2048
BATCH = 2
SEQ = 4096
DEC_BATCH = 32
PAST_LEN = 8192
CACHE_LEN = 2048
SSM_WIDTH = 1024
ATTN_WIDTH = 1024
SSM_GROUP = 16
N_GROUPS = 64
SSM_STATE = 64
HEAD_DIM = 128
N_HEADS = 8
D_IN = 4096
BRANCHES = ((128, 1), (512, 4), (2048, 16))
BAND = 128
ROPE_THETA = 10000.0
N_EXPERT_GROUPS = 4
EXPERTS_PER_GROUP = 8
N_EXPERTS = 32
EXPERT_FF = 512
NORM_EPS = 1e-6

N_SEG = 8
SEG_LEN = SEQ // N_SEG
N_CHUNK = 4
CH_IN = SSM_WIDTH // N_CHUNK
CH_ST = N_GROUPS * SSM_STATE // N_CHUNK
SSM_TK = 128
TM = 256
TK = TM // N_SEG
TK_SHIFT = TK.bit_length() - 1
LANE = 128
T_PROMPT = BATCH * SEQ
T_REAL = T_PROMPT + DEC_BATCH
T_PAD = T_PROMPT + TM
N_TOK_BLOCKS = T_PAD // TM
ROUTE_ROWS = 3 * TM
MOE_BLK = 256
N_MOE_BLOCKS = -(-(2 * T_REAL) // MOE_BLK) + N_EXPERTS
N_MOE_ROWS = N_MOE_BLOCKS * MOE_BLK
VMEM_LIMIT = 56 * 1024 * 1024
NEG = -1e30
LOG2_E = 1.4426950408889634
CAST_COLS = 512
GATHER_PRIORITY = 1
GATHER_DEPTH = 2


def _cparams(sem, vmem=VMEM_LIMIT):
    return pltpu.CompilerParams(dimension_semantics=sem, vmem_limit_bytes=vmem)


def _rms(x, g):
    return x * lax.rsqrt(jnp.mean(x * x, axis=-1, keepdims=True) + NORM_EPS) * g


def _gelu_tanh(y):
    return 0.5 * y * (1.0 + jnp.tanh(0.7978845608028654 * (y + 0.044715 * (y * y * y))))


def _sigmoid(x):
    return 1.0 / (1.0 + jnp.exp(-x))


def _rope_tables(pos, inv):
    ang = pos * inv
    lane = lax.broadcasted_iota(I32, ang.shape, 1)
    return jnp.cos(ang), jnp.where(lane < HEAD_DIM // 2, -jnp.sin(ang), jnp.sin(ang))


def _rope_heads(x, cos, sin):
    outs = []
    for h in range(x.shape[1] // HEAD_DIM):
        xh = x[:, h * HEAD_DIM:(h + 1) * HEAD_DIM]
        outs.append(xh * cos + pltpu.roll(xh, HEAD_DIM // 2, 1) * sin)
    return jnp.concatenate(outs, axis=1)


def _ssm_param_kernel(are_ref, aim_ref, ldt_ref, btr_ref, bti_ref,
                      lr_ref, li_ref, lsr_ref, lsi_ref, bbr_ref, bbi_ref):
    ar, ai = are_ref[...], aim_ref[...]
    dt = jnp.exp(ldt_ref[...])
    er, ei = ar * dt, ai * dt
    mag = jnp.exp(er)
    lr, li = mag * jnp.cos(ei), mag * jnp.sin(ei)
    lr_ref[...] = lr
    li_ref[...] = li
    mag_s = jnp.exp(er * SEG_LEN)
    lsr_ref[...] = mag_s * jnp.cos(ei * SEG_LEN)
    lsi_ref[...] = mag_s * jnp.sin(ei * SEG_LEN)
    xr, xi = lr - 1.0, li
    den = ar * ar + ai * ai
    cr = (xr * ar + xi * ai) / den
    ci = (xi * ar - xr * ai) / den
    btr, bti = btr_ref[...], bti_ref[...]
    bbr_ref[...] = cr[:, None, :] * btr - ci[:, None, :] * bti
    bbi_ref[...] = cr[:, None, :] * bti + ci[:, None, :] * btr


def _ssm_params(a_re, a_im, log_dt, b_re, b_im):
    g, n, p = N_GROUPS, SSM_STATE, SSM_GROUP
    gn = jax.ShapeDtypeStruct((g, n), F32)
    gpn = jax.ShapeDtypeStruct((g, p, n), F32)
    return pl.pallas_call(
        _ssm_param_kernel, out_shape=(gn, gn, gn, gn, gpn, gpn), name="ssm_params",
    )(a_re, a_im, log_dt.reshape(g, 1), b_re.transpose(0, 2, 1), b_im.transpose(0, 2, 1))


def _blockdiag_kernel(br_ref, bi_ref, cr_ref, ci_ref, bm_ref, bm16_ref, cm_ref, cm16_ref):
    bm_ref[...] = jnp.zeros(bm_ref.shape, F32)
    cm_ref[...] = jnp.zeros(cm_ref.shape, F32)
    gp, gl = 2 * SSM_GROUP, LANE
    keep_b = ((lax.broadcasted_iota(I32, (gp, gl), 0) // SSM_GROUP)
              == (lax.broadcasted_iota(I32, (gp, gl), 1) // SSM_STATE))
    for q in range(N_GROUPS // 2):
        j, p = divmod(q, 8)
        rows, cols = slice(gp * p, gp * (p + 1)), slice(gl * p, gl * (p + 1))
        bm_ref[j, rows, cols] = jnp.where(keep_b, br_ref[q], 0.0)
        bm_ref[j, rows, CH_ST + gl * p:CH_ST + gl * (p + 1)] = jnp.where(keep_b, bi_ref[q], 0.0)
    oct_rows = 8 * SSM_STATE
    keep_c = ((lax.broadcasted_iota(I32, (oct_rows, gl), 0) // SSM_STATE)
              == (lax.broadcasted_iota(I32, (oct_rows, gl), 1) // SSM_GROUP))
    for o in range(N_GROUPS // 8):
        j, half = divmod(o, 2)
        rows, cols = slice(oct_rows * half, oct_rows * (half + 1)), slice(gl * half, gl * (half + 1))
        cm_ref[j, rows, cols] = jnp.where(keep_c, cr_ref[o], 0.0)
        cm_ref[j, CH_ST + oct_rows * half:CH_ST + oct_rows * (half + 1), cols] = jnp.where(keep_c, -ci_ref[o], 0.0)
    bm16_ref[...] = bm_ref[...].astype(BF16)
    cm16_ref[...] = cm_ref[...].astype(BF16)


def _blockdiag(bb_r, bb_i, c_re, c_im):
    pair = lambda m: jnp.tile(m, (1, 1, 2)).reshape(N_GROUPS // 2, 2 * SSM_GROUP, LANE)
    octet = lambda c: jnp.tile(c.transpose(0, 2, 1), (1, 1, 8)).reshape(N_GROUPS // 8, 8 * SSM_STATE, LANE)
    bshape, cshape = (N_CHUNK, CH_IN, 2 * CH_ST), (N_CHUNK, 2 * CH_ST, CH_IN)
    return pl.pallas_call(
        _blockdiag_kernel,
        out_shape=(jax.ShapeDtypeStruct(bshape, F32), jax.ShapeDtypeStruct(bshape, BF16),
                   jax.ShapeDtypeStruct(cshape, F32), jax.ShapeDtypeStruct(cshape, BF16)),
        compiler_params=pltpu.CompilerParams(vmem_limit_bytes=VMEM_LIMIT), name="ssm_blockdiag",
    )(pair(bb_r), pair(bb_i), octet(c_re), octet(c_im))


def _load_weight_bf16(w_hbm, w16, stage, sem):
    n = w_hbm.shape[1] // CAST_COLS

    def chunk(c):
        return pltpu.make_async_copy(w_hbm.at[:, pl.ds(c * CAST_COLS, CAST_COLS)], stage.at[c % 2], sem.at[c % 2])

    chunk(0).start()
    for c in range(n):
        if c + 1 < n:
            chunk(c + 1).start()
        chunk(c).wait()
        w16[:, c * CAST_COLS:(c + 1) * CAST_COLS] = stage[c % 2].astype(BF16)


def _inproj_kernel(x_ref, g_ref, inv_ref, w_hbm, u_ref, q_ref, k_ref, v_ref, kn_ref, vn_ref, w_ref, stage, sem):
    kb = pl.program_id(1)

    @pl.when((pl.program_id(0) == 0) & (kb == 0))
    def _():
        _load_weight_bf16(w_hbm, w_ref, stage, sem)

    h = _rms(x_ref[0].reshape(TM, D_MODEL), g_ref[...]).astype(BF16)
    rid = lax.broadcasted_iota(I32, (TM, 1), 0)
    pos = ((rid >> TK_SHIFT) * SEG_LEN + kb * TK + (rid & (TK - 1))).astype(F32)
    cos, sin = _rope_tables(pos, inv_ref[...])
    w = SSM_WIDTH
    u = jnp.dot(h, w_ref[:, 0:w], preferred_element_type=F32)
    for s in range(N_SEG):
        for c in range(w // LANE):
            u_ref[0, c, pl.ds(s, TK, stride=N_SEG), :] = u[s * TK:(s + 1) * TK, c * LANE:(c + 1) * LANE]
    q_ref[0] = _rope_heads(jnp.dot(h, w_ref[:, w:2 * w], preferred_element_type=F32), cos, sin).reshape(N_SEG, TK, w)
    k = _rope_heads(jnp.dot(h, w_ref[:, 2 * w:3 * w], preferred_element_type=F32), cos, sin)
    v = jnp.dot(h, w_ref[:, 3 * w:4 * w], preferred_element_type=F32)
    k_ref[0] = k.reshape(N_SEG, TK, w)
    v_ref[0] = v.reshape(N_SEG, TK, w)
    kn_ref[0] = k[TM // 2:].reshape(N_SEG // 2, TK, w)
    vn_ref[0] = v[TM // 2:].reshape(N_SEG // 2, TK, w)


def _inproj(x4, g, inv, w):
    const = lambda b, k: (0, 0)
    blk4 = lambda nseg, width: pl.BlockSpec((1, nseg, TK, width), lambda b, k: (b, 0, k, 0))
    qkv = jax.ShapeDtypeStruct((BATCH, N_SEG, SEG_LEN, ATTN_WIDTH), F32)
    kvn = jax.ShapeDtypeStruct((BATCH, N_SEG // 2, SEG_LEN, ATTN_WIDTH), F32)
    return pl.pallas_call(
        _inproj_kernel, grid=(BATCH, SEG_LEN // TK),
        in_specs=[blk4(N_SEG, D_MODEL), pl.BlockSpec((1, D_MODEL), const),
                  pl.BlockSpec((1, HEAD_DIM), const), pl.BlockSpec(memory_space=pl.ANY)],
        out_specs=[pl.BlockSpec((1, SSM_WIDTH // LANE, TM, LANE), lambda b, k: (b, 0, k, 0)),
                   blk4(N_SEG, ATTN_WIDTH), blk4(N_SEG, ATTN_WIDTH), blk4(N_SEG, ATTN_WIDTH),
                   blk4(N_SEG // 2, ATTN_WIDTH), blk4(N_SEG // 2, ATTN_WIDTH)],
        out_shape=(jax.ShapeDtypeStruct((BATCH, SSM_WIDTH // LANE, SEQ, LANE), F32), qkv, qkv, qkv, kvn, kvn),
        scratch_shapes=[pltpu.VMEM((D_MODEL, D_IN), BF16), pltpu.VMEM((2, D_MODEL, CAST_COLS), F32),
                        pltpu.SemaphoreType.DMA((2,))],
        compiler_params=_cparams(("arbitrary",) * 2), name="inproj",
    )(x4, g, inv, w)


def _ssm_scan_kernel(*refs, final):
    if final:
        (u_ref, bm_ref, lr_ref, li_ref, e_ref, lsr_ref, lsi_ref, cm_ref, d_ref,
         z_ref, xend_ref, bu_scr, xr_scr, xi_scr) = refs
    else:
        u_ref, bm_ref, lr_ref, li_ref, xend_ref, bu_scr, xr_scr, xi_scr = refs
    kb = pl.program_id(2)

    @pl.when(kb == 0)
    def _():
        if final:
            lsr, lsi = lsr_ref[0:1, :], lsi_ref[0:1, :]
            seg_id = lax.broadcasted_iota(I32, (N_SEG, CH_ST), 0)
            pr = pi = jnp.zeros((1, CH_ST), F32)
            xr0 = xi0 = jnp.zeros((N_SEG, CH_ST), F32)
            for s in range(1, N_SEG):
                er = e_ref[0, 0, s - 1:s, 0:CH_ST]
                ei = e_ref[0, 0, s - 1:s, CH_ST:2 * CH_ST]
                pr, pi = er + lsr * pr - lsi * pi, ei + lsr * pi + lsi * pr
                xr0 = jnp.where(seg_id == s, pr, xr0)
                xi0 = jnp.where(seg_id == s, pi, xi0)
            xr_scr[...] = xr0
            xi_scr[...] = xi0
        else:
            xr_scr[...] = jnp.zeros((N_SEG, CH_ST), F32)
            xi_scr[...] = jnp.zeros((N_SEG, CH_ST), F32)

    u = jnp.concatenate([u_ref[0, c] for c in range(CH_IN // LANE)], axis=1)
    bu_scr[...] = jnp.dot(u.astype(BF16), bm_ref[0], preferred_element_type=F32)
    lr, li = lr_ref[...], li_ref[...]

    def step(k, carry):
        xr, xi = carry
        r0 = pl.multiple_of(k * N_SEG, N_SEG)
        bur = bu_scr[pl.ds(r0, N_SEG), 0:CH_ST]
        bui = bu_scr[pl.ds(r0, N_SEG), CH_ST:2 * CH_ST]
        nr = lr * xr - li * xi + bur
        ni = lr * xi + li * xr + bui
        if final:
            bu_scr[pl.ds(r0, N_SEG), 0:CH_ST] = nr
            bu_scr[pl.ds(r0, N_SEG), CH_ST:2 * CH_ST] = ni
        return nr, ni

    xr, xi = lax.fori_loop(0, SSM_TK, step, (xr_scr[...], xi_scr[...]), unroll=4)
    xr_scr[...] = xr
    xi_scr[...] = xi
    if final:
        y = jnp.dot(bu_scr[...].astype(BF16), cm_ref[0], preferred_element_type=F32)
        z = _gelu_tanh(y + d_ref[...] * u)
        for c in range(CH_IN // LANE):
            z_ref[0, c] = z[:, c * LANE:(c + 1) * LANE]

    @pl.when(kb == pl.num_programs(2) - 1)
    def _():
        xend_ref[0, 0] = jnp.concatenate([xr, xi], axis=1)


def _ssm_scan(u_perm, bm, lr8, li8, final, e=None, lsr8=None, lsi8=None, cm=None, dskip=None):
    rows = SSM_TK * N_SEG
    nkb = SEG_LEN // SSM_TK
    in_specs = [pl.BlockSpec((1, CH_IN // LANE, rows, LANE), lambda b, j, k: (b, j, k, 0)),
                pl.BlockSpec((1, CH_IN, 2 * CH_ST), lambda b, j, k: (j, 0, 0)),
                pl.BlockSpec((N_SEG, CH_ST), lambda b, j, k: (0, j)),
                pl.BlockSpec((N_SEG, CH_ST), lambda b, j, k: (0, j))]
    args = [u_perm, bm, lr8, li8]
    xend_spec = pl.BlockSpec((1, 1, N_SEG, 2 * CH_ST), lambda b, j, k: (b, j, 0, 0))
    xend_shape = jax.ShapeDtypeStruct((BATCH, N_CHUNK, N_SEG, 2 * CH_ST), F32)
    if final:
        in_specs += [xend_spec,
                     pl.BlockSpec((N_SEG, CH_ST), lambda b, j, k: (0, j)),
                     pl.BlockSpec((N_SEG, CH_ST), lambda b, j, k: (0, j)),
                     pl.BlockSpec((1, 2 * CH_ST, CH_IN), lambda b, j, k: (j, 0, 0)),
                     pl.BlockSpec((1, CH_IN), lambda b, j, k: (0, j))]
        args += [e, lsr8, lsi8, cm, dskip]
        out_specs = [pl.BlockSpec((1, CH_IN // LANE, rows, LANE), lambda b, j, k: (b, j, k, 0)), xend_spec]
        out_shape = (jax.ShapeDtypeStruct((BATCH, SSM_WIDTH // LANE, SEQ, LANE), F32), xend_shape)
    else:
        out_specs = xend_spec
        out_shape = xend_shape
    return pl.pallas_call(
        functools.partial(_ssm_scan_kernel, final=final), grid=(BATCH, N_CHUNK, nkb),
        in_specs=in_specs, out_specs=out_specs, out_shape=out_shape,
        scratch_shapes=[pltpu.VMEM((rows, 2 * CH_ST), F32),
                        pltpu.VMEM((N_SEG, CH_ST), F32), pltpu.VMEM((N_SEG, CH_ST), F32)],
        compiler_params=_cparams(("arbitrary",) * 3),
        name="ssm_scan_final" if final else "ssm_scan_ends",
    )(*args)


def _attn_kernel(q_ref, k_ref, v_ref, o_ref, o_scr, lse_scr, bias_scr):
    scale2 = HEAD_DIM ** -0.5 * LOG2_E
    qi = lax.broadcasted_iota(I32, (BAND, 2 * BAND), 0)
    kj = lax.broadcasted_iota(I32, (BAND, 2 * BAND), 1)
    cur_ok = (kj >= BAND) & (kj - BAND <= qi)
    prev_ok = (kj < BAND) & (kj >= qi)
    bias_scr[0] = jnp.where(cur_ok, 0.0, NEG)
    bias_scr[1] = jnp.where(cur_ok | prev_ok, 0.0, NEG)
    nt = (((1,), (1,)), ((), ()))
    for bi, (_, d) in enumerate(BRANCHES):
        nblk = SEQ // (BAND * d)
        shift = nblk.bit_length() - 1

        def tile(i, c, bi=bi, d=d, nblk=nblk, shift=shift):
            r = i >> shift
            ib = i & (nblk - 1)
            start = r + ib * (BAND * d)
            pstart = jnp.maximum(start - BAND * d, r)
            cur = pl.ds(start, BAND, stride=d)
            prev = pl.ds(pstart, BAND, stride=d)
            q = (q_ref[0, cur, :] * scale2).astype(BF16)
            k2 = jnp.concatenate([k_ref[0, prev, :], k_ref[0, cur, :]], axis=0).astype(BF16)
            v2 = jnp.concatenate([v_ref[0, prev, :], v_ref[0, cur, :]], axis=0).astype(BF16)
            s = lax.dot_general(q, k2, nt, preferred_element_type=F32) + bias_scr[jnp.minimum(ib, 1)]
            m = jnp.max(s, axis=-1, keepdims=True)
            p = jnp.exp2(s - m)
            l = jnp.sum(p, axis=-1, keepdims=True)
            o = jnp.dot(p.astype(BF16), v2, preferred_element_type=F32) * (1.0 / l)
            o_scr[bi, cur, :] = o
            lse_scr[bi, cur, :] = jnp.broadcast_to(m + jnp.log2(l), (BAND, HEAD_DIM))
            return c

        def softmax_pv(s, v, rows_out, bi=bi):
            m = jnp.max(s, axis=-1, keepdims=True)
            p = jnp.exp2(s - m)
            l = jnp.sum(p, axis=-1, keepdims=True)
            o_scr[bi, rows_out, :] = jnp.dot(p.astype(BF16), v, preferred_element_type=F32) * (1.0 / l)
            lse_scr[bi, rows_out, :] = jnp.broadcast_to(m + jnp.log2(l), (BAND, HEAD_DIM))

        def residue(r, c, d=d):
            sub = pl.ds(r, 2 * BAND, stride=d)
            q = (q_ref[0, sub, :] * scale2).astype(BF16)
            k = k_ref[0, sub, :].astype(BF16)
            v = v_ref[0, sub, :].astype(BF16)
            s0 = lax.dot_general(q[:BAND], k[:BAND], nt, preferred_element_type=F32) + bias_scr[1, :, BAND:]
            softmax_pv(s0, v[:BAND], pl.ds(r, BAND, stride=d))
            s1 = lax.dot_general(q[BAND:], k, nt, preferred_element_type=F32) + bias_scr[1]
            softmax_pv(s1, v, pl.ds(r + BAND * d, BAND, stride=d))
            return c

        if nblk == 2:
            lax.fori_loop(0, d, residue, 0, unroll=16)
        else:
            lax.fori_loop(0, SEQ // BAND, tile, 0, unroll=32)

    rows = 512

    def merge(i, c):
        sl = pl.ds(pl.multiple_of(i * rows, rows), rows)
        l0, l1, l2 = lse_scr[0, sl, :], lse_scr[1, sl, :], lse_scr[2, sl, :]
        m = jnp.maximum(jnp.maximum(l0, l1), l2)
        e0, e1, e2 = jnp.exp2(l0 - m), jnp.exp2(l1 - m), jnp.exp2(l2 - m)
        o_ref[0, sl, :] = ((e0 * o_scr[0, sl, :] + e1 * o_scr[1, sl, :] + e2 * o_scr[2, sl, :])
                           * (1.0 / (e0 + e1 + e2)))
        return c

    lax.fori_loop(0, SEQ // rows, merge, 0)


def _attention(q, k, v):
    spec = pl.BlockSpec((1, SEQ, HEAD_DIM), lambda b, h: (b, 0, h))
    return pl.pallas_call(
        _attn_kernel, grid=(BATCH, N_HEADS), in_specs=[spec, spec, spec], out_specs=spec,
        out_shape=jax.ShapeDtypeStruct((BATCH, SEQ, ATTN_WIDTH), F32),
        scratch_shapes=[pltpu.VMEM((3, SEQ, HEAD_DIM), F32), pltpu.VMEM((3, SEQ, HEAD_DIM), F32),
                        pltpu.VMEM((2, BAND, 2 * BAND), F32)],
        compiler_params=_cparams(("arbitrary",) * 2), name="dilated_attn",
    )(q, k, v)


def _outproj_kernel(x_ref, z_ref, o_ref, wglu_hbm, gs_ref, ga_ref, w_hbm, gf_ref, wr_ref, br_ref, xm_ref, lg_ref,
                    wglu_ref, w_ref, stage, sem):
    is_prompt = pl.program_id(0) < T_PROMPT // TM

    @pl.when(pl.program_id(0) == 0)
    def _():
        _load_weight_bf16(wglu_hbm, wglu_ref, stage.at[:, 0:SSM_WIDTH], sem)
        _load_weight_bf16(w_hbm, w_ref, stage, sem)

    @pl.when(is_prompt)
    def _():
        z = jnp.concatenate(
            [jnp.concatenate([z_ref[0, c, pl.ds(s, TK, stride=N_SEG), :] for c in range(SSM_WIDTH // LANE)], axis=1)
             for s in range(N_SEG)], axis=0)
        zz = z * _sigmoid(jnp.dot(z.astype(BF16), wglu_ref[...], preferred_element_type=F32))
        ms = _rms(zz, gs_ref[...]).astype(BF16)
        ma = _rms(o_ref[0].reshape(TM, ATTN_WIDTH), ga_ref[...]).astype(BF16)
        xm = (x_ref[0].reshape(TM, D_MODEL) + jnp.dot(ms, w_ref[0:SSM_WIDTH, :], preferred_element_type=F32)
              + jnp.dot(ma, w_ref[SSM_WIDTH:, :], preferred_element_type=F32))
        xm_ref[...] = xm
        h = _rms(xm, gf_ref[...]).astype(BF16)
        lg_ref[...] = jnp.dot(h, wr_ref[...], preferred_element_type=F32) + br_ref[...]

    @pl.when(jnp.logical_not(is_prompt))
    def _():
        xm_ref[...] = jnp.zeros((TM, D_MODEL), F32)
        lg_ref[...] = jnp.zeros((TM, 128), F32)


def _outproj(x4, z_slab, o_attn4, wglu, g_ssm, g_attn, w, g_ffn, wr_bf16, br):
    nkb = SEG_LEN // TK

    def split(i):
        i = jnp.minimum(i, T_PROMPT // TM - 1)
        return i // nkb, i % nkb

    def blk4(i):
        b, k = split(i)
        return (b, 0, k, 0)

    row = lambda i: (i, 0)
    const = lambda i: (0, 0)
    return pl.pallas_call(
        _outproj_kernel, grid=(N_TOK_BLOCKS,),
        in_specs=[pl.BlockSpec((1, N_SEG, TK, D_MODEL), blk4),
                  pl.BlockSpec((1, SSM_WIDTH // LANE, TM, LANE), blk4),
                  pl.BlockSpec((1, N_SEG, TK, ATTN_WIDTH), blk4),
                  pl.BlockSpec(memory_space=pl.ANY),
                  pl.BlockSpec((1, SSM_WIDTH), const),
                  pl.BlockSpec((1, ATTN_WIDTH), const),
                  pl.BlockSpec(memory_space=pl.ANY),
                  pl.BlockSpec((1, D_MODEL), const),
                  pl.BlockSpec((D_MODEL, 128), const),
                  pl.BlockSpec((1, 128), const)],
        out_specs=[pl.BlockSpec((TM, D_MODEL), row), pl.BlockSpec((TM, 128), row)],
        out_shape=(jax.ShapeDtypeStruct((T_PAD, D_MODEL), F32), jax.ShapeDtypeStruct((T_PAD, 128), F32)),
        scratch_shapes=[pltpu.VMEM((SSM_WIDTH, SSM_WIDTH), BF16), pltpu.VMEM((D_MODEL, D_MODEL), BF16),
                        pltpu.VMEM((2, D_MODEL, CAST_COLS), F32), pltpu.SemaphoreType.DMA((2,))],
        compiler_params=_cparams(("arbitrary",)), name="outproj",
    )(x4, z_slab, o_attn4, wglu, g_ssm, g_attn, w, g_ffn, wr_bf16, br)


def _s_inproj_kernel(x_ref, g_ref, w_ref, o_ref):
    h = _rms(x_ref[...], g_ref[...])
    o_ref[...] = jnp.dot(h, w_ref[...], preferred_element_type=F32, precision=HIGHEST)


def _s_inproj(x, g, w):
    tn = 1024
    return pl.pallas_call(
        _s_inproj_kernel, grid=(D_IN // tn,),
        in_specs=[pl.BlockSpec((DEC_BATCH, D_MODEL), lambda j: (0, 0)),
                  pl.BlockSpec((1, D_MODEL), lambda j: (0, 0)),
                  pl.BlockSpec((D_MODEL, tn), lambda j: (0, j))],
        out_specs=pl.BlockSpec((DEC_BATCH, tn), lambda j: (0, j)),
        out_shape=jax.ShapeDtypeStruct((DEC_BATCH, D_IN), F32),
        compiler_params=_cparams(("arbitrary",)), name="s_inproj",
    )(x, g, w)


def _s_ssm_kernel(u_ref, sr_ref, si_ref, lr_ref, li_ref, bm_ref, cm_ref, d_ref, z_ref, nr_ref, ni_ref):
    u = u_ref[...]
    bu = jnp.dot(u, bm_ref[0], preferred_element_type=F32, precision=HIGHEST)
    lr, li = lr_ref[...], li_ref[...]
    xr, xi = sr_ref[...], si_ref[...]
    nr = lr * xr - li * xi + bu[:, :CH_ST]
    ni = lr * xi + li * xr + bu[:, CH_ST:]
    nr_ref[...] = nr
    ni_ref[...] = ni
    y = jnp.dot(jnp.concatenate([nr, ni], axis=1), cm_ref[0], preferred_element_type=F32, precision=HIGHEST)
    z_ref[...] = _gelu_tanh(y + d_ref[...] * u)


def _s_ssm(u, sr, si, lr1, li1, bm, cm, dskip):
    st = jax.ShapeDtypeStruct((DEC_BATCH, N_GROUPS * SSM_STATE), F32)
    return pl.pallas_call(
        _s_ssm_kernel, grid=(N_CHUNK,),
        in_specs=[pl.BlockSpec((DEC_BATCH, CH_IN), lambda j: (0, j)),
                  pl.BlockSpec((DEC_BATCH, CH_ST), lambda j: (0, j)),
                  pl.BlockSpec((DEC_BATCH, CH_ST), lambda j: (0, j)),
                  pl.BlockSpec((1, CH_ST), lambda j: (0, j)),
                  pl.BlockSpec((1, CH_ST), lambda j: (0, j)),
                  pl.BlockSpec((1, CH_IN, 2 * CH_ST), lambda j: (j, 0, 0)),
                  pl.BlockSpec((1, 2 * CH_ST, CH_IN), lambda j: (j, 0, 0)),
                  pl.BlockSpec((1, CH_IN), lambda j: (0, j))],
        out_specs=[pl.BlockSpec((DEC_BATCH, CH_IN), lambda j: (0, j)),
                   pl.BlockSpec((DEC_BATCH, CH_ST), lambda j: (0, j)),
                   pl.BlockSpec((DEC_BATCH, CH_ST), lambda j: (0, j))],
        out_shape=(jax.ShapeDtypeStruct((DEC_BATCH, SSM_WIDTH), F32), st, st),
        compiler_params=_cparams(("arbitrary",)), name="s_ssm",
    )(u, sr, si, lr1, li1, bm, cm, dskip)


def _s_glu_kernel(z_ref, w_ref, g_ref, o_ref):
    z = z_ref[...]
    zz = z * _sigmoid(jnp.dot(z, w_ref[...], preferred_element_type=F32, precision=HIGHEST))
    o_ref[...] = _rms(zz, g_ref[...])


def _s_glu(z, w, g):
    return pl.pallas_call(
        _s_glu_kernel, out_shape=jax.ShapeDtypeStruct((DEC_BATCH, SSM_WIDTH), F32),
        compiler_params=pltpu.CompilerParams(vmem_limit_bytes=VMEM_LIMIT), name="s_glu",
    )(z, w, g)


def _s_attn_kernel(q_ref, k_ref, v_ref, inv_ref, k1_ref, v1_ref, k4_ref, v4_ref, k16_ref, v16_ref,
                   o_ref, kr_ref):
    scale = HEAD_DIM ** -0.5
    pos = jnp.full((N_HEADS, 1), float(PAST_LEN), F32)
    cos, sin = _rope_tables(pos, inv_ref[...])
    q = q_ref[...] * cos + pltpu.roll(q_ref[...], HEAD_DIM // 2, 1) * sin
    kn = k_ref[...] * cos + pltpu.roll(k_ref[...], HEAD_DIM // 2, 1) * sin
    vn = v_ref[...]
    kr_ref[...] = kn
    s0 = jnp.sum(q * kn, axis=-1, keepdims=True) * scale
    outs, lses = [], []
    for kc_ref, vc_ref in ((k1_ref, v1_ref), (k4_ref, v4_ref), (k16_ref, v16_ref)):
        s = jnp.sum(kc_ref[...] * q[None], axis=-1, keepdims=True) * scale
        m = jnp.maximum(jnp.max(s, axis=0), s0)
        p = jnp.exp(s - m[None])
        p0 = jnp.exp(s0 - m)
        l = jnp.sum(p, axis=0) + p0
        outs.append((jnp.sum(p * vc_ref[...], axis=0) + p0 * vn) / l)
        lses.append(m + jnp.log(l))
    mm = jnp.maximum(jnp.maximum(lses[0], lses[1]), lses[2])
    es = [jnp.exp(x - mm) for x in lses]
    o_ref[...] = (es[0] * outs[0] + es[1] * outs[1] + es[2] * outs[2]) / (es[0] + es[1] + es[2])


def _s_attention(q, k, v, inv, cache_k, cache_v):
    hd = (N_HEADS, HEAD_DIM)
    tok = pl.BlockSpec((None,) + hd, lambda b: (b, 0, 0))
    args, specs = [], []
    for _, d in BRANCHES:
        nrow = CACHE_LEN // d
        last = nrow // BAND - 1
        spec = pl.BlockSpec((None, BAND, None) + hd, lambda b, last=last: (b, last, 0, 0, 0))
        for c in (cache_k, cache_v):
            args.append(c.reshape(DEC_BATCH, nrow, d, *hd))
            specs.append(spec)
    out = jax.ShapeDtypeStruct((DEC_BATCH,) + hd, F32)
    return pl.pallas_call(
        _s_attn_kernel, grid=(DEC_BATCH,),
        in_specs=[tok, tok, tok, pl.BlockSpec((1, HEAD_DIM), lambda b: (0, 0))] + specs,
        out_specs=[tok, tok], out_shape=(out, out),
        compiler_params=_cparams(("arbitrary",)), name="s_attn",
    )(q, k, v, inv, *args)


def _s_outproj_kernel(xa_ref, lga_ref, x_ref, ms_ref, o_ref, ga_ref, w_ref, gf_ref, wr_ref, br_ref,
                      xm_ref, lg_ref):
    del xa_ref, lga_ref
    mix = jnp.concatenate([ms_ref[...], _rms(o_ref[...], ga_ref[...])], axis=1)
    xm = x_ref[...] + jnp.dot(mix, w_ref[...], preferred_element_type=F32, precision=HIGHEST)
    h = _rms(xm, gf_ref[...])
    lg = jnp.dot(h, wr_ref[...], preferred_element_type=F32, precision=HIGHEST) + br_ref[...]
    pad = TM - DEC_BATCH
    xm_ref[...] = jnp.concatenate([xm, jnp.zeros((pad, D_MODEL), F32)], axis=0)
    lg_ref[...] = jnp.concatenate([lg, jnp.zeros((pad, 128), F32)], axis=0)


def _s_outproj(xm_all, lg_all, x, mix_ssm, o_attn, g_attn, w, g_ffn, wr, br):
    last = lambda i: (N_TOK_BLOCKS - 1, 0)
    full = lambda shape: pl.BlockSpec(shape, lambda i: (0, 0))
    any_spec = pl.BlockSpec(memory_space=pl.ANY)
    return pl.pallas_call(
        _s_outproj_kernel, grid=(1,),
        in_specs=[any_spec, any_spec,
                  full((DEC_BATCH, D_MODEL)), full((DEC_BATCH, SSM_WIDTH)), full((DEC_BATCH, ATTN_WIDTH)),
                  full((1, ATTN_WIDTH)), full((D_MODEL, D_MODEL)), full((1, D_MODEL)),
                  full((D_MODEL, 128)), full((1, 128))],
        out_specs=[pl.BlockSpec((TM, D_MODEL), last), pl.BlockSpec((TM, 128), last)],
        out_shape=(jax.ShapeDtypeStruct(xm_all.shape, F32), jax.ShapeDtypeStruct(lg_all.shape, F32)),
        input_output_aliases={0: 0, 1: 1},
        compiler_params=_cparams(("arbitrary",)), name="s_outproj",
    )(xm_all, lg_all, x, mix_ssm, o_attn, g_attn, w, g_ffn, wr, br)


def _route_kernel(lg_ref, ri_ref, rw_ref, cnt_ref, carry):
    i = pl.program_id(0)

    @pl.when(i == 0)
    def _():
        carry[...] = jnp.zeros((1, 128), F32)

    x = lg_ref[...]
    lane = lax.broadcasted_iota(I32, (ROUTE_ROWS, 128), 1)
    row = lax.broadcasted_iota(I32, (ROUTE_ROWS, 128), 0) + i * ROUTE_ROWS
    valid = row < T_REAL
    big = jnp.int32(1 << 20)
    gmask = lane < N_EXPERT_GROUPS
    lgm = jnp.where(gmask, x, NEG)
    m = jnp.max(lgm, axis=-1, keepdims=True)
    gate1 = 1.0 / jnp.sum(jnp.where(gmask, jnp.exp(lgm - m), 0.0), axis=-1, keepdims=True)
    grp = jnp.min(jnp.where(gmask & (lgm == m), lane, big), axis=-1, keepdims=True)
    lo = N_EXPERT_GROUPS + EXPERTS_PER_GROUP * grp
    emask = (lane >= lo) & (lane < lo + EXPERTS_PER_GROUP)
    le = jnp.where(emask, x, NEG)
    t1 = jnp.max(le, axis=-1, keepdims=True)
    i1 = jnp.min(jnp.where(emask & (le == t1), lane, big), axis=-1, keepdims=True)
    emask2 = emask & (lane != i1)
    le2 = jnp.where(emask2, x, NEG)
    t2 = jnp.max(le2, axis=-1, keepdims=True)
    i2 = jnp.min(jnp.where(emask2 & (le2 == t2), lane, big), axis=-1, keepdims=True)
    e21 = jnp.exp(t2 - t1)
    w1 = gate1 / (1.0 + e21)
    w2 = gate1 * e21 / (1.0 + e21)
    eid1, eid2 = i1 - N_EXPERT_GROUPS, i2 - N_EXPERT_GROUPS
    oh1 = jnp.where(valid & (lane == eid1), 1.0, 0.0)
    oh2 = jnp.where(valid & (lane == eid2), 1.0, 0.0)
    a = oh1 + oh2
    rr = lax.broadcasted_iota(I32, (ROUTE_ROWS, ROUTE_ROWS), 0)
    cc = lax.broadcasted_iota(I32, (ROUTE_ROWS, ROUTE_ROWS), 1)
    before = jnp.where(cc < rr, 1.0, 0.0).astype(BF16)
    pre = jnp.dot(before, a.astype(BF16), preferred_element_type=F32) + carry[...]
    rank1 = jnp.sum(oh1 * pre, axis=-1, keepdims=True).astype(I32)
    rank2 = jnp.sum(oh2 * pre, axis=-1, keepdims=True).astype(I32)
    carry[...] = carry[...] + jnp.sum(a, axis=0, keepdims=True)
    cnt_ref[...] = jnp.broadcast_to(carry[...], (8, 128))
    zi = jnp.zeros((ROUTE_ROWS, 128), I32)
    ri = jnp.where(lane == 0, eid1, jnp.where(lane == 1, eid2, jnp.where(lane == 2, rank1, jnp.where(lane == 3, rank2, zi))))
    ri_ref[...] = jnp.where(valid, ri, zi)
    rw = jnp.where(lane == 0, w1, jnp.where(lane == 1, w2, 0.0))
    rw_ref[...] = jnp.where(valid, rw, 0.0)


def _route(logits):
    blk = pl.BlockSpec((ROUTE_ROWS, 128), lambda i: (i, 0))
    return pl.pallas_call(
        _route_kernel, grid=(T_PAD // ROUTE_ROWS,), in_specs=[blk],
        out_specs=[blk, blk, pl.BlockSpec((8, 128), lambda i: (0, 0))],
        out_shape=(jax.ShapeDtypeStruct((T_PAD, 128), I32), jax.ShapeDtypeStruct((T_PAD, 128), F32),
                   jax.ShapeDtypeStruct((8, 128), F32)),
        scratch_shapes=[pltpu.VMEM((1, 128), F32)],
        compiler_params=_cparams(("arbitrary",)), name="route",
    )(logits)


def _row_copy(src, s, dst, d, sem):
    return pltpu.make_async_copy(src.at[pl.ds(s, 1)], dst.at[pl.ds(d, 1)], sem)


def _invert_kernel(poff_ref, ri_ref, rt_ref):
    i = pl.program_id(0)

    @pl.when(i == 0)
    def _():
        def clear(p, c):
            rt_ref[p] = 0
            return c
        lax.fori_loop(0, N_MOE_ROWS, clear, 0, unroll=32)

    def put(r, c):
        t = i * TM + r
        for s in range(2):
            rt_ref[poff_ref[ri_ref[0, 0, 4 * r + s]] + ri_ref[0, 0, 4 * r + 2 + s]] = t
        return c

    @pl.when(i < T_PROMPT // TM)
    def _():
        lax.fori_loop(0, TM, put, 0, unroll=8)

    @pl.when(i == T_PROMPT // TM)
    def _():
        lax.fori_loop(0, DEC_BATCH, put, 0, unroll=8)


def _invert(poff, ri4):
    grid_spec = pltpu.PrefetchScalarGridSpec(
        num_scalar_prefetch=1, grid=(N_TOK_BLOCKS,),
        in_specs=[pl.BlockSpec((1, 1, 4 * TM), lambda i, po: (i, 0, 0), memory_space=pltpu.SMEM)],
        out_specs=pl.BlockSpec(memory_space=pltpu.SMEM))
    return pl.pallas_call(
        _invert_kernel, grid_spec=grid_spec,
        out_shape=jax.ShapeDtypeStruct((N_MOE_ROWS,), I32),
        compiler_params=_cparams(("arbitrary",)), name="moe_invert",
    )(poff, ri4)


def _expert_kernel(fb_ref, nv_ref, rt_ref, xm_ref, gf_ref, wg_ref, wu_ref, wd_ref, y_ref,
                   xbuf, ybuf, gsem, ysem, wgu_scr, wd_scr):
    e = pl.program_id(0)
    first, end, total = fb_ref[e], fb_ref[e + 1], fb_ref[N_EXPERTS]

    def y_copy(g, slot):
        return pltpu.make_async_copy(ybuf.at[slot], y_ref.at[pl.ds(g * MOE_BLK, MOE_BLK)], ysem.at[slot])

    def gather(g, slot, wait):
        def body(r, c):
            if wait:
                _row_copy(xm_ref, 0, xbuf.at[slot], 0, gsem.at[slot]).wait()
            else:
                _row_copy(xm_ref, rt_ref[g * MOE_BLK + r], xbuf.at[slot], r,
                          gsem.at[slot]).start(priority=GATHER_PRIORITY)
            return c
        lax.fori_loop(0, nv_ref[g], body, 0)

    @pl.when(e == 0)
    def _():
        xbuf[...] = jnp.zeros(xbuf.shape, F32)
        for d in range(GATHER_DEPTH - 1):
            gather(d, d, wait=False)

    @pl.when(end > first)
    def _():
        wgu_scr[:, 0:EXPERT_FF] = wg_ref[0].astype(BF16)
        wgu_scr[:, EXPERT_FF:] = wu_ref[0].astype(BF16)
        wd_scr[...] = wd_ref[0].astype(BF16)

    def block(g, carry):
        slot = g % 2
        xslot = g % GATHER_DEPTH

        gather(g + GATHER_DEPTH - 1, (g + GATHER_DEPTH - 1) % GATHER_DEPTH, wait=False)
        gather(g, xslot, wait=True)
        h = _rms(xbuf[xslot], gf_ref[...]).astype(BF16)
        gu = jnp.dot(h, wgu_scr[...], preferred_element_type=F32)
        gate, up = gu[:, :EXPERT_FF], gu[:, EXPERT_FF:]
        act = (gate * _sigmoid(gate) * up).astype(BF16)
        y = jnp.dot(act, wd_scr[...], preferred_element_type=F32)

        @pl.when(g >= 2)
        def _():
            y_copy(g - 2, slot).wait()
        ybuf[slot] = y
        y_copy(g, slot).start()
        return carry

    lax.fori_loop(first, end, block, 0)

    @pl.when(e == N_EXPERTS - 1)
    def _():
        @pl.when(total >= 2)
        def _():
            y_copy(total - 2, total % 2).wait()

        @pl.when(total >= 1)
        def _():
            y_copy(total - 1, (total - 1) % 2).wait()

        ybuf[0] = jnp.zeros((MOE_BLK, D_MODEL), F32)

        def zero_start(g, c):
            y_copy(g, 0).start()
            return c

        def zero_wait(g, c):
            y_copy(g, 0).wait()
            return c

        lax.fori_loop(total, N_MOE_BLOCKS, zero_start, 0)
        lax.fori_loop(total, N_MOE_BLOCKS, zero_wait, 0)


def _experts(first_blk, nvalid, row_tok, xm_all, g_ffn, w_g, w_u, w_d):
    grid_spec = pltpu.PrefetchScalarGridSpec(
        num_scalar_prefetch=3, grid=(N_EXPERTS,),
        in_specs=[pl.BlockSpec(memory_space=pl.ANY),
                  pl.BlockSpec((1, D_MODEL), lambda e, fb, nv, rt: (0, 0)),
                  pl.BlockSpec((1, D_MODEL, EXPERT_FF), lambda e, fb, nv, rt: (e, 0, 0)),
                  pl.BlockSpec((1, D_MODEL, EXPERT_FF), lambda e, fb, nv, rt: (e, 0, 0)),
                  pl.BlockSpec((1, EXPERT_FF, D_MODEL), lambda e, fb, nv, rt: (e, 0, 0))],
        out_specs=pl.BlockSpec(memory_space=pl.ANY),
        scratch_shapes=[pltpu.VMEM((GATHER_DEPTH, MOE_BLK, D_MODEL), F32), pltpu.VMEM((2, MOE_BLK, D_MODEL), F32),
                        pltpu.SemaphoreType.DMA((GATHER_DEPTH,)), pltpu.SemaphoreType.DMA((2,)),
                        pltpu.VMEM((D_MODEL, 2 * EXPERT_FF), BF16), pltpu.VMEM((EXPERT_FF, D_MODEL), BF16)])
    return pl.pallas_call(
        _expert_kernel, grid_spec=grid_spec,
        out_shape=jax.ShapeDtypeStruct((N_MOE_ROWS, D_MODEL), F32),
        compiler_params=_cparams(("arbitrary",)), name="moe_experts",
    )(first_blk, nvalid, row_tok, xm_all, g_ffn, w_g, w_u, w_d)


def _combine_kernel(poff_ref, ri_ref, rin_ref, rw_ref, xm_ref, g_ref, y_ref, o_ref, ybuf, sems, *, rows, nblk):
    i = pl.program_id(0)
    slot = i % 2

    def gather(r_ref, s, wait):
        def body(r, c):
            for j in range(2):
                src = 0 if wait else poff_ref[r_ref[0, 0, 4 * r + j]] + r_ref[0, 0, 4 * r + 2 + j]
                cp = _row_copy(y_ref, src, ybuf.at[s, j], r, sems.at[s])
                cp.wait() if wait else cp.start(priority=j)
            return c
        lax.fori_loop(0, rows, body, 0, unroll=8)

    @pl.when(i == 0)
    def _():
        gather(ri_ref, 0, wait=False)

    @pl.when(i + 1 < nblk)
    def _():
        gather(rin_ref, 1 - slot, wait=False)

    gather(ri_ref, slot, wait=True)
    w = rw_ref[...]
    x = xm_ref[...] + w[:, 0:1] * ybuf[slot, 0] + w[:, 1:2] * ybuf[slot, 1]
    o_ref[...] = _rms(x, g_ref[...]).reshape(o_ref.shape)


def _combine(poff, ri4, rw, xm, g, y_rows, rows, nblk, blk0, out_spec, out_shape):
    smem_blk = lambda f: pl.BlockSpec((1, 1, 4 * rows), f, memory_space=pltpu.SMEM)
    grid_spec = pltpu.PrefetchScalarGridSpec(
        num_scalar_prefetch=1, grid=(nblk,),
        in_specs=[smem_blk(lambda i, po: (i, 0, 0)),
                  smem_blk(lambda i, po: (jnp.minimum(i + 1, nblk - 1), 0, 0)),
                  pl.BlockSpec((rows, 128), lambda i, po: (blk0 + i, 0)),
                  pl.BlockSpec((rows, D_MODEL), lambda i, po: (blk0 + i, 0)),
                  pl.BlockSpec((1, D_MODEL), lambda i, po: (0, 0)),
                  pl.BlockSpec(memory_space=pl.ANY)],
        out_specs=out_spec,
        scratch_shapes=[pltpu.VMEM((2, 2, rows, D_MODEL), F32), pltpu.SemaphoreType.DMA((2,))])
    return pl.pallas_call(
        functools.partial(_combine_kernel, rows=rows, nblk=nblk), grid_spec=grid_spec, out_shape=out_shape,
        compiler_params=_cparams(("arbitrary",)), name="moe_combine",
    )(poff, ri4, ri4, rw, xm, g, y_rows)


def kernel(x_prompt, x_sample, cache_k, cache_v, state_ssm_re, state_ssm_im, g_mix_norm, w_in, ssm_a_re, ssm_a_im, ssm_log_dt, ssm_b_re, ssm_b_im, ssm_c_re, ssm_c_im, ssm_d, w_glu, g_ssm_out, g_attn_out, w_out, g_ffn_norm, w_router_group, b_router_group, w_router_expert, b_router_expert, w_expert_gate, w_expert_up, w_expert_down, g_final):
    row = lambda a: a.reshape(1, -1)
    half = HEAD_DIM // 2
    inv = ROPE_THETA ** (-jnp.arange(half, dtype=F32) / half)
    inv = jnp.concatenate([inv, inv]).reshape(1, HEAD_DIM)

    lam_r, lam_i, lams_r, lams_i, bb_r, bb_i = _ssm_params(ssm_a_re[0], ssm_a_im[0], ssm_log_dt[0], ssm_b_re[0], ssm_b_im[0])
    bm, bm16, cm, cm16 = _blockdiag(bb_r, bb_i, ssm_c_re[0], ssm_c_im[0])
    dskip = row(ssm_d[0])
    bcast8 = lambda a: jnp.broadcast_to(a.reshape(1, -1), (N_SEG, N_GROUPS * SSM_STATE))
    lr8, li8, lsr8, lsi8 = bcast8(lam_r), bcast8(lam_i), bcast8(lams_r), bcast8(lams_i)

    wr = jnp.concatenate([w_router_group[0], w_router_expert[0].reshape(D_MODEL, N_EXPERTS)], axis=1)
    wr = jnp.pad(wr, ((0, 0), (0, 128 - wr.shape[1])))
    br = jnp.pad(jnp.concatenate([b_router_group[0], b_router_expert[0].reshape(-1)]), (0, 128 - 36)).reshape(1, 128)

    x4 = x_prompt.reshape(BATCH, N_SEG, SEG_LEN, D_MODEL)
    u_il, q4, k4, v4, k_new, v_new = _inproj(x4, row(g_mix_norm[0]), inv, w_in[0])
    ends = _ssm_scan(u_il, bm16, lr8, li8, final=False)
    z_il, xend = _ssm_scan(u_il, bm16, lr8, li8, final=True, e=ends, lsr8=lsr8, lsi8=lsi8, cm=cm16, dskip=dskip)
    flat = lambda a: a.reshape(BATCH, SEQ, ATTN_WIDTH)
    o_attn = _attention(flat(q4), flat(k4), flat(v4))
    xm_all, lg_all = _outproj(
        x4, z_il, o_attn.reshape(BATCH, N_SEG, SEG_LEN, ATTN_WIDTH), w_glu[0], row(g_ssm_out[0]),
        row(g_attn_out[0]), w_out[0], row(g_ffn_norm[0]), wr.astype(BF16), br)

    xs = x_sample.reshape(DEC_BATCH, D_MODEL)
    proj = _s_inproj(xs, row(g_mix_norm[0]), w_in[0])
    us = proj[:, :SSM_WIDTH]
    qkv = proj[:, SSM_WIDTH:].reshape(DEC_BATCH, 3, N_HEADS, HEAD_DIM)
    zs, ns_r, ns_i = _s_ssm(us, state_ssm_re[0].reshape(DEC_BATCH, -1), state_ssm_im[0].reshape(DEC_BATCH, -1),
                            row(lam_r), row(lam_i), bm, cm, dskip)
    mix_ssm_s = _s_glu(zs, w_glu[0], row(g_ssm_out[0]))
    o_s, k_rot = _s_attention(qkv[:, 0], qkv[:, 1], qkv[:, 2], inv, cache_k[0], cache_v[0])
    xm_all, lg_all = _s_outproj(
        xm_all, lg_all, xs, mix_ssm_s, o_s.reshape(DEC_BATCH, ATTN_WIDTH), row(g_attn_out[0]),
        w_out[0], row(g_ffn_norm[0]), wr, br)

    ri, rw, cnt = _route(lg_all)
    counts = cnt[0, :N_EXPERTS].astype(I32)
    padded = (counts + MOE_BLK - 1) // MOE_BLK * MOE_BLK
    pend = jnp.cumsum(padded)
    poff = pend - padded
    first_blk = jnp.concatenate([poff, pend[-1:]]) // MOE_BLK
    blk_start = jnp.arange(N_MOE_BLOCKS + GATHER_DEPTH, dtype=I32) * MOE_BLK
    block_e = jnp.minimum(jnp.sum(pend[None, :] <= blk_start[:, None], axis=1), N_EXPERTS - 1)
    nvalid = jnp.clip(counts[block_e] - (blk_start - poff[block_e]), 0, MOE_BLK).astype(I32)
    ri4 = ri[:, 0:4].reshape(N_TOK_BLOCKS, 1, 4 * TM)
    row_tok = _invert(poff, ri4)
    y_rows = _experts(first_blk, nvalid, row_tok, xm_all, row(g_ffn_norm[0]),
                      w_expert_gate[0], w_expert_up[0], w_expert_down[0])
    gfin = row(g_final)
    nkb = SEG_LEN // TK
    y_prompt = _combine(poff, ri4, rw, xm_all, gfin, y_rows, TM, T_PROMPT // TM, 0,
                        pl.BlockSpec((1, N_SEG, TK, D_MODEL), lambda i, po: (i // nkb, 0, i % nkb, 0)),
                        jax.ShapeDtypeStruct((BATCH, N_SEG, SEG_LEN, D_MODEL), F32))
    ri4_s = ri4[N_TOK_BLOCKS - 1:, :, :4 * DEC_BATCH]
    y_sample = _combine(poff, ri4_s, rw, xm_all, gfin, y_rows, DEC_BATCH, 1, T_PROMPT // DEC_BATCH,
                        pl.BlockSpec((DEC_BATCH, D_MODEL), lambda i, po: (0, 0)),
                        jax.ShapeDtypeStruct((DEC_BATCH, D_MODEL), F32))

    kv_shape = (1, BATCH, CACHE_LEN, N_HEADS, HEAD_DIM)
    st_p = lambda a: a[:, :, N_SEG - 1, :].reshape(1, BATCH, N_GROUPS, SSM_STATE)
    st_s = lambda a: a.reshape(1, DEC_BATCH, N_GROUPS, SSM_STATE)
    kvs = lambda a: a.reshape(1, DEC_BATCH, 1, N_HEADS, HEAD_DIM)
    return (y_prompt.reshape(BATCH, SEQ, D_MODEL), y_sample.reshape(DEC_BATCH, 1, D_MODEL),
            k_new.reshape(kv_shape), v_new.reshape(kv_shape),
            st_p(xend[..., :CH_ST]), st_p(xend[..., CH_ST:]),
            kvs(k_rot), kvs(qkv[:, 2]), st_s(ns_r), st_s(ns_i))
```

```python
import functools
import math

import jax
import jax.numpy as jnp
from jax import lax
from jax.experimental import pallas as pl
from jax.experimental.pallas import tpu as pltpu

F32 = jnp.float32
BF16 = jnp.bfloat16
I32 = jnp.int32
U32 = jnp.uint32
HIGHEST = lax.Precision.HIGHEST

D_MODEL = 2048
BATCH = 2
SEQ = 4096
DEC_BATCH = 32
PAST_LEN = 8192
CACHE_LEN = 2048
SSM_WIDTH = 1024
ATTN_WIDTH = 1024
SSM_GROUP = 16
N_GROUPS = 64
SSM_STATE = 64
HEAD_DIM = 128
N_HEADS = 8
D_IN = 4096
BRANCHES = ((128, 1), (512, 4), (2048, 16))
BAND = 128
ROPE_THETA = 10000.0
N_EXPERT_GROUPS = 4
EXPERTS_PER_GROUP = 8
N_EXPERTS = 32
EXPERT_FF = 512
NORM_EPS = 1e-6

N_SEG = 8
SEG_LEN = SEQ // N_SEG
N_CHUNK = 4
CH_IN = SSM_WIDTH // N_CHUNK
CH_ST = N_GROUPS * SSM_STATE // N_CHUNK
SSM_TK = 128
TM = 256
TK = TM // N_SEG
TK_SHIFT = TK.bit_length() - 1
LANE = 128
T_PROMPT = BATCH * SEQ
T_REAL = T_PROMPT + DEC_BATCH
T_PAD = T_PROMPT + TM
N_TOK_BLOCKS = T_PAD // TM
ROUTE_ROWS = 3 * TM
MOE_BLK = 256
N_MOE_BLOCKS = -(-(2 * T_REAL) // MOE_BLK) + N_EXPERTS
N_MOE_ROWS = N_MOE_BLOCKS * MOE_BLK
VMEM_LIMIT = 56 * 1024 * 1024
NEG = -1e30
LOG2_E = 1.4426950408889634
CAST_COLS = 512
GATHER_PRIORITY = 1
GATHER_DEPTH = 2


def _cparams(sem, vmem=VMEM_LIMIT):
    return pltpu.CompilerParams(dimension_semantics=sem, vmem_limit_bytes=vmem)


def _rms(x, g):
    return x * lax.rsqrt(jnp.mean(x * x, axis=-1, keepdims=True) + NORM_EPS) * g


def _gelu_tanh(y):
    return 0.5 * y * (1.0 + jnp.tanh(0.7978845608028654 * (y + 0.044715 * (y * y * y))))


def _sigmoid(x):
    return 1.0 / (1.0 + jnp.exp(-x))


def _rope_tables(pos, inv):
    ang = pos * inv
    lane = lax.broadcasted_iota(I32, ang.shape, 1)
    return jnp.cos(ang), jnp.where(lane < HEAD_DIM // 2, -jnp.sin(ang), jnp.sin(ang))


def _rope_heads(x, cos, sin):
    outs = []
    for h in range(x.shape[1] // HEAD_DIM):
        xh = x[:, h * HEAD_DIM:(h + 1) * HEAD_DIM]
        outs.append(xh * cos + pltpu.roll(xh, HEAD_DIM // 2, 1) * sin)
    return jnp.concatenate(outs, axis=1)


def _ssm_param_kernel(are_ref, aim_ref, ldt_ref, btr_ref, bti_ref,
                      lr_ref, li_ref, lsr_ref, lsi_ref, bbr_ref, bbi_ref):
    ar, ai = are_ref[...], aim_ref[...]
    dt = jnp.exp(ldt_ref[...])
    er, ei = ar * dt, ai * dt
    mag = jnp.exp(er)
    lr, li = mag * jnp.cos(ei), mag * jnp.sin(ei)
    lr_ref[...] = lr
    li_ref[...] = li
    mag_s = jnp.exp(er * SEG_LEN)
    lsr_ref[...] = mag_s * jnp.cos(ei * SEG_LEN)
    lsi_ref[...] = mag_s * jnp.sin(ei * SEG_LEN)
    xr, xi = lr - 1.0, li
    den = ar * ar + ai * ai
    cr = (xr * ar + xi * ai) / den
    ci = (xi * ar - xr * ai) / den
    btr, bti = btr_ref[...], bti_ref[...]
    bbr_ref[...] = cr[:, None, :] * btr - ci[:, None, :] * bti
    bbi_ref[...] = cr[:, None, :] * bti + ci[:, None, :] * btr


def _ssm_params(a_re, a_im, log_dt, b_re, b_im):
    g, n, p = N_GROUPS, SSM_STATE, SSM_GROUP
    gn = jax.ShapeDtypeStruct((g, n), F32)
    gpn = jax.ShapeDtypeStruct((g, p, n), F32)
    return pl.pallas_call(
        _ssm_param_kernel, out_shape=(gn, gn, gn, gn, gpn, gpn), name="ssm_params",
    )(a_re, a_im, log_dt.reshape(g, 1), b_re.transpose(0, 2, 1), b_im.transpose(0, 2, 1))


def _blockdiag_kernel(br_ref, bi_ref, cr_ref, ci_ref, bm_ref, bm16_ref, cm_ref, cm16_ref):
    bm_ref[...] = jnp.zeros(bm_ref.shape, F32)
    cm_ref[...] = jnp.zeros(cm_ref.shape, F32)
    gp, gl = 2 * SSM_GROUP, LANE
    keep_b = ((lax.broadcasted_iota(I32, (gp, gl), 0) // SSM_GROUP)
              == (lax.broadcasted_iota(I32, (gp, gl), 1) // SSM_STATE))
    for q in range(N_GROUPS // 2):
        j, p = divmod(q, 8)
        rows, cols = slice(gp * p, gp * (p + 1)), slice(gl * p, gl * (p + 1))
        bm_ref[j, rows, cols] = jnp.where(keep_b, br_ref[q], 0.0)
        bm_ref[j, rows, CH_ST + gl * p:CH_ST + gl * (p + 1)] = jnp.where(keep_b, bi_ref[q], 0.0)
    oct_rows = 8 * SSM_STATE
    keep_c = ((lax.broadcasted_iota(I32, (oct_rows, gl), 0) // SSM_STATE)
              == (lax.broadcasted_iota(I32, (oct_rows, gl), 1) // SSM_GROUP))
    for o in range(N_GROUPS // 8):
        j, half = divmod(o, 2)
        rows, cols = slice(oct_rows * half, oct_rows * (half + 1)), slice(gl * half, gl * (half + 1))
        cm_ref[j, rows, cols] = jnp.where(keep_c, cr_ref[o], 0.0)
        cm_ref[j, CH_ST + oct_rows * half:CH_ST + oct_rows * (half + 1), cols] = jnp.where(keep_c, -ci_ref[o], 0.0)
    bm16_ref[...] = bm_ref[...].astype(BF16)
    cm16_ref[...] = cm_ref[...].astype(BF16)


def _blockdiag(bb_r, bb_i, c_re, c_im):
    pair = lambda m: jnp.tile(m, (1, 1, 2)).reshape(N_GROUPS // 2, 2 * SSM_GROUP, LANE)
    octet = lambda c: jnp.tile(c.transpose(0, 2, 1), (1, 1, 8)).reshape(N_GROUPS // 8, 8 * SSM_STATE, LANE)
    bshape, cshape = (N_CHUNK, CH_IN, 2 * CH_ST), (N_CHUNK, 2 * CH_ST, CH_IN)
    return pl.pallas_call(
        _blockdiag_kernel,
        out_shape=(jax.ShapeDtypeStruct(bshape, F32), jax.ShapeDtypeStruct(bshape, BF16),
                   jax.ShapeDtypeStruct(cshape, F32), jax.ShapeDtypeStruct(cshape, BF16)),
        compiler_params=pltpu.CompilerParams(vmem_limit_bytes=VMEM_LIMIT), name="ssm_blockdiag",
    )(pair(bb_r), pair(bb_i), octet(c_re), octet(c_im))


def _load_weight_bf16(w_hbm, w16, stage, sem):
    n = w_hbm.shape[1] // CAST_COLS

    def chunk(c):
        return pltpu.make_async_copy(w_hbm.at[:, pl.ds(c * CAST_COLS, CAST_COLS)], stage.at[c % 2], sem.at[c % 2])

    chunk(0).start()
    for c in range(n):
        if c + 1 < n:
            chunk(c + 1).start()
        chunk(c).wait()
        w16[:, c * CAST_COLS:(c + 1) * CAST_COLS] = stage[c % 2].astype(BF16)


def _inproj_kernel(x_ref, g_ref, inv_ref, w_hbm, u_ref, q_ref, k_ref, v_ref, kn_ref, vn_ref, w_ref, stage, sem):
    kb = pl.program_id(1)

    @pl.when((pl.program_id(0) == 0) & (kb == 0))
    def _():
        _load_weight_bf16(w_hbm, w_ref, stage, sem)

    h = _rms(x_ref[0].reshape(TM, D_MODEL), g_ref[...]).astype(BF16)
    rid = lax.broadcasted_iota(I32, (TM, 1), 0)
    pos = ((rid >> TK_SHIFT) * SEG_LEN + kb * TK + (rid & (TK - 1))).astype(F32)
    cos, sin = _rope_tables(pos, inv_ref[...])
    w = SSM_WIDTH
    u = jnp.dot(h, w_ref[:, 0:w], preferred_element_type=F32)
    for s in range(N_SEG):
        for c in range(w // LANE):
            u_ref[0, c, pl.ds(s, TK, stride=N_SEG), :] = u[s * TK:(s + 1) * TK, c * LANE:(c + 1) * LANE]
    q_ref[0] = _rope_heads(jnp.dot(h, w_ref[:, w:2 * w], preferred_element_type=F32), cos, sin).reshape(N_SEG, TK, w)
    k = _rope_heads(jnp.dot(h, w_ref[:, 2 * w:3 * w], preferred_element_type=F32), cos, sin)
    v = jnp.dot(h, w_ref[:, 3 * w:4 * w], preferred_element_type=F32)
    k_ref[0] = k.reshape(N_SEG, TK, w)
    v_ref[0] = v.reshape(N_SEG, TK, w)
    kn_ref[0] = k[TM // 2:].reshape(N_SEG // 2, TK, w)
    vn_ref[0] = v[TM // 2:].reshape(N_SEG // 2, TK, w)


def _inproj(x4, g, inv, w):
    const = lambda b, k: (0, 0)
    blk4 = lambda nseg, width: pl.BlockSpec((1, nseg, TK, width), lambda b, k: (b, 0, k, 0))
    qkv = jax.ShapeDtypeStruct((BATCH, N_SEG, SEG_LEN, ATTN_WIDTH), F32)
    kvn = jax.ShapeDtypeStruct((BATCH, N_SEG // 2, SEG_LEN, ATTN_WIDTH), F32)
    return pl.pallas_call(
        _inproj_kernel, grid=(BATCH, SEG_LEN // TK),
        in_specs=[blk4(N_SEG, D_MODEL), pl.BlockSpec((1, D_MODEL), const),
                  pl.BlockSpec((1, HEAD_DIM), const), pl.BlockSpec(memory_space=pl.ANY)],
        out_specs=[pl.BlockSpec((1, SSM_WIDTH // LANE, TM, LANE), lambda b, k: (b, 0, k, 0)),
                   blk4(N_SEG, ATTN_WIDTH), blk4(N_SEG, ATTN_WIDTH), blk4(N_SEG, ATTN_WIDTH),
                   blk4(N_SEG // 2, ATTN_WIDTH), blk4(N_SEG // 2, ATTN_WIDTH)],
        out_shape=(jax.ShapeDtypeStruct((BATCH, SSM_WIDTH // LANE, SEQ, LANE), F32), qkv, qkv, qkv, kvn, kvn),
        scratch_shapes=[pltpu.VMEM((D_MODEL, D_IN), BF16), pltpu.VMEM((2, D_MODEL, CAST_COLS), F32),
                        pltpu.SemaphoreType.DMA((2,))],
        compiler_params=_cparams(("arbitrary",) * 2), name="inproj",
    )(x4, g, inv, w)


def _ssm_scan_kernel(*refs, final):
    if final:
        (u_ref, bm_ref, lr_ref, li_ref, e_ref, lsr_ref, lsi_ref, cm_ref, d_ref,
         z_ref, xend_ref, bu_scr, xr_scr, xi_scr) = refs
    else:
        u_ref, bm_ref, lr_ref, li_ref, xend_ref, bu_scr, xr_scr, xi_scr = refs
    kb = pl.program_id(2)

    @pl.when(kb == 0)
    def _():
        if final:
            lsr, lsi = lsr_ref[0:1, :], lsi_ref[0:1, :]
            seg_id = lax.broadcasted_iota(I32, (N_SEG, CH_ST), 0)
            pr = pi = jnp.zeros((1, CH_ST), F32)
            xr0 = xi0 = jnp.zeros((N_SEG, CH_ST), F32)
            for s in range(1, N_SEG):
                er = e_ref[0, 0, s - 1:s, 0:CH_ST]
                ei = e_ref[0, 0, s - 1:s, CH_ST:2 * CH_ST]
                pr, pi = er + lsr * pr - lsi * pi, ei + lsr * pi + lsi * pr
                xr0 = jnp.where(seg_id == s, pr, xr0)
                xi0 = jnp.where(seg_id == s, pi, xi0)
            xr_scr[...] = xr0
            xi_scr[...] = xi0
        else:
            xr_scr[...] = jnp.zeros((N_SEG, CH_ST), F32)
            xi_scr[...] = jnp.zeros((N_SEG, CH_ST), F32)

    u = jnp.concatenate([u_ref[0, c] for c in range(CH_IN // LANE)], axis=1)
    bu_scr[...] = jnp.dot(u.astype(BF16), bm_ref[0], preferred_element_type=F32)
    lr, li = lr_ref[...], li_ref[...]

    def step(k, carry):
        xr, xi = carry
        r0 = pl.multiple_of(k * N_SEG, N_SEG)
        bur = bu_scr[pl.ds(r0, N_SEG), 0:CH_ST]
        bui = bu_scr[pl.ds(r0, N_SEG), CH_ST:2 * CH_ST]
        nr = lr * xr - li * xi + bur
        ni = lr * xi + li * xr + bui
        if final:
            bu_scr[pl.ds(r0, N_SEG), 0:CH_ST] = nr
            bu_scr[pl.ds(r0, N_SEG), CH_ST:2 * CH_ST] = ni
        return nr, ni

    xr, xi = lax.fori_loop(0, SSM_TK, step, (xr_scr[...], xi_scr[...]), unroll=4)
    xr_scr[...] = xr
    xi_scr[...] = xi
    if final:
        y = jnp.dot(bu_scr[...].astype(BF16), cm_ref[0], preferred_element_type=F32)
        z = _gelu_tanh(y + d_ref[...] * u)
        for c in range(CH_IN // LANE):
            z_ref[0, c] = z[:, c * LANE:(c + 1) * LANE]

    @pl.when(kb == pl.num_programs(2) - 1)
    def _():
        xend_ref[0, 0] = jnp.concatenate([xr, xi], axis=1)


def _ssm_scan(u_perm, bm, lr8, li8, final, e=None, lsr8=None, lsi8=None, cm=None, dskip=None):
    rows = SSM_TK * N_SEG
    nkb = SEG_LEN // SSM_TK
    in_specs = [pl.BlockSpec((1, CH_IN // LANE, rows, LANE), lambda b, j, k: (b, j, k, 0)),
                pl.BlockSpec((1, CH_IN, 2 * CH_ST), lambda b, j, k: (j, 0, 0)),
                pl.BlockSpec((N_SEG, CH_ST), lambda b, j, k: (0, j)),
                pl.BlockSpec((N_SEG, CH_ST), lambda b, j, k: (0, j))]
    args = [u_perm, bm, lr8, li8]
    xend_spec = pl.BlockSpec((1, 1, N_SEG, 2 * CH_ST), lambda b, j, k: (b, j, 0, 0))
    xend_shape = jax.ShapeDtypeStruct((BATCH, N_CHUNK, N_SEG, 2 * CH_ST), F32)
    if final:
        in_specs += [xend_spec,
                     pl.BlockSpec((N_SEG, CH_ST), lambda b, j, k: (0, j)),
                     pl.BlockSpec((N_SEG, CH_ST), lambda b, j, k: (0, j)),
                     pl.BlockSpec((1, 2 * CH_ST, CH_IN), lambda b, j, k: (j, 0, 0)),
                     pl.BlockSpec((1, CH_IN), lambda b, j, k: (0, j))]
        args += [e, lsr8, lsi8, cm, dskip]
        out_specs = [pl.BlockSpec((1, CH_IN // LANE, rows, LANE), lambda b, j, k: (b, j, k, 0)), xend_spec]
        out_shape = (jax.ShapeDtypeStruct((BATCH, SSM_WIDTH // LANE, SEQ, LANE), F32), xend_shape)
    else:
        out_specs = xend_spec
        out_shape = xend_shape
    return pl.pallas_call(
        functools.partial(_ssm_scan_kernel, final=final), grid=(BATCH, N_CHUNK, nkb),
        in_specs=in_specs, out_specs=out_specs, out_shape=out_shape,
        scratch_shapes=[pltpu.VMEM((rows, 2 * CH_ST), F32),
                        pltpu.VMEM((N_SEG, CH_ST), F32), pltpu.VMEM((N_SEG, CH_ST), F32)],
        compiler_params=_cparams(("arbitrary",) * 3),
        name="ssm_scan_final" if final else "ssm_scan_ends",
    )(*args)


def _attn_kernel(q_ref, k_ref, v_ref, o_ref, o_scr, lse_scr, bias_scr):
    scale2 = HEAD_DIM ** -0.5 * LOG2_E
    qi = lax.broadcasted_iota(I32, (BAND, 2 * BAND), 0)
    kj = lax.broadcasted_iota(I32, (BAND, 2 * BAND), 1)
    cur_ok = (kj >= BAND) & (kj - BAND <= qi)
    prev_ok = (kj < BAND) & (kj >= qi)
    bias_scr[0] = jnp.where(cur_ok, 0.0, NEG)
    bias_scr[1] = jnp.where(cur_ok | prev_ok, 0.0, NEG)
    nt = (((1,), (1,)), ((), ()))
    for bi, (_, d) in enumerate(BRANCHES):
        nblk = SEQ // (BAND * d)
        shift = nblk.bit_length() - 1

        def tile(i, c, bi=bi, d=d, nblk=nblk, shift=shift):
            r = i >> shift
            ib = i & (nblk - 1)
            start = r + ib * (BAND * d)
            pstart = jnp.maximum(start - BAND * d, r)
            cur = pl.ds(start, BAND, stride=d)
            prev = pl.ds(pstart, BAND, stride=d)
            q = (q_ref[0, cur, :] * scale2).astype(BF16)
            k2 = jnp.concatenate([k_ref[0, prev, :], k_ref[0, cur, :]], axis=0).astype(BF16)
            v2 = jnp.concatenate([v_ref[0, prev, :], v_ref[0, cur, :]], axis=0).astype(BF16)
            s = lax.dot_general(q, k2, nt, preferred_element_type=F32) + bias_scr[jnp.minimum(ib, 1)]
            m = jnp.max(s, axis=-1, keepdims=True)
            p = jnp.exp2(s - m)
            l = jnp.sum(p, axis=-1, keepdims=True)
            o = jnp.dot(p.astype(BF16), v2, preferred_element_type=F32) * (1.0 / l)
            o_scr[bi, cur, :] = o
            lse_scr[bi, cur, :] = jnp.broadcast_to(m + jnp.log2(l), (BAND, HEAD_DIM))
            return c

        def softmax_pv(s, v, rows_out, bi=bi):
            m = jnp.max(s, axis=-1, keepdims=True)
            p = jnp.exp2(s - m)
            l = jnp.sum(p, axis=-1, keepdims=True)
            o_scr[bi, rows_out, :] = jnp.dot(p.astype(BF16), v, preferred_element_type=F32) * (1.0 / l)
            lse_scr[bi, rows_out, :] = jnp.broadcast_to(m + jnp.log2(l), (BAND, HEAD_DIM))

        def residue(r, c, d=d):
            sub = pl.ds(r, 2 * BAND, stride=d)
            q = (q_ref[0, sub, :] * scale2).astype(BF16)
            k = k_ref[0, sub, :].astype(BF16)
            v = v_ref[0, sub, :].astype(BF16)
            s0 = lax.dot_general(q[:BAND], k[:BAND], nt, preferred_element_type=F32) + bias_scr[1, :, BAND:]
            softmax_pv(s0, v[:BAND], pl.ds(r, BAND, stride=d))
            s1 = lax.dot_general(q[BAND:], k, nt, preferred_element_type=F32) + bias_scr[1]
            softmax_pv(s1, v, pl.ds(r + BAND * d, BAND, stride=d))
            return c

        if nblk == 2:
            lax.fori_loop(0, d, residue, 0, unroll=16)
        else:
            lax.fori_loop(0, SEQ // BAND, tile, 0, unroll=32)

    rows = 512

    def merge(i, c):
        sl = pl.ds(pl.multiple_of(i * rows, rows), rows)
        l0, l1, l2 = lse_scr[0, sl, :], lse_scr[1, sl, :], lse_scr[2, sl, :]
        m = jnp.maximum(jnp.maximum(l0, l1), l2)
        e0, e1, e2 = jnp.exp2(l0 - m), jnp.exp2(l1 - m), jnp.exp2(l2 - m)
        o_ref[0, sl, :] = ((e0 * o_scr[0, sl, :] + e1 * o_scr[1, sl, :] + e2 * o_scr[2, sl, :])
                           * (1.0 / (e0 + e1 + e2)))
        return c

    lax.fori_loop(0, SEQ // rows, merge, 0)


def _attention(q, k, v):
    spec = pl.BlockSpec((1, SEQ, HEAD_DIM), lambda b, h: (b, 0, h))
    return pl.pallas_call(
        _attn_kernel, grid=(BATCH, N_HEADS), in_specs=[spec, spec, spec], out_specs=spec,
        out_shape=jax.ShapeDtypeStruct((BATCH, SEQ, ATTN_WIDTH), F32),
        scratch_shapes=[pltpu.VMEM((3, SEQ, HEAD_DIM), F32), pltpu.VMEM((3, SEQ, HEAD_DIM), F32),
                        pltpu.VMEM((2, BAND, 2 * BAND), F32)],
        compiler_params=_cparams(("arbitrary",) * 2), name="dilated_attn",
    )(q, k, v)


def _outproj_kernel(x_ref, z_ref, o_ref, wglu_hbm, gs_ref, ga_ref, w_hbm, gf_ref, wr_ref, br_ref, xm_ref, lg_ref,
                    wglu_ref, w_ref, stage, sem):
    is_prompt = pl.program_id(0) < T_PROMPT // TM

    @pl.when(pl.program_id(0) == 0)
    def _():
        _load_weight_bf16(wglu_hbm, wglu_ref, stage.at[:, 0:SSM_WIDTH], sem)
        _load_weight_bf16(w_hbm, w_ref, stage, sem)

    @pl.when(is_prompt)
    def _():
        z = jnp.concatenate(
            [jnp.concatenate([z_ref[0, c, pl.ds(s, TK, stride=N_SEG), :] for c in range(SSM_WIDTH // LANE)], axis=1)
             for s in range(N_SEG)], axis=0)
        zz = z * _sigmoid(jnp.dot(z.astype(BF16), wglu_ref[...], preferred_element_type=F32))
        ms = _rms(zz, gs_ref[...]).astype(BF16)
        ma = _rms(o_ref[0].reshape(TM, ATTN_WIDTH), ga_ref[...]).astype(BF16)
        xm = (x_ref[0].reshape(TM, D_MODEL) + jnp.dot(ms, w_ref[0:SSM_WIDTH, :], preferred_element_type=F32)
              + jnp.dot(ma, w_ref[SSM_WIDTH:, :], preferred_element_type=F32))
        xm_ref[...] = xm
        h = _rms(xm, gf_ref[...]).astype(BF16)
        lg_ref[...] = jnp.dot(h, wr_ref[...], preferred_element_type=F32) + br_ref[...]

    @pl.when(jnp.logical_not(is_prompt))
    def _():
        xm_ref[...] = jnp.zeros((TM, D_MODEL), F32)
        lg_ref[...] = jnp.zeros((TM, 128), F32)


def _outproj(x4, z_slab, o_attn4, wglu, g_ssm, g_attn, w, g_ffn, wr_bf16, br):
    nkb = SEG_LEN // TK

    def split(i):
        i = jnp.minimum(i, T_PROMPT // TM - 1)
        return i // nkb, i % nkb

    def blk4(i):
        b, k = split(i)
        return (b, 0, k, 0)

    row = lambda i: (i, 0)
    const = lambda i: (0, 0)
    return pl.pallas_call(
        _outproj_kernel, grid=(N_TOK_BLOCKS,),
        in_specs=[pl.BlockSpec((1, N_SEG, TK, D_MODEL), blk4),
                  pl.BlockSpec((1, SSM_WIDTH // LANE, TM, LANE), blk4),
                  pl.BlockSpec((1, N_SEG, TK, ATTN_WIDTH), blk4),
                  pl.BlockSpec(memory_space=pl.ANY),
                  pl.BlockSpec((1, SSM_WIDTH), const),
                  pl.BlockSpec((1, ATTN_WIDTH), const),
                  pl.BlockSpec(memory_space=pl.ANY),
                  pl.BlockSpec((1, D_MODEL), const),
                  pl.BlockSpec((D_MODEL, 128), const),
                  pl.BlockSpec((1, 128), const)],
        out_specs=[pl.BlockSpec((TM, D_MODEL), row), pl.BlockSpec((TM, 128), row)],
        out_shape=(jax.ShapeDtypeStruct((T_PAD, D_MODEL), F32), jax.ShapeDtypeStruct((T_PAD, 128), F32)),
        scratch_shapes=[pltpu.VMEM((SSM_WIDTH, SSM_WIDTH), BF16), pltpu.VMEM((D_MODEL, D_MODEL), BF16),
                        pltpu.VMEM((2, D_MODEL, CAST_COLS), F32), pltpu.SemaphoreType.DMA((2,))],
        compiler_params=_cparams(("arbitrary",)), name="outproj",
    )(x4, z_slab, o_attn4, wglu, g_ssm, g_attn, w, g_ffn, wr_bf16, br)


def _s_inproj_kernel(x_ref, g_ref, w_ref, o_ref):
    h = _rms(x_ref[...], g_ref[...])
    o_ref[...] = jnp.dot(h, w_ref[...], preferred_element_type=F32, precision=HIGHEST)


def _s_inproj(x, g, w):
    tn = 1024
    return pl.pallas_call(
        _s_inproj_kernel, grid=(D_IN // tn,),
        in_specs=[pl.BlockSpec((DEC_BATCH, D_MODEL), lambda j: (0, 0)),
                  pl.BlockSpec((1, D_MODEL), lambda j: (0, 0)),
                  pl.BlockSpec((D_MODEL, tn), lambda j: (0, j))],
        out_specs=pl.BlockSpec((DEC_BATCH, tn), lambda j: (0, j)),
        out_shape=jax.ShapeDtypeStruct((DEC_BATCH, D_IN), F32),
        compiler_params=_cparams(("arbitrary",)), name="s_inproj",
    )(x, g, w)


def _s_ssm_kernel(u_ref, sr_ref, si_ref, lr_ref, li_ref, bm_ref, cm_ref, d_ref, z_ref, nr_ref, ni_ref):
    u = u_ref[...]
    bu = jnp.dot(u, bm_ref[0], preferred_element_type=F32, precision=HIGHEST)
    lr, li = lr_ref[...], li_ref[...]
    xr, xi = sr_ref[...], si_ref[...]
    nr = lr * xr - li * xi + bu[:, :CH_ST]
    ni = lr * xi + li * xr + bu[:, CH_ST:]
    nr_ref[...] = nr
    ni_ref[...] = ni
    y = jnp.dot(jnp.concatenate([nr, ni], axis=1), cm_ref[0], preferred_element_type=F32, precision=HIGHEST)
    z_ref[...] = _gelu_tanh(y + d_ref[...] * u)


def _s_ssm(u, sr, si, lr1, li1, bm, cm, dskip):
    st = jax.ShapeDtypeStruct((DEC_BATCH, N_GROUPS * SSM_STATE), F32)
    return pl.pallas_call(
        _s_ssm_kernel, grid=(N_CHUNK,),
        in_specs=[pl.BlockSpec((DEC_BATCH, CH_IN), lambda j: (0, j)),
                  pl.BlockSpec((DEC_BATCH, CH_ST), lambda j: (0, j)),
                  pl.BlockSpec((DEC_BATCH, CH_ST), lambda j: (0, j)),
                  pl.BlockSpec((1, CH_ST), lambda j: (0, j)),
                  pl.BlockSpec((1, CH_ST), lambda j: (0, j)),
                  pl.BlockSpec((1, CH_IN, 2 * CH_ST), lambda j: (j, 0, 0)),
                  pl.BlockSpec((1, 2 * CH_ST, CH_IN), lambda j: (j, 0, 0)),
                  pl.BlockSpec((1, CH_IN), lambda j: (0, j))],
        out_specs=[pl.BlockSpec((DEC_BATCH, CH_IN), lambda j: (0, j)),
                   pl.BlockSpec((DEC_BATCH, CH_ST), lambda j: (0, j)),
                   pl.BlockSpec((DEC_BATCH, CH_ST), lambda j: (0, j))],
        out_shape=(jax.ShapeDtypeStruct((DEC_BATCH, SSM_WIDTH), F32), st, st),
        compiler_params=_cparams(("arbitrary",)), name="s_ssm",
    )(u, sr, si, lr1, li1, bm, cm, dskip)


def _s_glu_kernel(z_ref, w_ref, g_ref, o_ref):
    z = z_ref[...]
    zz = z * _sigmoid(jnp.dot(z, w_ref[...], preferred_element_type=F32, precision=HIGHEST))
    o_ref[...] = _rms(zz, g_ref[...])


def _s_glu(z, w, g):
    return pl.pallas_call(
        _s_glu_kernel, out_shape=jax.ShapeDtypeStruct((DEC_BATCH, SSM_WIDTH), F32),
        compiler_params=pltpu.CompilerParams(vmem_limit_bytes=VMEM_LIMIT), name="s_glu",
    )(z, w, g)


def _s_attn_kernel(q_ref, k_ref, v_ref, inv_ref, k1_ref, v1_ref, k4_ref, v4_ref, k16_ref, v16_ref,
                   o_ref, kr_ref):
    scale = HEAD_DIM ** -0.5
    pos = jnp.full((N_HEADS, 1), float(PAST_LEN), F32)
    cos, sin = _rope_tables(pos, inv_ref[...])
    q = q_ref[...] * cos + pltpu.roll(q_ref[...], HEAD_DIM // 2, 1) * sin
    kn = k_ref[...] * cos + pltpu.roll(k_ref[...], HEAD_DIM // 2, 1) * sin
    vn = v_ref[...]
    kr_ref[...] = kn
    s0 = jnp.sum(q * kn, axis=-1, keepdims=True) * scale
    outs, lses = [], []
    for kc_ref, vc_ref in ((k1_ref, v1_ref), (k4_ref, v4_ref), (k16_ref, v16_ref)):
        s = jnp.sum(kc_ref[...] * q[None], axis=-1, keepdims=True) * scale
        m = jnp.maximum(jnp.max(s, axis=0), s0)
        p = jnp.exp(s - m[None])
        p0 = jnp.exp(s0 - m)
        l = jnp.sum(p, axis=0) + p0
        outs.append((jnp.sum(p * vc_ref[...], axis=0) + p0 * vn) / l)
        lses.append(m + jnp.log(l))
    mm = jnp.maximum(jnp.maximum(lses[0], lses[1]), lses[2])
    es = [jnp.exp(x - mm) for x in lses]
    o_ref[...] = (es[0] * outs[0] + es[1] * outs[1] + es[2] * outs[2]) / (es[0] + es[1] + es[2])


def _s_attention(q, k, v, inv, cache_k, cache_v):
    hd = (N_HEADS, HEAD_DIM)
    tok = pl.BlockSpec((None,) + hd, lambda b: (b, 0, 0))
    args, specs = [], []
    for _, d in BRANCHES:
        nrow = CACHE_LEN // d
        last = nrow // BAND - 1
        spec = pl.BlockSpec((None, BAND, None) + hd, lambda b, last=last: (b, last, 0, 0, 0))
        for c in (cache_k, cache_v):
            args.append(c.reshape(DEC_BATCH, nrow, d, *hd))
            specs.append(spec)
    out = jax.ShapeDtypeStruct((DEC_BATCH,) + hd, F32)
    return pl.pallas_call(
        _s_attn_kernel, grid=(DEC_BATCH,),
        in_specs=[tok, tok, tok, pl.BlockSpec((1, HEAD_DIM), lambda b: (0, 0))] + specs,
        out_specs=[tok, tok], out_shape=(out, out),
        compiler_params=_cparams(("arbitrary",)), name="s_attn",
    )(q, k, v, inv, *args)


def _s_outproj_kernel(xa_ref, lga_ref, x_ref, ms_ref, o_ref, ga_ref, w_ref, gf_ref, wr_ref, br_ref,
                      xm_ref, lg_ref):
    del xa_ref, lga_ref
    mix = jnp.concatenate([ms_ref[...], _rms(o_ref[...], ga_ref[...])], axis=1)
    xm = x_ref[...] + jnp.dot(mix, w_ref[...], preferred_element_type=F32, precision=HIGHEST)
    h = _rms(xm, gf_ref[...])
    lg = jnp.dot(h, wr_ref[...], preferred_element_type=F32, precision=HIGHEST) + br_ref[...]
    pad = TM - DEC_BATCH
    xm_ref[...] = jnp.concatenate([xm, jnp.zeros((pad, D_MODEL), F32)], axis=0)
    lg_ref[...] = jnp.concatenate([lg, jnp.zeros((pad, 128), F32)], axis=0)


def _s_outproj(xm_all, lg_all, x, mix_ssm, o_attn, g_attn, w, g_ffn, wr, br):
    last = lambda i: (N_TOK_BLOCKS - 1, 0)
    full = lambda shape: pl.BlockSpec(shape, lambda i: (0, 0))
    any_spec = pl.BlockSpec(memory_space=pl.ANY)
    return pl.pallas_call(
        _s_outproj_kernel, grid=(1,),
        in_specs=[any_spec, any_spec,
                  full((DEC_BATCH, D_MODEL)), full((DEC_BATCH, SSM_WIDTH)), full((DEC_BATCH, ATTN_WIDTH)),
                  full((1, ATTN_WIDTH)), full((D_MODEL, D_MODEL)), full((1, D_MODEL)),
                  full((D_MODEL, 128)), full((1, 128))],
        out_specs=[pl.BlockSpec((TM, D_MODEL), last), pl.BlockSpec((TM, 128), last)],
        out_shape=(jax.ShapeDtypeStruct(xm_all.shape, F32), jax.ShapeDtypeStruct(lg_all.shape, F32)),
        input_output_aliases={0: 0, 1: 1},
        compiler_params=_cparams(("arbitrary",)), name="s_outproj",
    )(xm_all, lg_all, x, mix_ssm, o_attn, g_attn, w, g_ffn, wr, br)


def _route_kernel(lg_ref, ri_ref, rw_ref, cnt_ref, carry):
    i = pl.program_id(0)

    @pl.when(i == 0)
    def _():
        carry[...] = jnp.zeros((1, 128), F32)

    x = lg_ref[...]
    lane = lax.broadcasted_iota(I32, (ROUTE_ROWS, 128), 1)
    row = lax.broadcasted_iota(I32, (ROUTE_ROWS, 128), 0) + i * ROUTE_ROWS
    valid = row < T_REAL
    big = jnp.int32(1 << 20)
    gmask = lane < N_EXPERT_GROUPS
    lgm = jnp.where(gmask, x, NEG)
    m = jnp.max(lgm, axis=-1, keepdims=True)
    gate1 = 1.0 / jnp.sum(jnp.where(gmask, jnp.exp(lgm - m), 0.0), axis=-1, keepdims=True)
    grp = jnp.min(jnp.where(gmask & (lgm == m), lane, big), axis=-1, keepdims=True)
    lo = N_EXPERT_GROUPS + EXPERTS_PER_GROUP * grp
    emask = (lane >= lo) & (lane < lo + EXPERTS_PER_GROUP)
    le = jnp.where(emask, x, NEG)
    t1 = jnp.max(le, axis=-1, keepdims=True)
    i1 = jnp.min(jnp.where(emask & (le == t1), lane, big), axis=-1, keepdims=True)
    emask2 = emask & (lane != i1)
    le2 = jnp.where(emask2, x, NEG)
    t2 = jnp.max(le2, axis=-1, keepdims=True)
    i2 = jnp.min(jnp.where(emask2 & (le2 == t2), lane, big), axis=-1, keepdims=True)
    e21 = jnp.exp(t2 - t1)
    w1 = gate1 / (1.0 + e21)
    w2 = gate1 * e21 / (1.0 + e21)
    eid1, eid2 = i1 - N_EXPERT_GROUPS, i2 - N_EXPERT_GROUPS
    oh1 = jnp.where(valid & (lane == eid1), 1.0, 0.0)
    oh2 = jnp.where(valid & (lane == eid2), 1.0, 0.0)
    a = oh1 + oh2
    rr = lax.broadcasted_iota(I32, (ROUTE_ROWS, ROUTE_ROWS), 0)
    cc = lax.broadcasted_iota(I32, (ROUTE_ROWS, ROUTE_ROWS), 1)
    before = jnp.where(cc < rr, 1.0, 0.0).astype(BF16)
    pre = jnp.dot(before, a.astype(BF16), preferred_element_type=F32) + carry[...]
    rank1 = jnp.sum(oh1 * pre, axis=-1, keepdims=True).astype(I32)
    rank2 = jnp.sum(oh2 * pre, axis=-1, keepdims=True).astype(I32)
    carry[...] = carry[...] + jnp.sum(a, axis=0, keepdims=True)
    cnt_ref[...] = jnp.broadcast_to(carry[...], (8, 128))
    zi = jnp.zeros((ROUTE_ROWS, 128), I32)
    ri = jnp.where(lane == 0, eid1, jnp.where(lane == 1, eid2, jnp.where(lane == 2, rank1, jnp.where(lane == 3, rank2, zi))))
    ri_ref[...] = jnp.where(valid, ri, zi)
    rw = jnp.where(lane == 0, w1, jnp.where(lane == 1, w2, 0.0))
    rw_ref[...] = jnp.where(valid, rw, 0.0)


def _route(logits):
    blk = pl.BlockSpec((ROUTE_ROWS, 128), lambda i: (i, 0))
    return pl.pallas_call(
        _route_kernel, grid=(T_PAD // ROUTE_ROWS,), in_specs=[blk],
        out_specs=[blk, blk, pl.BlockSpec((8, 128), lambda i: (0, 0))],
        out_shape=(jax.ShapeDtypeStruct((T_PAD, 128), I32), jax.ShapeDtypeStruct((T_PAD, 128), F32),
                   jax.ShapeDtypeStruct((8, 128), F32)),
        scratch_shapes=[pltpu.VMEM((1, 128), F32)],
        compiler_params=_cparams(("arbitrary",)), name="route",
    )(logits)


def _row_copy(src, s, dst, d, sem):
    return pltpu.make_async_copy(src.at[pl.ds(s, 1)], dst.at[pl.ds(d, 1)], sem)


def _invert_kernel(poff_ref, ri_ref, dest_ref, rt_ref):
    i = pl.program_id(0)

    @pl.when(i == 0)
    def _():
        def clear(p, c):
            rt_ref[p] = 0
            return c
        lax.fori_loop(0, N_MOE_ROWS, clear, 0, unroll=32)

    def put(r, c):
        t = i * TM + r
        for s in range(2):
            d = poff_ref[ri_ref[0, 0, 4 * r + s]] + ri_ref[0, 0, 4 * r + 2 + s]
            dest_ref[0, 0, 2 * r + s] = d
            rt_ref[d] = t
        return c

    def blank(r, c):
        dest_ref[0, 0, 2 * r] = 0
        dest_ref[0, 0, 2 * r + 1] = 0
        return c

    @pl.when(i < T_PROMPT // TM)
    def _():
        lax.fori_loop(0, TM, put, 0, unroll=8)

    @pl.when(i == T_PROMPT // TM)
    def _():
        lax.fori_loop(0, DEC_BATCH, put, 0, unroll=8)
        lax.fori_loop(DEC_BATCH, TM, blank, 0, unroll=8)


def _invert(poff, ri4):
    grid_spec = pltpu.PrefetchScalarGridSpec(
        num_scalar_prefetch=1, grid=(N_TOK_BLOCKS,),
        in_specs=[pl.BlockSpec((1, 1, 4 * TM), lambda i, po: (i, 0, 0), memory_space=pltpu.SMEM)],
        out_specs=[pl.BlockSpec((1, 1, 2 * TM), lambda i, po: (i, 0, 0), memory_space=pltpu.SMEM),
                   pl.BlockSpec(memory_space=pltpu.SMEM)])
    return pl.pallas_call(
        _invert_kernel, grid_spec=grid_spec,
        out_shape=(jax.ShapeDtypeStruct((N_TOK_BLOCKS, 1, 2 * TM), I32),
                   jax.ShapeDtypeStruct((N_MOE_ROWS,), I32)),
        compiler_params=_cparams(("arbitrary",)), name="moe_invert",
    )(poff, ri4)


def _expert_kernel(fb_ref, nv_ref, rt_ref, xm_ref, gf_ref, wg_ref, wu_ref, wd_ref, y_ref,
                   xbuf, ybuf, gsem, ysem, wgu_scr, wd_scr):
    e = pl.program_id(0)
    first, end, total = fb_ref[e], fb_ref[e + 1], fb_ref[N_EXPERTS]

    def y_copy(g, slot):
        return pltpu.make_async_copy(ybuf.at[slot], y_ref.at[pl.ds(g * MOE_BLK, MOE_BLK)], ysem.at[slot])

    def gather(g, slot, wait):
        def body(r, c):
            if wait:
                _row_copy(xm_ref, 0, xbuf.at[slot], 0, gsem.at[slot]).wait()
            else:
                _row_copy(xm_ref, rt_ref[g * MOE_BLK + r], xbuf.at[slot], r,
                          gsem.at[slot]).start(priority=GATHER_PRIORITY)
            return c
        lax.fori_loop(0, nv_ref[g], body, 0)

    @pl.when(e == 0)
    def _():
        xbuf[...] = jnp.zeros(xbuf.shape, F32)
        for d in range(GATHER_DEPTH - 1):
            gather(d, d, wait=False)

    @pl.when(end > first)
    def _():
        wgu_scr[:, 0:EXPERT_FF] = wg_ref[0].astype(BF16)
        wgu_scr[:, EXPERT_FF:] = wu_ref[0].astype(BF16)
        wd_scr[...] = wd_ref[0].astype(BF16)

    def block(g, carry):
        slot = g % 2
        xslot = g % GATHER_DEPTH

        gather(g + GATHER_DEPTH - 1, (g + GATHER_DEPTH - 1) % GATHER_DEPTH, wait=False)
        gather(g, xslot, wait=True)
        h = _rms(xbuf[xslot], gf_ref[...]).astype(BF16)
        gu = jnp.dot(h, wgu_scr[...], preferred_element_type=F32)
        gate, up = gu[:, :EXPERT_FF], gu[:, EXPERT_FF:]
        act = (gate * _sigmoid(gate) * up).astype(BF16)
        y = jnp.dot(act, wd_scr[...], preferred_element_type=F32)

        @pl.when(g >= 2)
        def _():
            y_copy(g - 2, slot).wait()
        ybuf[slot] = y
        y_copy(g, slot).start()
        return carry

    lax.fori_loop(first, end, block, 0)

    @pl.when(e == N_EXPERTS - 1)
    def _():
        @pl.when(total >= 2)
        def _():
            y_copy(total - 2, total % 2).wait()

        @pl.when(total >= 1)
        def _():
            y_copy(total - 1, (total - 1) % 2).wait()

        ybuf[0] = jnp.zeros((MOE_BLK, D_MODEL), F32)

        def zero_start(g, c):
            y_copy(g, 0).start()
            return c

        def zero_wait(g, c):
            y_copy(g, 0).wait()
            return c

        lax.fori_loop(total, N_MOE_BLOCKS, zero_start, 0)
        lax.fori_loop(total, N_MOE_BLOCKS, zero_wait, 0)


def _experts(first_blk, nvalid, row_tok, xm_all, g_ffn, w_g, w_u, w_d):
    grid_spec = pltpu.PrefetchScalarGridSpec(
        num_scalar_prefetch=3, grid=(N_EXPERTS,),
        in_specs=[pl.BlockSpec(memory_space=pl.ANY),
                  pl.BlockSpec((1, D_MODEL), lambda e, fb, nv, rt: (0, 0)),
                  pl.BlockSpec((1, D_MODEL, EXPERT_FF), lambda e, fb, nv, rt: (e, 0, 0)),
                  pl.BlockSpec((1, D_MODEL, EXPERT_FF), lambda e, fb, nv, rt: (e, 0, 0)),
                  pl.BlockSpec((1, EXPERT_FF, D_MODEL), lambda e, fb, nv, rt: (e, 0, 0))],
        out_specs=pl.BlockSpec(memory_space=pl.ANY),
        scratch_shapes=[pltpu.VMEM((GATHER_DEPTH, MOE_BLK, D_MODEL), F32), pltpu.VMEM((2, MOE_BLK, D_MODEL), F32),
                        pltpu.SemaphoreType.DMA((GATHER_DEPTH,)), pltpu.SemaphoreType.DMA((2,)),
                        pltpu.VMEM((D_MODEL, 2 * EXPERT_FF), BF16), pltpu.VMEM((EXPERT_FF, D_MODEL), BF16)])
    return pl.pallas_call(
        _expert_kernel, grid_spec=grid_spec,
        out_shape=jax.ShapeDtypeStruct((N_MOE_ROWS, D_MODEL), F32),
        compiler_params=_cparams(("arbitrary",)), name="moe_experts",
    )(first_blk, nvalid, row_tok, xm_all, g_ffn, w_g, w_u, w_d)


def _combine_kernel(dest_ref, destn_ref, rw_ref, xm_ref, g_ref, y_ref, o_ref, ybuf, sems, *, rows, nblk):
    i = pl.program_id(0)
    slot = i % 2

    def gather(d_ref, s, wait):
        def body(r, c):
            for j in range(2):
                cp = _row_copy(y_ref, 0 if wait else d_ref[0, 0, 2 * r + j], ybuf.at[s, j], r, sems.at[s])
                cp.wait() if wait else cp.start(priority=j)
            return c
        lax.fori_loop(0, rows, body, 0, unroll=8)

    @pl.when(i == 0)
    def _():
        gather(dest_ref, 0, wait=False)

    @pl.when(i + 1 < nblk)
    def _():
        gather(destn_ref, 1 - slot, wait=False)

    gather(dest_ref, slot, wait=True)
    w = rw_ref[...]
    x = xm_ref[...] + w[:, 0:1] * ybuf[slot, 0] + w[:, 1:2] * ybuf[slot, 1]
    o_ref[...] = _rms(x, g_ref[...]).reshape(o_ref.shape)


def _combine(dest3, rw, xm, g, y_rows, rows, nblk, blk0, out_spec, out_shape):
    smem_blk = lambda f: pl.BlockSpec((1, 1, 2 * rows), f, memory_space=pltpu.SMEM)
    return pl.pallas_call(
        functools.partial(_combine_kernel, rows=rows, nblk=nblk), grid=(nblk,),
        in_specs=[smem_blk(lambda i: (i, 0, 0)),
                  smem_blk(lambda i: (jnp.minimum(i + 1, nblk - 1), 0, 0)),
                  pl.BlockSpec((rows, 128), lambda i: (blk0 + i, 0)),
                  pl.BlockSpec((rows, D_MODEL), lambda i: (blk0 + i, 0)),
                  pl.BlockSpec((1, D_MODEL), lambda i: (0, 0)),
                  pl.BlockSpec(memory_space=pl.ANY)],
        out_specs=out_spec, out_shape=out_shape,
        scratch_shapes=[pltpu.VMEM((2, 2, rows, D_MODEL), F32), pltpu.SemaphoreType.DMA((2,))],
        compiler_params=_cparams(("arbitrary",)), name="moe_combine",
    )(dest3, dest3, rw, xm, g, y_rows)


def kernel(x_prompt, x_sample, cache_k, cache_v, state_ssm_re, state_ssm_im, g_mix_norm, w_in, ssm_a_re, ssm_a_im, ssm_log_dt, ssm_b_re, ssm_b_im, ssm_c_re, ssm_c_im, ssm_d, w_glu, g_ssm_out, g_attn_out, w_out, g_ffn_norm, w_router_group, b_router_group, w_router_expert, b_router_expert, w_expert_gate, w_expert_up, w_expert_down, g_final):
    row = lambda a: a.reshape(1, -1)
    half = HEAD_DIM // 2
    inv = ROPE_THETA ** (-jnp.arange(half, dtype=F32) / half)
    inv = jnp.concatenate([inv, inv]).reshape(1, HEAD_DIM)

    lam_r, lam_i, lams_r, lams_i, bb_r, bb_i = _ssm_params(ssm_a_re[0], ssm_a_im[0], ssm_log_dt[0], ssm_b_re[0], ssm_b_im[0])
    bm, bm16, cm, cm16 = _blockdiag(bb_r, bb_i, ssm_c_re[0], ssm_c_im[0])
    dskip = row(ssm_d[0])
    bcast8 = lambda a: jnp.broadcast_to(a.reshape(1, -1), (N_SEG, N_GROUPS * SSM_STATE))
    lr8, li8, lsr8, lsi8 = bcast8(lam_r), bcast8(lam_i), bcast8(lams_r), bcast8(lams_i)

    wr = jnp.concatenate([w_router_group[0], w_router_expert[0].reshape(D_MODEL, N_EXPERTS)], axis=1)
    wr = jnp.pad(wr, ((0, 0), (0, 128 - wr.shape[1])))
    br = jnp.pad(jnp.concatenate([b_router_group[0], b_router_expert[0].reshape(-1)]), (0, 128 - 36)).reshape(1, 128)

    x4 = x_prompt.reshape(BATCH, N_SEG, SEG_LEN, D_MODEL)
    u_il, q4, k4, v4, k_new, v_new = _inproj(x4, row(g_mix_norm[0]), inv, w_in[0])
    ends = _ssm_scan(u_il, bm16, lr8, li8, final=False)
    z_il, xend = _ssm_scan(u_il, bm16, lr8, li8, final=True, e=ends, lsr8=lsr8, lsi8=lsi8, cm=cm16, dskip=dskip)
    flat = lambda a: a.reshape(BATCH, SEQ, ATTN_WIDTH)
    o_attn = _attention(flat(q4), flat(k4), flat(v4))
    xm_all, lg_all = _outproj(
        x4, z_il, o_attn.reshape(BATCH, N_SEG, SEG_LEN, ATTN_WIDTH), w_glu[0], row(g_ssm_out[0]),
        row(g_attn_out[0]), w_out[0], row(g_ffn_norm[0]), wr.astype(BF16), br)

    xs = x_sample.reshape(DEC_BATCH, D_MODEL)
    proj = _s_inproj(xs, row(g_mix_norm[0]), w_in[0])
    us = proj[:, :SSM_WIDTH]
    qkv = proj[:, SSM_WIDTH:].reshape(DEC_BATCH, 3, N_HEADS, HEAD_DIM)
    zs, ns_r, ns_i = _s_ssm(us, state_ssm_re[0].reshape(DEC_BATCH, -1), state_ssm_im[0].reshape(DEC_BATCH, -1),
                            row(lam_r), row(lam_i), bm, cm, dskip)
    mix_ssm_s = _s_glu(zs, w_glu[0], row(g_ssm_out[0]))
    o_s, k_rot = _s_attention(qkv[:, 0], qkv[:, 1], qkv[:, 2], inv, cache_k[0], cache_v[0])
    xm_all, lg_all = _s_outproj(
        xm_all, lg_all, xs, mix_ssm_s, o_s.reshape(DEC_BATCH, ATTN_WIDTH), row(g_attn_out[0]),
        w_out[0], row(g_ffn_norm[0]), wr, br)

    ri, rw, cnt = _route(lg_all)
    counts = cnt[0, :N_EXPERTS].astype(I32)
    padded = (counts + MOE_BLK - 1) // MOE_BLK * MOE_BLK
    pend = jnp.cumsum(padded)
    poff = pend - padded
    first_blk = jnp.concatenate([poff, pend[-1:]]) // MOE_BLK
    blk_start = jnp.arange(N_MOE_BLOCKS + GATHER_DEPTH, dtype=I32) * MOE_BLK
    block_e = jnp.minimum(jnp.sum(pend[None, :] <= blk_start[:, None], axis=1), N_EXPERTS - 1)
    nvalid = jnp.clip(counts[block_e] - (blk_start - poff[block_e]), 0, MOE_BLK).astype(I32)
    dest3, row_tok = _invert(poff, ri[:, 0:4].reshape(N_TOK_BLOCKS, 1, 4 * TM))
    y_rows = _experts(first_blk, nvalid, row_tok, xm_all, row(g_ffn_norm[0]),
                      w_expert_gate[0], w_expert_up[0], w_expert_down[0])
    gfin = row(g_final)
    nkb = SEG_LEN // TK
    y_prompt = _combine(dest3, rw, xm_all, gfin, y_rows, TM, T_PROMPT // TM, 0,
                        pl.BlockSpec((1, N_SEG, TK, D_MODEL), lambda i: (i // nkb, 0, i % nkb, 0)),
                        jax.ShapeDtypeStruct((BATCH, N_SEG, SEG_LEN, D_MODEL), F32))
    dest_s = dest3[N_TOK_BLOCKS - 1:, :, :2 * DEC_BATCH]
    y_sample = _combine(dest_s, rw, xm_all, gfin, y_rows, DEC_BATCH, 1, T_PROMPT // DEC_BATCH,
                        pl.BlockSpec((DEC_BATCH, D_MODEL), lambda i: (0, 0)),
                        jax.ShapeDtypeStruct((DEC_BATCH, D_MODEL), F32))

    kv_shape = (1, BATCH, CACHE_LEN, N_HEADS, HEAD_DIM)
    st_p = lambda a: a[:, :, N_SEG - 1, :].reshape(1, BATCH, N_GROUPS, SSM_STATE)
    st_s = lambda a: a.reshape(1, DEC_BATCH, N_GROUPS, SSM_STATE)
    kvs = lambda a: a.reshape(1, DEC_BATCH, 1, N_HEADS, HEAD_DIM)
    return (y_prompt.reshape(BATCH, SEQ, D_MODEL), y_sample.reshape(DEC_BATCH, 1, D_MODEL),
            k_new.reshape(kv_shape), v_new.reshape(kv_shape),
            st_p(xend[..., :CH_ST]), st_p(xend[..., CH_ST:]),
            kvs(k_rot), kvs(qkv[:, 2]), st_s(ns_r), st_s(ns_i))
```

```python
import functools
import math

import jax
import jax.numpy as jnp
from jax import lax
from jax.experimental import pallas as pl
from jax.experimental.pallas import tpu as pltpu

F32 = jnp.float32
BF16 = jnp.bfloat16
I32 = jnp.int32
U32 = jnp.uint32
HIGHEST = lax.Precision.HIGHEST

D_MODEL = 2048
BATCH = 2
SEQ = 4096
DEC_BATCH = 32
PAST_LEN = 8192
CACHE_LEN = 2048
SSM_WIDTH = 1024
ATTN_WIDTH = 1024
SSM_GROUP = 16
N_GROUPS = 64
SSM_STATE = 64
HEAD_DIM = 128
N_HEADS = 8
D_IN = 4096
BRANCHES = ((128, 1), (512, 4), (2048, 16))
BAND = 128
ROPE_THETA = 10000.0
N_EXPERT_GROUPS = 4
EXPERTS_PER_GROUP = 8
N_EXPERTS = 32
EXPERT_FF = 512
NORM_EPS = 1e-6

N_SEG = 8
SEG_LEN = SEQ // N_SEG
N_CHUNK = 4
CH_IN = SSM_WIDTH // N_CHUNK
CH_ST = N_GROUPS * SSM_STATE // N_CHUNK
SSM_TK = 256
TM = 256
TK = TM // N_SEG
TK_SHIFT = TK.bit_length() - 1
LANE = 128
T_PROMPT = BATCH * SEQ
T_REAL = T_PROMPT + DEC_BATCH
T_PAD = T_PROMPT + TM
N_TOK_BLOCKS = T_PAD // TM
ROUTE_ROWS = 3 * TM
MOE_BLK = 256
N_MOE_BLOCKS = -(-(2 * T_REAL) // MOE_BLK) + N_EXPERTS
N_MOE_ROWS = N_MOE_BLOCKS * MOE_BLK
VMEM_LIMIT = 56 * 1024 * 1024
NEG = -1e30
LOG2_E = 1.4426950408889634
CAST_COLS = 512
GATHER_PRIORITY = 1
GATHER_DEPTH = 2


def _cparams(sem, vmem=VMEM_LIMIT):
    return pltpu.CompilerParams(dimension_semantics=sem, vmem_limit_bytes=vmem)


def _rms(x, g):
    return x * lax.rsqrt(jnp.mean(x * x, axis=-1, keepdims=True) + NORM_EPS) * g


def _gelu_tanh(y):
    return 0.5 * y * (1.0 + jnp.tanh(0.7978845608028654 * (y + 0.044715 * (y * y * y))))


def _sigmoid(x):
    return 1.0 / (1.0 + jnp.exp(-x))


def _rope_tables(pos, inv):
    ang = pos * inv
    lane = lax.broadcasted_iota(I32, ang.shape, 1)
    return jnp.cos(ang), jnp.where(lane < HEAD_DIM // 2, -jnp.sin(ang), jnp.sin(ang))


def _rope_heads(x, cos, sin):
    outs = []
    for h in range(x.shape[1] // HEAD_DIM):
        xh = x[:, h * HEAD_DIM:(h + 1) * HEAD_DIM]
        outs.append(xh * cos + pltpu.roll(xh, HEAD_DIM // 2, 1) * sin)
    return jnp.concatenate(outs, axis=1)


def _ssm_param_kernel(are_ref, aim_ref, ldt_ref, btr_ref, bti_ref,
                      lr_ref, li_ref, lsr_ref, lsi_ref, bbr_ref, bbi_ref):
    ar, ai = are_ref[...], aim_ref[...]
    dt = jnp.exp(ldt_ref[...])
    er, ei = ar * dt, ai * dt
    mag = jnp.exp(er)
    lr, li = mag * jnp.cos(ei), mag * jnp.sin(ei)
    lr_ref[...] = lr
    li_ref[...] = li
    mag_s = jnp.exp(er * SEG_LEN)
    lsr_ref[...] = mag_s * jnp.cos(ei * SEG_LEN)
    lsi_ref[...] = mag_s * jnp.sin(ei * SEG_LEN)
    xr, xi = lr - 1.0, li
    den = ar * ar + ai * ai
    cr = (xr * ar + xi * ai) / den
    ci = (xi * ar - xr * ai) / den
    btr, bti = btr_ref[...], bti_ref[...]
    bbr_ref[...] = cr[:, None, :] * btr - ci[:, None, :] * bti
    bbi_ref[...] = cr[:, None, :] * bti + ci[:, None, :] * btr


def _ssm_params(a_re, a_im, log_dt, b_re, b_im):
    g, n, p = N_GROUPS, SSM_STATE, SSM_GROUP
    gn = jax.ShapeDtypeStruct((g, n), F32)
    gpn = jax.ShapeDtypeStruct((g, p, n), F32)
    return pl.pallas_call(
        _ssm_param_kernel, out_shape=(gn, gn, gn, gn, gpn, gpn), name="ssm_params",
    )(a_re, a_im, log_dt.reshape(g, 1), b_re.transpose(0, 2, 1), b_im.transpose(0, 2, 1))


def _blockdiag_kernel(br_ref, bi_ref, cr_ref, ci_ref, bm_ref, bm16_ref, cm_ref, cm16_ref):
    bm_ref[...] = jnp.zeros(bm_ref.shape, F32)
    cm_ref[...] = jnp.zeros(cm_ref.shape, F32)
    gp, gl = 2 * SSM_GROUP, LANE
    keep_b = ((lax.broadcasted_iota(I32, (gp, gl), 0) // SSM_GROUP)
              == (lax.broadcasted_iota(I32, (gp, gl), 1) // SSM_STATE))
    for q in range(N_GROUPS // 2):
        j, p = divmod(q, 8)
        rows, cols = slice(gp * p, gp * (p + 1)), slice(gl * p, gl * (p + 1))
        bm_ref[j, rows, cols] = jnp.where(keep_b, br_ref[q], 0.0)
        bm_ref[j, rows, CH_ST + gl * p:CH_ST + gl * (p + 1)] = jnp.where(keep_b, bi_ref[q], 0.0)
    oct_rows = 8 * SSM_STATE
    keep_c = ((lax.broadcasted_iota(I32, (oct_rows, gl), 0) // SSM_STATE)
              == (lax.broadcasted_iota(I32, (oct_rows, gl), 1) // SSM_GROUP))
    for o in range(N_GROUPS // 8):
        j, half = divmod(o, 2)
        rows, cols = slice(oct_rows * half, oct_rows * (half + 1)), slice(gl * half, gl * (half + 1))
        cm_ref[j, rows, cols] = jnp.where(keep_c, cr_ref[o], 0.0)
        cm_ref[j, CH_ST + oct_rows * half:CH_ST + oct_rows * (half + 1), cols] = jnp.where(keep_c, -ci_ref[o], 0.0)
    bm16_ref[...] = bm_ref[...].astype(BF16)
    cm16_ref[...] = cm_ref[...].astype(BF16)


def _blockdiag(bb_r, bb_i, c_re, c_im):
    pair = lambda m: jnp.tile(m, (1, 1, 2)).reshape(N_GROUPS // 2, 2 * SSM_GROUP, LANE)
    octet = lambda c: jnp.tile(c.transpose(0, 2, 1), (1, 1, 8)).reshape(N_GROUPS // 8, 8 * SSM_STATE, LANE)
    bshape, cshape = (N_CHUNK, CH_IN, 2 * CH_ST), (N_CHUNK, 2 * CH_ST, CH_IN)
    return pl.pallas_call(
        _blockdiag_kernel,
        out_shape=(jax.ShapeDtypeStruct(bshape, F32), jax.ShapeDtypeStruct(bshape, BF16),
                   jax.ShapeDtypeStruct(cshape, F32), jax.ShapeDtypeStruct(cshape, BF16)),
        compiler_params=pltpu.CompilerParams(vmem_limit_bytes=VMEM_LIMIT), name="ssm_blockdiag",
    )(pair(bb_r), pair(bb_i), octet(c_re), octet(c_im))


def _load_weight_bf16(w_hbm, w16, stage, sem):
    n = w_hbm.shape[1] // CAST_COLS

    def chunk(c):
        return pltpu.make_async_copy(w_hbm.at[:, pl.ds(c * CAST_COLS, CAST_COLS)], stage.at[c % 2], sem.at[c % 2])

    chunk(0).start()
    for c in range(n):
        if c + 1 < n:
            chunk(c + 1).start()
        chunk(c).wait()
        w16[:, c * CAST_COLS:(c + 1) * CAST_COLS] = stage[c % 2].astype(BF16)


def _inproj_kernel(x_ref, g_ref, inv_ref, w_hbm, u_ref, q_ref, k_ref, v_ref, kn_ref, vn_ref, w_ref, stage, sem):
    kb = pl.program_id(1)

    @pl.when((pl.program_id(0) == 0) & (kb == 0))
    def _():
        _load_weight_bf16(w_hbm, w_ref, stage, sem)

    h = _rms(x_ref[0].reshape(TM, D_MODEL), g_ref[...]).astype(BF16)
    rid = lax.broadcasted_iota(I32, (TM, 1), 0)
    pos = ((rid >> TK_SHIFT) * SEG_LEN + kb * TK + (rid & (TK - 1))).astype(F32)
    cos, sin = _rope_tables(pos, inv_ref[...])
    w = SSM_WIDTH
    u = jnp.dot(h, w_ref[:, 0:w], preferred_element_type=F32)
    for s in range(N_SEG):
        for c in range(w // LANE):
            u_ref[0, c, pl.ds(s, TK, stride=N_SEG), :] = u[s * TK:(s + 1) * TK, c * LANE:(c + 1) * LANE]
    q_ref[0] = _rope_heads(jnp.dot(h, w_ref[:, w:2 * w], preferred_element_type=F32), cos, sin).reshape(N_SEG, TK, w)
    k = _rope_heads(jnp.dot(h, w_ref[:, 2 * w:3 * w], preferred_element_type=F32), cos, sin)
    v = jnp.dot(h, w_ref[:, 3 * w:4 * w], preferred_element_type=F32)
    k_ref[0] = k.reshape(N_SEG, TK, w)
    v_ref[0] = v.reshape(N_SEG, TK, w)
    kn_ref[0] = k[TM // 2:].reshape(N_SEG // 2, TK, w)
    vn_ref[0] = v[TM // 2:].reshape(N_SEG // 2, TK, w)


def _inproj(x4, g, inv, w):
    const = lambda b, k: (0, 0)
    blk4 = lambda nseg, width: pl.BlockSpec((1, nseg, TK, width), lambda b, k: (b, 0, k, 0))
    qkv = jax.ShapeDtypeStruct((BATCH, N_SEG, SEG_LEN, ATTN_WIDTH), F32)
    kvn = jax.ShapeDtypeStruct((BATCH, N_SEG // 2, SEG_LEN, ATTN_WIDTH), F32)
    return pl.pallas_call(
        _inproj_kernel, grid=(BATCH, SEG_LEN // TK),
        in_specs=[blk4(N_SEG, D_MODEL), pl.BlockSpec((1, D_MODEL), const),
                  pl.BlockSpec((1, HEAD_DIM), const), pl.BlockSpec(memory_space=pl.ANY)],
        out_specs=[pl.BlockSpec((1, SSM_WIDTH // LANE, TM, LANE), lambda b, k: (b, 0, k, 0)),
                   blk4(N_SEG, ATTN_WIDTH), blk4(N_SEG, ATTN_WIDTH), blk4(N_SEG, ATTN_WIDTH),
                   blk4(N_SEG // 2, ATTN_WIDTH), blk4(N_SEG // 2, ATTN_WIDTH)],
        out_shape=(jax.ShapeDtypeStruct((BATCH, SSM_WIDTH // LANE, SEQ, LANE), F32), qkv, qkv, qkv, kvn, kvn),
        scratch_shapes=[pltpu.VMEM((D_MODEL, D_IN), BF16), pltpu.VMEM((2, D_MODEL, CAST_COLS), F32),
                        pltpu.SemaphoreType.DMA((2,))],
        compiler_params=_cparams(("arbitrary",) * 2), name="inproj",
    )(x4, g, inv, w)


def _ssm_scan_kernel(*refs, final):
    if final:
        (u_ref, bm_ref, lr_ref, li_ref, e_ref, lsr_ref, lsi_ref, cm_ref, d_ref,
         z_ref, xend_ref, bu_scr, xr_scr, xi_scr) = refs
    else:
        u_ref, bm_ref, lr_ref, li_ref, xend_ref, bu_scr, xr_scr, xi_scr = refs
    kb = pl.program_id(2)

    @pl.when(kb == 0)
    def _():
        if final:
            lsr, lsi = lsr_ref[0:1, :], lsi_ref[0:1, :]
            seg_id = lax.broadcasted_iota(I32, (N_SEG, CH_ST), 0)
            pr = pi = jnp.zeros((1, CH_ST), F32)
            xr0 = xi0 = jnp.zeros((N_SEG, CH_ST), F32)
            for s in range(1, N_SEG):
                er = e_ref[0, 0, s - 1:s, 0:CH_ST]
                ei = e_ref[0, 0, s - 1:s, CH_ST:2 * CH_ST]
                pr, pi = er + lsr * pr - lsi * pi, ei + lsr * pi + lsi * pr
                xr0 = jnp.where(seg_id == s, pr, xr0)
                xi0 = jnp.where(seg_id == s, pi, xi0)
            xr_scr[...] = xr0
            xi_scr[...] = xi0
        else:
            xr_scr[...] = jnp.zeros((N_SEG, CH_ST), F32)
            xi_scr[...] = jnp.zeros((N_SEG, CH_ST), F32)

    u = jnp.concatenate([u_ref[0, c] for c in range(CH_IN // LANE)], axis=1)
    bu_scr[...] = jnp.dot(u.astype(BF16), bm_ref[0], preferred_element_type=F32)
    lr, li = lr_ref[...], li_ref[...]

    def step(k, carry):
        xr, xi = carry
        r0 = pl.multiple_of(k * N_SEG, N_SEG)
        bur = bu_scr[pl.ds(r0, N_SEG), 0:CH_ST]
        bui = bu_scr[pl.ds(r0, N_SEG), CH_ST:2 * CH_ST]
        nr = lr * xr - li * xi + bur
        ni = lr * xi + li * xr + bui
        if final:
            bu_scr[pl.ds(r0, N_SEG), 0:CH_ST] = nr
            bu_scr[pl.ds(r0, N_SEG), CH_ST:2 * CH_ST] = ni
        return nr, ni

    xr, xi = lax.fori_loop(0, SSM_TK, step, (xr_scr[...], xi_scr[...]), unroll=4)
    xr_scr[...] = xr
    xi_scr[...] = xi
    if final:
        y = jnp.dot(bu_scr[...].astype(BF16), cm_ref[0], preferred_element_type=F32)
        z = _gelu_tanh(y + d_ref[...] * u)
        for c in range(CH_IN // LANE):
            z_ref[0, c] = z[:, c * LANE:(c + 1) * LANE]

    @pl.when(kb == pl.num_programs(2) - 1)
    def _():
        xend_ref[0, 0] = jnp.concatenate([xr, xi], axis=1)


def _ssm_scan(u_perm, bm, lr8, li8, final, e=None, lsr8=None, lsi8=None, cm=None, dskip=None):
    rows = SSM_TK * N_SEG
    nkb = SEG_LEN // SSM_TK
    in_specs = [pl.BlockSpec((1, CH_IN // LANE, rows, LANE), lambda b, j, k: (b, j, k, 0)),
                pl.BlockSpec((1, CH_IN, 2 * CH_ST), lambda b, j, k: (j, 0, 0)),
                pl.BlockSpec((N_SEG, CH_ST), lambda b, j, k: (0, j)),
                pl.BlockSpec((N_SEG, CH_ST), lambda b, j, k: (0, j))]
    args = [u_perm, bm, lr8, li8]
    xend_spec = pl.BlockSpec((1, 1, N_SEG, 2 * CH_ST), lambda b, j, k: (b, j, 0, 0))
    xend_shape = jax.ShapeDtypeStruct((BATCH, N_CHUNK, N_SEG, 2 * CH_ST), F32)
    if final:
        in_specs += [xend_spec,
                     pl.BlockSpec((N_SEG, CH_ST), lambda b, j, k: (0, j)),
                     pl.BlockSpec((N_SEG, CH_ST), lambda b, j, k: (0, j)),
                     pl.BlockSpec((1, 2 * CH_ST, CH_IN), lambda b, j, k: (j, 0, 0)),
                     pl.BlockSpec((1, CH_IN), lambda b, j, k: (0, j))]
        args += [e, lsr8, lsi8, cm, dskip]
        out_specs = [pl.BlockSpec((1, CH_IN // LANE, rows, LANE), lambda b, j, k: (b, j, k, 0)), xend_spec]
        out_shape = (jax.ShapeDtypeStruct((BATCH, SSM_WIDTH // LANE, SEQ, LANE), F32), xend_shape)
    else:
        out_specs = xend_spec
        out_shape = xend_shape
    return pl.pallas_call(
        functools.partial(_ssm_scan_kernel, final=final), grid=(BATCH, N_CHUNK, nkb),
        in_specs=in_specs, out_specs=out_specs, out_shape=out_shape,
        scratch_shapes=[pltpu.VMEM((rows, 2 * CH_ST), F32),
                        pltpu.VMEM((N_SEG, CH_ST), F32), pltpu.VMEM((N_SEG, CH_ST), F32)],
        compiler_params=_cparams(("arbitrary",) * 3),
        name="ssm_scan_final" if final else "ssm_scan_ends",
    )(*args)


def _attn_kernel(q_ref, k_ref, v_ref, o_ref, o_scr, lse_scr, bias_scr):
    scale2 = HEAD_DIM ** -0.5 * LOG2_E
    qi = lax.broadcasted_iota(I32, (BAND, 2 * BAND), 0)
    kj = lax.broadcasted_iota(I32, (BAND, 2 * BAND), 1)
    cur_ok = (kj >= BAND) & (kj - BAND <= qi)
    prev_ok = (kj < BAND) & (kj >= qi)
    bias_scr[0] = jnp.where(cur_ok, 0.0, NEG)
    bias_scr[1] = jnp.where(cur_ok | prev_ok, 0.0, NEG)
    nt = (((1,), (1,)), ((), ()))
    for bi, (_, d) in enumerate(BRANCHES):
        nblk = SEQ // (BAND * d)
        shift = nblk.bit_length() - 1

        def tile(i, c, bi=bi, d=d, nblk=nblk, shift=shift):
            r = i >> shift
            ib = i & (nblk - 1)
            start = r + ib * (BAND * d)
            pstart = jnp.maximum(start - BAND * d, r)
            cur = pl.ds(start, BAND, stride=d)
            prev = pl.ds(pstart, BAND, stride=d)
            q = (q_ref[0, cur, :] * scale2).astype(BF16)
            k2 = jnp.concatenate([k_ref[0, prev, :], k_ref[0, cur, :]], axis=0).astype(BF16)
            v2 = jnp.concatenate([v_ref[0, prev, :], v_ref[0, cur, :]], axis=0).astype(BF16)
            s = lax.dot_general(q, k2, nt, preferred_element_type=F32) + bias_scr[jnp.minimum(ib, 1)]
            m = jnp.max(s, axis=-1, keepdims=True)
            p = jnp.exp2(s - m)
            l = jnp.sum(p, axis=-1, keepdims=True)
            o = jnp.dot(p.astype(BF16), v2, preferred_element_type=F32) * (1.0 / l)
            o_scr[bi, cur, :] = o
            lse_scr[bi, cur, :] = jnp.broadcast_to(m + jnp.log2(l), (BAND, HEAD_DIM))
            return c

        def softmax_pv(s, v, rows_out, bi=bi):
            m = jnp.max(s, axis=-1, keepdims=True)
            p = jnp.exp2(s - m)
            l = jnp.sum(p, axis=-1, keepdims=True)
            o_scr[bi, rows_out, :] = jnp.dot(p.astype(BF16), v, preferred_element_type=F32) * (1.0 / l)
            lse_scr[bi, rows_out, :] = jnp.broadcast_to(m + jnp.log2(l), (BAND, HEAD_DIM))

        def residue(r, c, d=d):
            sub = pl.ds(r, 2 * BAND, stride=d)
            q = (q_ref[0, sub, :] * scale2).astype(BF16)
            k = k_ref[0, sub, :].astype(BF16)
            v = v_ref[0, sub, :].astype(BF16)
            s0 = lax.dot_general(q[:BAND], k[:BAND], nt, preferred_element_type=F32) + bias_scr[1, :, BAND:]
            softmax_pv(s0, v[:BAND], pl.ds(r, BAND, stride=d))
            s1 = lax.dot_general(q[BAND:], k, nt, preferred_element_type=F32) + bias_scr[1]
            softmax_pv(s1, v, pl.ds(r + BAND * d, BAND, stride=d))
            return c

        if nblk == 2:
            lax.fori_loop(0, d, residue, 0, unroll=16)
        else:
            lax.fori_loop(0, SEQ // BAND, tile, 0, unroll=32)

    rows = 512

    def merge(i, c):
        sl = pl.ds(pl.multiple_of(i * rows, rows), rows)
        l0, l1, l2 = lse_scr[0, sl, :], lse_scr[1, sl, :], lse_scr[2, sl, :]
        m = jnp.maximum(jnp.maximum(l0, l1), l2)
        e0, e1, e2 = jnp.exp2(l0 - m), jnp.exp2(l1 - m), jnp.exp2(l2 - m)
        o_ref[0, sl, :] = ((e0 * o_scr[0, sl, :] + e1 * o_scr[1, sl, :] + e2 * o_scr[2, sl, :])
                           * (1.0 / (e0 + e1 + e2)))
        return c

    lax.fori_loop(0, SEQ // rows, merge, 0)


def _attention(q, k, v):
    spec = pl.BlockSpec((1, SEQ, HEAD_DIM), lambda b, h: (b, 0, h))
    return pl.pallas_call(
        _attn_kernel, grid=(BATCH, N_HEADS), in_specs=[spec, spec, spec], out_specs=spec,
        out_shape=jax.ShapeDtypeStruct((BATCH, SEQ, ATTN_WIDTH), F32),
        scratch_shapes=[pltpu.VMEM((3, SEQ, HEAD_DIM), F32), pltpu.VMEM((3, SEQ, HEAD_DIM), F32),
                        pltpu.VMEM((2, BAND, 2 * BAND), F32)],
        compiler_params=_cparams(("arbitrary",) * 2), name="dilated_attn",
    )(q, k, v)


def _outproj_kernel(x_ref, z_ref, o_ref, wglu_hbm, gs_ref, ga_ref, w_hbm, gf_ref, wr_ref, br_ref, xm_ref, lg_ref,
                    wglu_ref, w_ref, stage, sem):
    is_prompt = pl.program_id(0) < T_PROMPT // TM

    @pl.when(pl.program_id(0) == 0)
    def _():
        _load_weight_bf16(wglu_hbm, wglu_ref, stage.at[:, 0:SSM_WIDTH], sem)
        _load_weight_bf16(w_hbm, w_ref, stage, sem)

    @pl.when(is_prompt)
    def _():
        z = jnp.concatenate(
            [jnp.concatenate([z_ref[0, c, pl.ds(s, TK, stride=N_SEG), :] for c in range(SSM_WIDTH // LANE)], axis=1)
             for s in range(N_SEG)], axis=0)
        zz = z * _sigmoid(jnp.dot(z.astype(BF16), wglu_ref[...], preferred_element_type=F32))
        ms = _rms(zz, gs_ref[...]).astype(BF16)
        ma = _rms(o_ref[0].reshape(TM, ATTN_WIDTH), ga_ref[...]).astype(BF16)
        xm = (x_ref[0].reshape(TM, D_MODEL) + jnp.dot(ms, w_ref[0:SSM_WIDTH, :], preferred_element_type=F32)
              + jnp.dot(ma, w_ref[SSM_WIDTH:, :], preferred_element_type=F32))
        xm_ref[...] = xm
        h = _rms(xm, gf_ref[...]).astype(BF16)
        lg_ref[...] = jnp.dot(h, wr_ref[...], preferred_element_type=F32) + br_ref[...]

    @pl.when(jnp.logical_not(is_prompt))
    def _():
        xm_ref[...] = jnp.zeros((TM, D_MODEL), F32)
        lg_ref[...] = jnp.zeros((TM, 128), F32)


def _outproj(x4, z_slab, o_attn4, wglu, g_ssm, g_attn, w, g_ffn, wr_bf16, br):
    nkb = SEG_LEN // TK

    def split(i):
        i = jnp.minimum(i, T_PROMPT // TM - 1)
        return i // nkb, i % nkb

    def blk4(i):
        b, k = split(i)
        return (b, 0, k, 0)

    row = lambda i: (i, 0)
    const = lambda i: (0, 0)
    return pl.pallas_call(
        _outproj_kernel, grid=(N_TOK_BLOCKS,),
        in_specs=[pl.BlockSpec((1, N_SEG, TK, D_MODEL), blk4),
                  pl.BlockSpec((1, SSM_WIDTH // LANE, TM, LANE), blk4),
                  pl.BlockSpec((1, N_SEG, TK, ATTN_WIDTH), blk4),
                  pl.BlockSpec(memory_space=pl.ANY),
                  pl.BlockSpec((1, SSM_WIDTH), const),
                  pl.BlockSpec((1, ATTN_WIDTH), const),
                  pl.BlockSpec(memory_space=pl.ANY),
                  pl.BlockSpec((1, D_MODEL), const),
                  pl.BlockSpec((D_MODEL, 128), const),
                  pl.BlockSpec((1, 128), const)],
        out_specs=[pl.BlockSpec((TM, D_MODEL), row), pl.BlockSpec((TM, 128), row)],
        out_shape=(jax.ShapeDtypeStruct((T_PAD, D_MODEL), F32), jax.ShapeDtypeStruct((T_PAD, 128), F32)),
        scratch_shapes=[pltpu.VMEM((SSM_WIDTH, SSM_WIDTH), BF16), pltpu.VMEM((D_MODEL, D_MODEL), BF16),
                        pltpu.VMEM((2, D_MODEL, CAST_COLS), F32), pltpu.SemaphoreType.DMA((2,))],
        compiler_params=_cparams(("arbitrary",)), name="outproj",
    )(x4, z_slab, o_attn4, wglu, g_ssm, g_attn, w, g_ffn, wr_bf16, br)


def _s_inproj_kernel(x_ref, g_ref, w_ref, o_ref):
    h = _rms(x_ref[...], g_ref[...])
    o_ref[...] = jnp.dot(h, w_ref[...], preferred_element_type=F32, precision=HIGHEST)


def _s_inproj(x, g, w):
    tn = 1024
    return pl.pallas_call(
        _s_inproj_kernel, grid=(D_IN // tn,),
        in_specs=[pl.BlockSpec((DEC_BATCH, D_MODEL), lambda j: (0, 0)),
                  pl.BlockSpec((1, D_MODEL), lambda j: (0, 0)),
                  pl.BlockSpec((D_MODEL, tn), lambda j: (0, j))],
        out_specs=pl.BlockSpec((DEC_BATCH, tn), lambda j: (0, j)),
        out_shape=jax.ShapeDtypeStruct((DEC_BATCH, D_IN), F32),
        compiler_params=_cparams(("arbitrary",)), name="s_inproj",
    )(x, g, w)


def _s_ssm_kernel(u_ref, sr_ref, si_ref, lr_ref, li_ref, bm_ref, cm_ref, d_ref, z_ref, nr_ref, ni_ref):
    u = u_ref[...]
    bu = jnp.dot(u, bm_ref[0], preferred_element_type=F32, precision=HIGHEST)
    lr, li = lr_ref[...], li_ref[...]
    xr, xi = sr_ref[...], si_ref[...]
    nr = lr * xr - li * xi + bu[:, :CH_ST]
    ni = lr * xi + li * xr + bu[:, CH_ST:]
    nr_ref[...] = nr
    ni_ref[...] = ni
    y = jnp.dot(jnp.concatenate([nr, ni], axis=1), cm_ref[0], preferred_element_type=F32, precision=HIGHEST)
    z_ref[...] = _gelu_tanh(y + d_ref[...] * u)


def _s_ssm(u, sr, si, lr1, li1, bm, cm, dskip):
    st = jax.ShapeDtypeStruct((DEC_BATCH, N_GROUPS * SSM_STATE), F32)
    return pl.pallas_call(
        _s_ssm_kernel, grid=(N_CHUNK,),
        in_specs=[pl.BlockSpec((DEC_BATCH, CH_IN), lambda j: (0, j)),
                  pl.BlockSpec((DEC_BATCH, CH_ST), lambda j: (0, j)),
                  pl.BlockSpec((DEC_BATCH, CH_ST), lambda j: (0, j)),
                  pl.BlockSpec((1, CH_ST), lambda j: (0, j)),
                  pl.BlockSpec((1, CH_ST), lambda j: (0, j)),
                  pl.BlockSpec((1, CH_IN, 2 * CH_ST), lambda j: (j, 0, 0)),
                  pl.BlockSpec((1, 2 * CH_ST, CH_IN), lambda j: (j, 0, 0)),
                  pl.BlockSpec((1, CH_IN), lambda j: (0, j))],
        out_specs=[pl.BlockSpec((DEC_BATCH, CH_IN), lambda j: (0, j)),
                   pl.BlockSpec((DEC_BATCH, CH_ST), lambda j: (0, j)),
                   pl.BlockSpec((DEC_BATCH, CH_ST), lambda j: (0, j))],
        out_shape=(jax.ShapeDtypeStruct((DEC_BATCH, SSM_WIDTH), F32), st, st),
        compiler_params=_cparams(("arbitrary",)), name="s_ssm",
    )(u, sr, si, lr1, li1, bm, cm, dskip)


def _s_glu_kernel(z_ref, w_ref, g_ref, o_ref):
    z = z_ref[...]
    zz = z * _sigmoid(jnp.dot(z, w_ref[...], preferred_element_type=F32, precision=HIGHEST))
    o_ref[...] = _rms(zz, g_ref[...])


def _s_glu(z, w, g):
    return pl.pallas_call(
        _s_glu_kernel, out_shape=jax.ShapeDtypeStruct((DEC_BATCH, SSM_WIDTH), F32),
        compiler_params=pltpu.CompilerParams(vmem_limit_bytes=VMEM_LIMIT), name="s_glu",
    )(z, w, g)


def _s_attn_kernel(q_ref, k_ref, v_ref, inv_ref, k1_ref, v1_ref, k4_ref, v4_ref, k16_ref, v16_ref,
                   o_ref, kr_ref):
    scale = HEAD_DIM ** -0.5
    pos = jnp.full((N_HEADS, 1), float(PAST_LEN), F32)
    cos, sin = _rope_tables(pos, inv_ref[...])
    q = q_ref[...] * cos + pltpu.roll(q_ref[...], HEAD_DIM // 2, 1) * sin
    kn = k_ref[...] * cos + pltpu.roll(k_ref[...], HEAD_DIM // 2, 1) * sin
    vn = v_ref[...]
    kr_ref[...] = kn
    s0 = jnp.sum(q * kn, axis=-1, keepdims=True) * scale
    outs, lses = [], []
    for kc_ref, vc_ref in ((k1_ref, v1_ref), (k4_ref, v4_ref), (k16_ref, v16_ref)):
        s = jnp.sum(kc_ref[...] * q[None], axis=-1, keepdims=True) * scale
        m = jnp.maximum(jnp.max(s, axis=0), s0)
        p = jnp.exp(s - m[None])
        p0 = jnp.exp(s0 - m)
        l = jnp.sum(p, axis=0) + p0
        outs.append((jnp.sum(p * vc_ref[...], axis=0) + p0 * vn) / l)
        lses.append(m + jnp.log(l))
    mm = jnp.maximum(jnp.maximum(lses[0], lses[1]), lses[2])
    es = [jnp.exp(x - mm) for x in lses]
    o_ref[...] = (es[0] * outs[0] + es[1] * outs[1] + es[2] * outs[2]) / (es[0] + es[1] + es[2])


def _s_attention(q, k, v, inv, cache_k, cache_v):
    hd = (N_HEADS, HEAD_DIM)
    tok = pl.BlockSpec((None,) + hd, lambda b: (b, 0, 0))
    args, specs = [], []
    for _, d in BRANCHES:
        nrow = CACHE_LEN // d
        last = nrow // BAND - 1
        spec = pl.BlockSpec((None, BAND, None) + hd, lambda b, last=last: (b, last, 0, 0, 0))
        for c in (cache_k, cache_v):
            args.append(c.reshape(DEC_BATCH, nrow, d, *hd))
            specs.append(spec)
    out = jax.ShapeDtypeStruct((DEC_BATCH,) + hd, F32)
    return pl.pallas_call(
        _s_attn_kernel, grid=(DEC_BATCH,),
        in_specs=[tok, tok, tok, pl.BlockSpec((1, HEAD_DIM), lambda b: (0, 0))] + specs,
        out_specs=[tok, tok], out_shape=(out, out),
        compiler_params=_cparams(("arbitrary",)), name="s_attn",
    )(q, k, v, inv, *args)


def _s_outproj_kernel(xa_ref, lga_ref, x_ref, ms_ref, o_ref, ga_ref, w_ref, gf_ref, wr_ref, br_ref,
                      xm_ref, lg_ref):
    del xa_ref, lga_ref
    mix = jnp.concatenate([ms_ref[...], _rms(o_ref[...], ga_ref[...])], axis=1)
    xm = x_ref[...] + jnp.dot(mix, w_ref[...], preferred_element_type=F32, precision=HIGHEST)
    h = _rms(xm, gf_ref[...])
    lg = jnp.dot(h, wr_ref[...], preferred_element_type=F32, precision=HIGHEST) + br_ref[...]
    pad = TM - DEC_BATCH
    xm_ref[...] = jnp.concatenate([xm, jnp.zeros((pad, D_MODEL), F32)], axis=0)
    lg_ref[...] = jnp.concatenate([lg, jnp.zeros((pad, 128), F32)], axis=0)


def _s_outproj(xm_all, lg_all, x, mix_ssm, o_attn, g_attn, w, g_ffn, wr, br):
    last = lambda i: (N_TOK_BLOCKS - 1, 0)
    full = lambda shape: pl.BlockSpec(shape, lambda i: (0, 0))
    any_spec = pl.BlockSpec(memory_space=pl.ANY)
    return pl.pallas_call(
        _s_outproj_kernel, grid=(1,),
        in_specs=[any_spec, any_spec,
                  full((DEC_BATCH, D_MODEL)), full((DEC_BATCH, SSM_WIDTH)), full((DEC_BATCH, ATTN_WIDTH)),
                  full((1, ATTN_WIDTH)), full((D_MODEL, D_MODEL)), full((1, D_MODEL)),
                  full((D_MODEL, 128)), full((1, 128))],
        out_specs=[pl.BlockSpec((TM, D_MODEL), last), pl.BlockSpec((TM, 128), last)],
        out_shape=(jax.ShapeDtypeStruct(xm_all.shape, F32), jax.ShapeDtypeStruct(lg_all.shape, F32)),
        input_output_aliases={0: 0, 1: 1},
        compiler_params=_cparams(("arbitrary",)), name="s_outproj",
    )(xm_all, lg_all, x, mix_ssm, o_attn, g_attn, w, g_ffn, wr, br)


def _route_kernel(lg_ref, ri_ref, rw_ref, cnt_ref, carry):
    i = pl.program_id(0)

    @pl.when(i == 0)
    def _():
        carry[...] = jnp.zeros((1, 128), F32)

    x = lg_ref[...]
    lane = lax.broadcasted_iota(I32, (ROUTE_ROWS, 128), 1)
    row = lax.broadcasted_iota(I32, (ROUTE_ROWS, 128), 0) + i * ROUTE_ROWS
    valid = row < T_REAL
    big = jnp.int32(1 << 20)
    gmask = lane < N_EXPERT_GROUPS
    lgm = jnp.where(gmask, x, NEG)
    m = jnp.max(lgm, axis=-1, keepdims=True)
    gate1 = 1.0 / jnp.sum(jnp.where(gmask, jnp.exp(lgm - m), 0.0), axis=-1, keepdims=True)
    grp = jnp.min(jnp.where(gmask & (lgm == m), lane, big), axis=-1, keepdims=True)
    lo = N_EXPERT_GROUPS + EXPERTS_PER_GROUP * grp
    emask = (lane >= lo) & (lane < lo + EXPERTS_PER_GROUP)
    le = jnp.where(emask, x, NEG)
    t1 = jnp.max(le, axis=-1, keepdims=True)
    i1 = jnp.min(jnp.where(emask & (le == t1), lane, big), axis=-1, keepdims=True)
    emask2 = emask & (lane != i1)
    le2 = jnp.where(emask2, x, NEG)
    t2 = jnp.max(le2, axis=-1, keepdims=True)
    i2 = jnp.min(jnp.where(emask2 & (le2 == t2), lane, big), axis=-1, keepdims=True)
    e21 = jnp.exp(t2 - t1)
    w1 = gate1 / (1.0 + e21)
    w2 = gate1 * e21 / (1.0 + e21)
    eid1, eid2 = i1 - N_EXPERT_GROUPS, i2 - N_EXPERT_GROUPS
    oh1 = jnp.where(valid & (lane == eid1), 1.0, 0.0)
    oh2 = jnp.where(valid & (lane == eid2), 1.0, 0.0)
    a = oh1 + oh2
    rr = lax.broadcasted_iota(I32, (ROUTE_ROWS, ROUTE_ROWS), 0)
    cc = lax.broadcasted_iota(I32, (ROUTE_ROWS, ROUTE_ROWS), 1)
    before = jnp.where(cc < rr, 1.0, 0.0).astype(BF16)
    pre = jnp.dot(before, a.astype(BF16), preferred_element_type=F32) + carry[...]
    rank1 = jnp.sum(oh1 * pre, axis=-1, keepdims=True).astype(I32)
    rank2 = jnp.sum(oh2 * pre, axis=-1, keepdims=True).astype(I32)
    carry[...] = carry[...] + jnp.sum(a, axis=0, keepdims=True)
    cnt_ref[...] = jnp.broadcast_to(carry[...], (8, 128))
    zi = jnp.zeros((ROUTE_ROWS, 128), I32)
    ri = jnp.where(lane == 0, eid1, jnp.where(lane == 1, eid2, jnp.where(lane == 2, rank1, jnp.where(lane == 3, rank2, zi))))
    ri_ref[...] = jnp.where(valid, ri, zi)
    rw = jnp.where(lane == 0, w1, jnp.where(lane == 1, w2, 0.0))
    rw_ref[...] = jnp.where(valid, rw, 0.0)


def _route(logits):
    blk = pl.BlockSpec((ROUTE_ROWS, 128), lambda i: (i, 0))
    return pl.pallas_call(
        _route_kernel, grid=(T_PAD // ROUTE_ROWS,), in_specs=[blk],
        out_specs=[blk, blk, pl.BlockSpec((8, 128), lambda i: (0, 0))],
        out_shape=(jax.ShapeDtypeStruct((T_PAD, 128), I32), jax.ShapeDtypeStruct((T_PAD, 128), F32),
                   jax.ShapeDtypeStruct((8, 128), F32)),
        scratch_shapes=[pltpu.VMEM((1, 128), F32)],
        compiler_params=_cparams(("arbitrary",)), name="route",
    )(logits)


def _row_copy(src, s, dst, d, sem):
    return pltpu.make_async_copy(src.at[pl.ds(s, 1)], dst.at[pl.ds(d, 1)], sem)


def _invert_kernel(poff_ref, ri_ref, dest_ref, rt_ref):
    i = pl.program_id(0)

    @pl.when(i == 0)
    def _():
        def clear(p, c):
            rt_ref[p] = 0
            return c
        lax.fori_loop(0, N_MOE_ROWS, clear, 0, unroll=32)

    def put(r, c):
        t = i * TM + r
        for s in range(2):
            d = poff_ref[ri_ref[0, 0, 4 * r + s]] + ri_ref[0, 0, 4 * r + 2 + s]
            dest_ref[0, 0, 2 * r + s] = d
            rt_ref[d] = t
        return c

    def blank(r, c):
        dest_ref[0, 0, 2 * r] = 0
        dest_ref[0, 0, 2 * r + 1] = 0
        return c

    @pl.when(i < T_PROMPT // TM)
    def _():
        lax.fori_loop(0, TM, put, 0, unroll=8)

    @pl.when(i == T_PROMPT // TM)
    def _():
        lax.fori_loop(0, DEC_BATCH, put, 0, unroll=8)
        lax.fori_loop(DEC_BATCH, TM, blank, 0, unroll=8)


def _invert(poff, ri4):
    grid_spec = pltpu.PrefetchScalarGridSpec(
        num_scalar_prefetch=1, grid=(N_TOK_BLOCKS,),
        in_specs=[pl.BlockSpec((1, 1, 4 * TM), lambda i, po: (i, 0, 0), memory_space=pltpu.SMEM)],
        out_specs=[pl.BlockSpec((1, 1, 2 * TM), lambda i, po: (i, 0, 0), memory_space=pltpu.SMEM),
                   pl.BlockSpec(memory_space=pltpu.SMEM)])
    return pl.pallas_call(
        _invert_kernel, grid_spec=grid_spec,
        out_shape=(jax.ShapeDtypeStruct((N_TOK_BLOCKS, 1, 2 * TM), I32),
                   jax.ShapeDtypeStruct((N_MOE_ROWS,), I32)),
        compiler_params=_cparams(("arbitrary",)), name="moe_invert",
    )(poff, ri4)


def _expert_kernel(fb_ref, nv_ref, rt_ref, xm_ref, gf_ref, wg_ref, wu_ref, wd_ref, y_ref,
                   xbuf, ybuf, gsem, ysem, wgu_scr, wd_scr):
    e = pl.program_id(0)
    first, end, total = fb_ref[e], fb_ref[e + 1], fb_ref[N_EXPERTS]

    def y_copy(g, slot):
        return pltpu.make_async_copy(ybuf.at[slot], y_ref.at[pl.ds(g * MOE_BLK, MOE_BLK)], ysem.at[slot])

    def gather(g, slot, wait):
        def body(r, c):
            if wait:
                _row_copy(xm_ref, 0, xbuf.at[slot], 0, gsem.at[slot]).wait()
            else:
                _row_copy(xm_ref, rt_ref[g * MOE_BLK + r], xbuf.at[slot], r,
                          gsem.at[slot]).start(priority=GATHER_PRIORITY)
            return c
        lax.fori_loop(0, nv_ref[g], body, 0)

    @pl.when(e == 0)
    def _():
        xbuf[...] = jnp.zeros(xbuf.shape, F32)
        for d in range(GATHER_DEPTH - 1):
            gather(d, d, wait=False)

    @pl.when(end > first)
    def _():
        wgu_scr[:, 0:EXPERT_FF] = wg_ref[0].astype(BF16)
        wgu_scr[:, EXPERT_FF:] = wu_ref[0].astype(BF16)
        wd_scr[...] = wd_ref[0].astype(BF16)

    def block(g, carry):
        slot = g % 2
        xslot = g % GATHER_DEPTH

        gather(g + GATHER_DEPTH - 1, (g + GATHER_DEPTH - 1) % GATHER_DEPTH, wait=False)
        gather(g, xslot, wait=True)
        h = _rms(xbuf[xslot], gf_ref[...]).astype(BF16)
        gu = jnp.dot(h, wgu_scr[...], preferred_element_type=F32)
        gate, up = gu[:, :EXPERT_FF], gu[:, EXPERT_FF:]
        act = (gate * _sigmoid(gate) * up).astype(BF16)
        y = jnp.dot(act, wd_scr[...], preferred_element_type=F32)

        @pl.when(g >= 2)
        def _():
            y_copy(g - 2, slot).wait()
        ybuf[slot] = y
        y_copy(g, slot).start()
        return carry

    lax.fori_loop(first, end, block, 0)

    @pl.when(e == N_EXPERTS - 1)
    def _():
        @pl.when(total >= 2)
        def _():
            y_copy(total - 2, total % 2).wait()

        @pl.when(total >= 1)
        def _():
            y_copy(total - 1, (total - 1) % 2).wait()

        ybuf[0] = jnp.zeros((MOE_BLK, D_MODEL), F32)

        def zero_start(g, c):
            y_copy(g, 0).start()
            return c

        def zero_wait(g, c):
            y_copy(g, 0).wait()
            return c

        lax.fori_loop(total, N_MOE_BLOCKS, zero_start, 0)
        lax.fori_loop(total, N_MOE_BLOCKS, zero_wait, 0)


def _experts(first_blk, nvalid, row_tok, xm_all, g_ffn, w_g, w_u, w_d):
    grid_spec = pltpu.PrefetchScalarGridSpec(
        num_scalar_prefetch=3, grid=(N_EXPERTS,),
        in_specs=[pl.BlockSpec(memory_space=pl.ANY),
                  pl.BlockSpec((1, D_MODEL), lambda e, fb, nv, rt: (0, 0)),
                  pl.BlockSpec((1, D_MODEL, EXPERT_FF), lambda e, fb, nv, rt: (e, 0, 0)),
                  pl.BlockSpec((1, D_MODEL, EXPERT_FF), lambda e, fb, nv, rt: (e, 0, 0)),
                  pl.BlockSpec((1, EXPERT_FF, D_MODEL), lambda e, fb, nv, rt: (e, 0, 0))],
        out_specs=pl.BlockSpec(memory_space=pl.ANY),
        scratch_shapes=[pltpu.VMEM((GATHER_DEPTH, MOE_BLK, D_MODEL), F32), pltpu.VMEM((2, MOE_BLK, D_MODEL), F32),
                        pltpu.SemaphoreType.DMA((GATHER_DEPTH,)), pltpu.SemaphoreType.DMA((2,)),
                        pltpu.VMEM((D_MODEL, 2 * EXPERT_FF), BF16), pltpu.VMEM((EXPERT_FF, D_MODEL), BF16)])
    return pl.pallas_call(
        _expert_kernel, grid_spec=grid_spec,
        out_shape=jax.ShapeDtypeStruct((N_MOE_ROWS, D_MODEL), F32),
        compiler_params=_cparams(("arbitrary",)), name="moe_experts",
    )(first_blk, nvalid, row_tok, xm_all, g_ffn, w_g, w_u, w_d)


def _combine_kernel(dest_ref, destn_ref, rw_ref, xm_ref, g_ref, y_ref, o_ref, ybuf, sems, *, rows, nblk):
    i = pl.program_id(0)
    slot = i % 2

    def gather(d_ref, s, wait):
        def body(r, c):
            for j in range(2):
                cp = _row_copy(y_ref, 0 if wait else d_ref[0, 0, 2 * r + j], ybuf.at[s, j], r, sems.at[s])
                cp.wait() if wait else cp.start(priority=j)
            return c
        lax.fori_loop(0, rows, body, 0, unroll=8)

    @pl.when(i == 0)
    def _():
        gather(dest_ref, 0, wait=False)

    @pl.when(i + 1 < nblk)
    def _():
        gather(destn_ref, 1 - slot, wait=False)

    gather(dest_ref, slot, wait=True)
    w = rw_ref[...]
    x = xm_ref[...] + w[:, 0:1] * ybuf[slot, 0] + w[:, 1:2] * ybuf[slot, 1]
    o_ref[...] = _rms(x, g_ref[...]).reshape(o_ref.shape)


def _combine(dest3, rw, xm, g, y_rows, rows, nblk, blk0, out_spec, out_shape):
    smem_blk = lambda f: pl.BlockSpec((1, 1, 2 * rows), f, memory_space=pltpu.SMEM)
    return pl.pallas_call(
        functools.partial(_combine_kernel, rows=rows, nblk=nblk), grid=(nblk,),
        in_specs=[smem_blk(lambda i: (i, 0, 0)),
                  smem_blk(lambda i: (jnp.minimum(i + 1, nblk - 1), 0, 0)),
                  pl.BlockSpec((rows, 128), lambda i: (blk0 + i, 0)),
                  pl.BlockSpec((rows, D_MODEL), lambda i: (blk0 + i, 0)),
                  pl.BlockSpec((1, D_MODEL), lambda i: (0, 0)),
                  pl.BlockSpec(memory_space=pl.ANY)],
        out_specs=out_spec, out_shape=out_shape,
        scratch_shapes=[pltpu.VMEM((2, 2, rows, D_MODEL), F32), pltpu.SemaphoreType.DMA((2,))],
        compiler_params=_cparams(("arbitrary",)), name="moe_combine",
    )(dest3, dest3, rw, xm, g, y_rows)


def kernel(x_prompt, x_sample, cache_k, cache_v, state_ssm_re, state_ssm_im, g_mix_norm, w_in, ssm_a_re, ssm_a_im, ssm_log_dt, ssm_b_re, ssm_b_im, ssm_c_re, ssm_c_im, ssm_d, w_glu, g_ssm_out, g_attn_out, w_out, g_ffn_norm, w_router_group, b_router_group, w_router_expert, b_router_expert, w_expert_gate, w_expert_up, w_expert_down, g_final):
    row = lambda a: a.reshape(1, -1)
    half = HEAD_DIM // 2
    inv = ROPE_THETA ** (-jnp.arange(half, dtype=F32) / half)
    inv = jnp.concatenate([inv, inv]).reshape(1, HEAD_DIM)

    lam_r, lam_i, lams_r, lams_i, bb_r, bb_i = _ssm_params(ssm_a_re[0], ssm_a_im[0], ssm_log_dt[0], ssm_b_re[0], ssm_b_im[0])
    bm, bm16, cm, cm16 = _blockdiag(bb_r, bb_i, ssm_c_re[0], ssm_c_im[0])
    dskip = row(ssm_d[0])
    bcast8 = lambda a: jnp.broadcast_to(a.reshape(1, -1), (N_SEG, N_GROUPS * SSM_STATE))
    lr8, li8, lsr8, lsi8 = bcast8(lam_r), bcast8(lam_i), bcast8(lams_r), bcast8(lams_i)

    wr = jnp.concatenate([w_router_group[0], w_router_expert[0].reshape(D_MODEL, N_EXPERTS)], axis=1)
    wr = jnp.pad(wr, ((0, 0), (0, 128 - wr.shape[1])))
    br = jnp.pad(jnp.concatenate([b_router_group[0], b_router_expert[0].reshape(-1)]), (0, 128 - 36)).reshape(1, 128)

    x4 = x_prompt.reshape(BATCH, N_SEG, SEG_LEN, D_MODEL)
    u_il, q4, k4, v4, k_new, v_new = _inproj(x4, row(g_mix_norm[0]), inv, w_in[0])
    ends = _ssm_scan(u_il, bm16, lr8, li8, final=False)
    z_il, xend = _ssm_scan(u_il, bm16, lr8, li8, final=True, e=ends, lsr8=lsr8, lsi8=lsi8, cm=cm16, dskip=dskip)
    flat = lambda a: a.reshape(BATCH, SEQ, ATTN_WIDTH)
    o_attn = _attention(flat(q4), flat(k4), flat(v4))
    xm_all, lg_all = _outproj(
        x4, z_il, o_attn.reshape(BATCH, N_SEG, SEG_LEN, ATTN_WIDTH), w_glu[0], row(g_ssm_out[0]),
        row(g_attn_out[0]), w_out[0], row(g_ffn_norm[0]), wr.astype(BF16), br)

    xs = x_sample.reshape(DEC_BATCH, D_MODEL)
    proj = _s_inproj(xs, row(g_mix_norm[0]), w_in[0])
    us = proj[:, :SSM_WIDTH]
    qkv = proj[:, SSM_WIDTH:].reshape(DEC_BATCH, 3, N_HEADS, HEAD_DIM)
    zs, ns_r, ns_i = _s_ssm(us, state_ssm_re[0].reshape(DEC_BATCH, -1), state_ssm_im[0].reshape(DEC_BATCH, -1),
                            row(lam_r), row(lam_i), bm, cm, dskip)
    mix_ssm_s = _s_glu(zs, w_glu[0], row(g_ssm_out[0]))
    o_s, k_rot = _s_attention(qkv[:, 0], qkv[:, 1], qkv[:, 2], inv, cache_k[0], cache_v[0])
    xm_all, lg_all = _s_outproj(
        xm_all, lg_all, xs, mix_ssm_s, o_s.reshape(DEC_BATCH, ATTN_WIDTH), row(g_attn_out[0]),
        w_out[0], row(g_ffn_norm[0]), wr, br)

    ri, rw, cnt = _route(lg_all)
    counts = cnt[0, :N_EXPERTS].astype(I32)
    padded = (counts + MOE_BLK - 1) // MOE_BLK * MOE_BLK
    pend = jnp.cumsum(padded)
    poff = pend - padded
    first_blk = jnp.concatenate([poff, pend[-1:]]) // MOE_BLK
    blk_start = jnp.arange(N_MOE_BLOCKS + GATHER_DEPTH, dtype=I32) * MOE_BLK
    block_e = jnp.minimum(jnp.sum(pend[None, :] <= blk_start[:, None], axis=1), N_EXPERTS - 1)
    nvalid = jnp.clip(counts[block_e] - (blk_start - poff[block_e]), 0, MOE_BLK).astype(I32)
    dest3, row_tok = _invert(poff, ri[:, 0:4].reshape(N_TOK_BLOCKS, 1, 4 * TM))
    y_rows = _experts(first_blk, nvalid, row_tok, xm_all, row(g_ffn_norm[0]),
                      w_expert_gate[0], w_expert_up[0], w_expert_down[0])
    gfin = row(g_final)
    nkb = SEG_LEN // TK
    y_prompt = _combine(dest3, rw, xm_all, gfin, y_rows, TM, T_PROMPT // TM, 0,
                        pl.BlockSpec((1, N_SEG, TK, D_MODEL), lambda i: (i // nkb, 0, i % nkb, 0)),
                        jax.ShapeDtypeStruct((BATCH, N_SEG, SEG_LEN, D_MODEL), F32))
    dest_s = dest3[N_TOK_BLOCKS - 1:, :, :2 * DEC_BATCH]
    y_sample = _combine(dest_s, rw, xm_all, gfin, y_rows, DEC_BATCH, 1, T_PROMPT // DEC_BATCH,
                        pl.BlockSpec((DEC_BATCH, D_MODEL), lambda i: (0, 0)),
                        jax.ShapeDtypeStruct((DEC_BATCH, D_MODEL), F32))

    kv_shape = (1, BATCH, CACHE_LEN, N_HEADS, HEAD_DIM)
    st_p = lambda a: a[:, :, N_SEG - 1, :].reshape(1, BATCH, N_GROUPS, SSM_STATE)
    st_s = lambda a: a.reshape(1, DEC_BATCH, N_GROUPS, SSM_STATE)
    kvs = lambda a: a.reshape(1, DEC_BATCH, 1, N_HEADS, HEAD_DIM)
    return (y_prompt.reshape(BATCH, SEQ, D_MODEL), y_sample.reshape(DEC_BATCH, 1, D_MODEL),
            k_new.reshape(kv_shape), v_new.reshape(kv_shape),
            st_p(xend[..., :CH_ST]), st_p(xend[..., CH_ST:]),
            kvs(k_rot), kvs(qkv[:, 2]), st_s(ns_r), st_s(ns_i))
```

```python
import functools
import math

import jax
import jax.numpy as jnp
from jax import lax
from jax.experimental import pallas as pl
from jax.experimental.pallas import tpu as pltpu

F32 = jnp.float32
BF16 = jnp.bfloat16
I32 = jnp.int32
U32 = jnp.uint32
HIGHEST = lax.Precision.HIGHEST

D_MODEL = 2048
BATCH = 2
SEQ = 4096
DEC_BATCH = 32
PAST_LEN = 8192
CACHE_LEN = 2048
SSM_WIDTH = 1024
ATTN_WIDTH = 1024
SSM_GROUP = 16
N_GROUPS = 64
SSM_STATE = 64
HEAD_DIM = 128
N_HEADS = 8
D_IN = 4096
BRANCHES = ((128, 1), (512, 4), (2048, 16))
BAND = 128
ROPE_THETA = 10000.0
N_EXPERT_GROUPS = 4
EXPERTS_PER_GROUP = 8
N_EXPERTS = 32
EXPERT_FF = 512
NORM_EPS = 1e-6

N_SEG = 8
SEG_LEN = SEQ // N_SEG
N_CHUNK = 4
CH_IN = SSM_WIDTH // N_CHUNK
CH_ST = N_GROUPS * SSM_STATE // N_CHUNK
SSM_TK = 256
TM = 256
TK = TM // N_SEG
TK_SHIFT = TK.bit_length() - 1
LANE = 128
T_PROMPT = BATCH * SEQ
T_REAL = T_PROMPT + DEC_BATCH
T_PAD = T_PROMPT + TM
N_TOK_BLOCKS = T_PAD // TM
ROUTE_ROWS = 3 * TM
MOE_BLK = 256
N_MOE_BLOCKS = -(-(2 * T_REAL) // MOE_BLK) + N_EXPERTS
N_MOE_ROWS = N_MOE_BLOCKS * MOE_BLK
VMEM_LIMIT = 56 * 1024 * 1024
NEG = -1e30
LOG2_E = 1.4426950408889634
CAST_COLS = 512
GATHER_PRIORITY = 1
GATHER_DEPTH = 2


def _cparams(sem, vmem=VMEM_LIMIT):
    return pltpu.CompilerParams(dimension_semantics=sem, vmem_limit_bytes=vmem)


def _rms(x, g):
    return x * lax.rsqrt(jnp.mean(x * x, axis=-1, keepdims=True) + NORM_EPS) * g


def _gelu_tanh(y):
    return 0.5 * y * (1.0 + jnp.tanh(0.7978845608028654 * (y + 0.044715 * (y * y * y))))


def _sigmoid(x):
    return 1.0 / (1.0 + jnp.exp(-x))


def _rope_tables(pos, inv):
    ang = pos * inv
    lane = lax.broadcasted_iota(I32, ang.shape, 1)
    return jnp.cos(ang), jnp.where(lane < HEAD_DIM // 2, -jnp.sin(ang), jnp.sin(ang))


def _rope_heads(x, cos, sin):
    outs = []
    for h in range(x.shape[1] // HEAD_DIM):
        xh = x[:, h * HEAD_DIM:(h + 1) * HEAD_DIM]
        outs.append(xh * cos + pltpu.roll(xh, HEAD_DIM // 2, 1) * sin)
    return jnp.concatenate(outs, axis=1)


def _ssm_param_kernel(are_ref, aim_ref, ldt_ref, btr_ref, bti_ref,
                      lr_ref, li_ref, lsr_ref, lsi_ref, bbr_ref, bbi_ref):
    ar, ai = are_ref[...], aim_ref[...]
    dt = jnp.exp(ldt_ref[...])
    er, ei = ar * dt, ai * dt
    mag = jnp.exp(er)
    lr, li = mag * jnp.cos(ei), mag * jnp.sin(ei)
    lr_ref[...] = lr
    li_ref[...] = li
    mag_s = jnp.exp(er * SEG_LEN)
    lsr_ref[...] = mag_s * jnp.cos(ei * SEG_LEN)
    lsi_ref[...] = mag_s * jnp.sin(ei * SEG_LEN)
    xr, xi = lr - 1.0, li
    den = ar * ar + ai * ai
    cr = (xr * ar + xi * ai) / den
    ci = (xi * ar - xr * ai) / den
    btr, bti = btr_ref[...], bti_ref[...]
    bbr_ref[...] = cr[:, None, :] * btr - ci[:, None, :] * bti
    bbi_ref[...] = cr[:, None, :] * bti + ci[:, None, :] * btr


def _ssm_params(a_re, a_im, log_dt, b_re, b_im):
    g, n, p = N_GROUPS, SSM_STATE, SSM_GROUP
    gn = jax.ShapeDtypeStruct((g, n), F32)
    gpn = jax.ShapeDtypeStruct((g, p, n), F32)
    return pl.pallas_call(
        _ssm_param_kernel, out_shape=(gn, gn, gn, gn, gpn, gpn), name="ssm_params",
    )(a_re, a_im, log_dt.reshape(g, 1), b_re.transpose(0, 2, 1), b_im.transpose(0, 2, 1))


def _blockdiag_kernel(br_ref, bi_ref, cr_ref, ci_ref, bm_ref, bm16_ref, cm_ref, cm16_ref):
    bm_ref[...] = jnp.zeros(bm_ref.shape, F32)
    cm_ref[...] = jnp.zeros(cm_ref.shape, F32)
    gp, gl = 2 * SSM_GROUP, LANE
    keep_b = ((lax.broadcasted_iota(I32, (gp, gl), 0) // SSM_GROUP)
              == (lax.broadcasted_iota(I32, (gp, gl), 1) // SSM_STATE))
    for q in range(N_GROUPS // 2):
        j, p = divmod(q, 8)
        rows, cols = slice(gp * p, gp * (p + 1)), slice(gl * p, gl * (p + 1))
        bm_ref[j, rows, cols] = jnp.where(keep_b, br_ref[q], 0.0)
        bm_ref[j, rows, CH_ST + gl * p:CH_ST + gl * (p + 1)] = jnp.where(keep_b, bi_ref[q], 0.0)
    oct_rows = 8 * SSM_STATE
    keep_c = ((lax.broadcasted_iota(I32, (oct_rows, gl), 0) // SSM_STATE)
              == (lax.broadcasted_iota(I32, (oct_rows, gl), 1) // SSM_GROUP))
    for o in range(N_GROUPS // 8):
        j, half = divmod(o, 2)
        rows, cols = slice(oct_rows * half, oct_rows * (half + 1)), slice(gl * half, gl * (half + 1))
        cm_ref[j, rows, cols] = jnp.where(keep_c, cr_ref[o], 0.0)
        cm_ref[j, CH_ST + oct_rows * half:CH_ST + oct_rows * (half + 1), cols] = jnp.where(keep_c, -ci_ref[o], 0.0)
    bm16_ref[...] = bm_ref[...].astype(BF16)
    cm16_ref[...] = cm_ref[...].astype(BF16)


def _blockdiag(bb_r, bb_i, c_re, c_im):
    pair = lambda m: jnp.tile(m, (1, 1, 2)).reshape(N_GROUPS // 2, 2 * SSM_GROUP, LANE)
    octet = lambda c: jnp.tile(c.transpose(0, 2, 1), (1, 1, 8)).reshape(N_GROUPS // 8, 8 * SSM_STATE, LANE)
    bshape, cshape = (N_CHUNK, CH_IN, 2 * CH_ST), (N_CHUNK, 2 * CH_ST, CH_IN)
    return pl.pallas_call(
        _blockdiag_kernel,
        out_shape=(jax.ShapeDtypeStruct(bshape, F32), jax.ShapeDtypeStruct(bshape, BF16),
                   jax.ShapeDtypeStruct(cshape, F32), jax.ShapeDtypeStruct(cshape, BF16)),
        compiler_params=pltpu.CompilerParams(vmem_limit_bytes=VMEM_LIMIT), name="ssm_blockdiag",
    )(pair(bb_r), pair(bb_i), octet(c_re), octet(c_im))


def _load_weight_bf16(w_hbm, w16, stage, sem):
    n = w_hbm.shape[1] // CAST_COLS

    def chunk(c):
        return pltpu.make_async_copy(w_hbm.at[:, pl.ds(c * CAST_COLS, CAST_COLS)], stage.at[c % 2], sem.at[c % 2])

    chunk(0).start()
    for c in range(n):
        if c + 1 < n:
            chunk(c + 1).start()
        chunk(c).wait()
        w16[:, c * CAST_COLS:(c + 1) * CAST_COLS] = stage[c % 2].astype(BF16)


def _inproj_kernel(x_ref, g_ref, inv_ref, w_hbm, u_ref, q_ref, k_ref, v_ref, kn_ref, vn_ref, w_ref, stage, sem):
    kb = pl.program_id(1)

    @pl.when((pl.program_id(0) == 0) & (kb == 0))
    def _():
        _load_weight_bf16(w_hbm, w_ref, stage, sem)

    h = _rms(x_ref[0].reshape(TM, D_MODEL), g_ref[...]).astype(BF16)
    rid = lax.broadcasted_iota(I32, (TM, 1), 0)
    pos = ((rid >> TK_SHIFT) * SEG_LEN + kb * TK + (rid & (TK - 1))).astype(F32)
    cos, sin = _rope_tables(pos, inv_ref[...])
    w = SSM_WIDTH
    u = jnp.dot(h, w_ref[:, 0:w], preferred_element_type=F32)
    for s in range(N_SEG):
        for c in range(w // LANE):
            u_ref[0, c, pl.ds(s, TK, stride=N_SEG), :] = u[s * TK:(s + 1) * TK, c * LANE:(c + 1) * LANE]
    q_ref[0] = _rope_heads(jnp.dot(h, w_ref[:, w:2 * w], preferred_element_type=F32), cos, sin).reshape(N_SEG, TK, w)
    k = _rope_heads(jnp.dot(h, w_ref[:, 2 * w:3 * w], preferred_element_type=F32), cos, sin)
    v = jnp.dot(h, w_ref[:, 3 * w:4 * w], preferred_element_type=F32)
    k_ref[0] = k.reshape(N_SEG, TK, w)
    v_ref[0] = v.reshape(N_SEG, TK, w)
    kn_ref[0] = k[TM // 2:].reshape(N_SEG // 2, TK, w)
    vn_ref[0] = v[TM // 2:].reshape(N_SEG // 2, TK, w)


def _inproj(x4, g, inv, w):
    const = lambda b, k: (0, 0)
    blk4 = lambda nseg, width: pl.BlockSpec((1, nseg, TK, width), lambda b, k: (b, 0, k, 0))
    qkv = jax.ShapeDtypeStruct((BATCH, N_SEG, SEG_LEN, ATTN_WIDTH), F32)
    kvn = jax.ShapeDtypeStruct((BATCH, N_SEG // 2, SEG_LEN, ATTN_WIDTH), F32)
    return pl.pallas_call(
        _inproj_kernel, grid=(BATCH, SEG_LEN // TK),
        in_specs=[blk4(N_SEG, D_MODEL), pl.BlockSpec((1, D_MODEL), const),
                  pl.BlockSpec((1, HEAD_DIM), const), pl.BlockSpec(memory_space=pl.ANY)],
        out_specs=[pl.BlockSpec((1, SSM_WIDTH // LANE, TM, LANE), lambda b, k: (b, 0, k, 0)),
                   blk4(N_SEG, ATTN_WIDTH), blk4(N_SEG, ATTN_WIDTH), blk4(N_SEG, ATTN_WIDTH),
                   blk4(N_SEG // 2, ATTN_WIDTH), blk4(N_SEG // 2, ATTN_WIDTH)],
        out_shape=(jax.ShapeDtypeStruct((BATCH, SSM_WIDTH // LANE, SEQ, LANE), F32), qkv, qkv, qkv, kvn, kvn),
        scratch_shapes=[pltpu.VMEM((D_MODEL, D_IN), BF16), pltpu.VMEM((2, D_MODEL, CAST_COLS), F32),
                        pltpu.SemaphoreType.DMA((2,))],
        compiler_params=_cparams(("arbitrary",) * 2), name="inproj",
    )(x4, g, inv, w)


def _ssm_scan_kernel(*refs, final):
    if final:
        (u_ref, bm_ref, lr_ref, li_ref, e_ref, lsr_ref, lsi_ref, cm_ref, d_ref,
         z_ref, xend_ref, bu_scr, xr_scr, xi_scr) = refs
    else:
        u_ref, bm_ref, lr_ref, li_ref, xend_ref, bu_scr, xr_scr, xi_scr = refs
    kb = pl.program_id(2)

    @pl.when(kb == 0)
    def _():
        if final:
            lsr, lsi = lsr_ref[0:1, :], lsi_ref[0:1, :]
            seg_id = lax.broadcasted_iota(I32, (N_SEG, CH_ST), 0)
            pr = pi = jnp.zeros((1, CH_ST), F32)
            xr0 = xi0 = jnp.zeros((N_SEG, CH_ST), F32)
            for s in range(1, N_SEG):
                er = e_ref[0, 0, s - 1:s, 0:CH_ST]
                ei = e_ref[0, 0, s - 1:s, CH_ST:2 * CH_ST]
                pr, pi = er + lsr * pr - lsi * pi, ei + lsr * pi + lsi * pr
                xr0 = jnp.where(seg_id == s, pr, xr0)
                xi0 = jnp.where(seg_id == s, pi, xi0)
            xr_scr[...] = xr0
            xi_scr[...] = xi0
        else:
            xr_scr[...] = jnp.zeros((N_SEG, CH_ST), F32)
            xi_scr[...] = jnp.zeros((N_SEG, CH_ST), F32)

    u = jnp.concatenate([u_ref[0, c] for c in range(CH_IN // LANE)], axis=1)
    bu_scr[...] = jnp.dot(u.astype(BF16), bm_ref[0], preferred_element_type=F32)
    lr, li = lr_ref[...], li_ref[...]

    def step(k, carry):
        xr, xi = carry
        r0 = pl.multiple_of(k * N_SEG, N_SEG)
        bur = bu_scr[pl.ds(r0, N_SEG), 0:CH_ST]
        bui = bu_scr[pl.ds(r0, N_SEG), CH_ST:2 * CH_ST]
        nr = lr * xr - li * xi + bur
        ni = lr * xi + li * xr + bui
        if final:
            bu_scr[pl.ds(r0, N_SEG), 0:CH_ST] = nr
            bu_scr[pl.ds(r0, N_SEG), CH_ST:2 * CH_ST] = ni
        return nr, ni

    xr, xi = lax.fori_loop(0, SSM_TK, step, (xr_scr[...], xi_scr[...]), unroll=4)
    xr_scr[...] = xr
    xi_scr[...] = xi
    if final:
        y = jnp.dot(bu_scr[...].astype(BF16), cm_ref[0], preferred_element_type=F32)
        z = _gelu_tanh(y + d_ref[...] * u)
        for c in range(CH_IN // LANE):
            z_ref[0, c] = z[:, c * LANE:(c + 1) * LANE]

    @pl.when(kb == pl.num_programs(2) - 1)
    def _():
        xend_ref[0, 0] = jnp.concatenate([xr, xi], axis=1)


def _ssm_scan(u_perm, bm, lr8, li8, final, e=None, lsr8=None, lsi8=None, cm=None, dskip=None):
    rows = SSM_TK * N_SEG
    nkb = SEG_LEN // SSM_TK
    in_specs = [pl.BlockSpec((1, CH_IN // LANE, rows, LANE), lambda b, j, k: (b, j, k, 0)),
                pl.BlockSpec((1, CH_IN, 2 * CH_ST), lambda b, j, k: (j, 0, 0)),
                pl.BlockSpec((N_SEG, CH_ST), lambda b, j, k: (0, j)),
                pl.BlockSpec((N_SEG, CH_ST), lambda b, j, k: (0, j))]
    args = [u_perm, bm, lr8, li8]
    xend_spec = pl.BlockSpec((1, 1, N_SEG, 2 * CH_ST), lambda b, j, k: (b, j, 0, 0))
    xend_shape = jax.ShapeDtypeStruct((BATCH, N_CHUNK, N_SEG, 2 * CH_ST), F32)
    if final:
        in_specs += [xend_spec,
                     pl.BlockSpec((N_SEG, CH_ST), lambda b, j, k: (0, j)),
                     pl.BlockSpec((N_SEG, CH_ST), lambda b, j, k: (0, j)),
                     pl.BlockSpec((1, 2 * CH_ST, CH_IN), lambda b, j, k: (j, 0, 0)),
                     pl.BlockSpec((1, CH_IN), lambda b, j, k: (0, j))]
        args += [e, lsr8, lsi8, cm, dskip]
        out_specs = [pl.BlockSpec((1, CH_IN // LANE, rows, LANE), lambda b, j, k: (b, j, k, 0)), xend_spec]
        out_shape = (jax.ShapeDtypeStruct((BATCH, SSM_WIDTH // LANE, SEQ, LANE), F32), xend_shape)
    else:
        out_specs = xend_spec
        out_shape = xend_shape
    return pl.pallas_call(
        functools.partial(_ssm_scan_kernel, final=final), grid=(BATCH, N_CHUNK, nkb),
        in_specs=in_specs, out_specs=out_specs, out_shape=out_shape,
        scratch_shapes=[pltpu.VMEM((rows, 2 * CH_ST), F32),
                        pltpu.VMEM((N_SEG, CH_ST), F32), pltpu.VMEM((N_SEG, CH_ST), F32)],
        compiler_params=_cparams(("arbitrary",) * 3),
        name="ssm_scan_final" if final else "ssm_scan_ends",
    )(*args)


def _attn_kernel(q_ref, k_ref, v_ref, o_ref, o_scr, lse_scr, bias_scr):
    scale2 = HEAD_DIM ** -0.5 * LOG2_E
    qi = lax.broadcasted_iota(I32, (BAND, 2 * BAND), 0)
    kj = lax.broadcasted_iota(I32, (BAND, 2 * BAND), 1)
    cur_ok = (kj >= BAND) & (kj - BAND <= qi)
    prev_ok = (kj < BAND) & (kj >= qi)
    bias_scr[0] = jnp.where(cur_ok, 0.0, NEG)
    bias_scr[1] = jnp.where(cur_ok | prev_ok, 0.0, NEG)
    nt = (((1,), (1,)), ((), ()))
    for bi, (_, d) in enumerate(BRANCHES):
        nblk = SEQ // (BAND * d)
        shift = nblk.bit_length() - 1

        def tile(i, c, bi=bi, d=d, nblk=nblk, shift=shift):
            r = i >> shift
            ib = i & (nblk - 1)
            start = r + ib * (BAND * d)
            pstart = jnp.maximum(start - BAND * d, r)
            cur = pl.ds(start, BAND, stride=d)
            prev = pl.ds(pstart, BAND, stride=d)
            q = (q_ref[0, cur, :] * scale2).astype(BF16)
            k2 = jnp.concatenate([k_ref[0, prev, :], k_ref[0, cur, :]], axis=0).astype(BF16)
            v2 = jnp.concatenate([v_ref[0, prev, :], v_ref[0, cur, :]], axis=0).astype(BF16)
            s = lax.dot_general(q, k2, nt, preferred_element_type=F32) + bias_scr[jnp.minimum(ib, 1)]
            m = jnp.max(s, axis=-1, keepdims=True)
            p = jnp.exp2(s - m)
            l = jnp.sum(p, axis=-1, keepdims=True)
            o = jnp.dot(p.astype(BF16), v2, preferred_element_type=F32) * (1.0 / l)
            o_scr[bi, cur, :] = o
            lse_scr[bi, cur, :] = jnp.broadcast_to(m + jnp.log2(l), (BAND, HEAD_DIM))
            return c

        def softmax_pv(s, v, rows_out, bi=bi):
            m = jnp.max(s, axis=-1, keepdims=True)
            p = jnp.exp2(s - m)
            l = jnp.sum(p, axis=-1, keepdims=True)
            o_scr[bi, rows_out, :] = jnp.dot(p.astype(BF16), v, preferred_element_type=F32) * (1.0 / l)
            lse_scr[bi, rows_out, :] = jnp.broadcast_to(m + jnp.log2(l), (BAND, HEAD_DIM))

        def residue(r, c, d=d):
            sub = pl.ds(r, 2 * BAND, stride=d)
            q = (q_ref[0, sub, :] * scale2).astype(BF16)
            k = k_ref[0, sub, :].astype(BF16)
            v = v_ref[0, sub, :].astype(BF16)
            s0 = lax.dot_general(q[:BAND], k[:BAND], nt, preferred_element_type=F32) + bias_scr[1, :, BAND:]
            softmax_pv(s0, v[:BAND], pl.ds(r, BAND, stride=d))
            s1 = lax.dot_general(q[BAND:], k, nt, preferred_element_type=F32) + bias_scr[1]
            softmax_pv(s1, v, pl.ds(r + BAND * d, BAND, stride=d))
            return c

        if nblk == 2:
            lax.fori_loop(0, d, residue, 0, unroll=16)
        else:
            lax.fori_loop(0, SEQ // BAND, tile, 0, unroll=32)

    rows = 512

    def merge(i, c):
        sl = pl.ds(pl.multiple_of(i * rows, rows), rows)
        l0, l1, l2 = lse_scr[0, sl, :], lse_scr[1, sl, :], lse_scr[2, sl, :]
        m = jnp.maximum(jnp.maximum(l0, l1), l2)
        e0, e1, e2 = jnp.exp2(l0 - m), jnp.exp2(l1 - m), jnp.exp2(l2 - m)
        o_ref[0, sl, :] = ((e0 * o_scr[0, sl, :] + e1 * o_scr[1, sl, :] + e2 * o_scr[2, sl, :])
                           * (1.0 / (e0 + e1 + e2)))
        return c

    lax.fori_loop(0, SEQ // rows, merge, 0)


def _attention(q, k, v):
    spec = pl.BlockSpec((1, SEQ, HEAD_DIM), lambda b, h: (b, 0, h))
    return pl.pallas_call(
        _attn_kernel, grid=(BATCH, N_HEADS), in_specs=[spec, spec, spec], out_specs=spec,
        out_shape=jax.ShapeDtypeStruct((BATCH, SEQ, ATTN_WIDTH), F32),
        scratch_shapes=[pltpu.VMEM((3, SEQ, HEAD_DIM), F32), pltpu.VMEM((3, SEQ, HEAD_DIM), F32),
                        pltpu.VMEM((2, BAND, 2 * BAND), F32)],
        compiler_params=_cparams(("arbitrary",) * 2), name="dilated_attn",
    )(q, k, v)


def _outproj_kernel(x_ref, z_ref, o_ref, wglu_hbm, gs_ref, ga_ref, w_hbm, gf_ref, wr_ref, br_ref, xm_ref, lg_ref,
                    wglu_ref, w_ref, stage, sem):
    is_prompt = pl.program_id(0) < T_PROMPT // TM

    @pl.when(pl.program_id(0) == 0)
    def _():
        _load_weight_bf16(wglu_hbm, wglu_ref, stage.at[:, 0:SSM_WIDTH], sem)
        _load_weight_bf16(w_hbm, w_ref, stage, sem)

    @pl.when(is_prompt)
    def _():
        z = jnp.concatenate(
            [jnp.concatenate([z_ref[0, c, pl.ds(s, TK, stride=N_SEG), :] for c in range(SSM_WIDTH // LANE)], axis=1)
             for s in range(N_SEG)], axis=0)
        zz = z * _sigmoid(jnp.dot(z.astype(BF16), wglu_ref[...], preferred_element_type=F32))
        ms = _rms(zz, gs_ref[...]).astype(BF16)
        ma = _rms(o_ref[0].reshape(TM, ATTN_WIDTH), ga_ref[...]).astype(BF16)
        xm = (x_ref[0].reshape(TM, D_MODEL) + jnp.dot(ms, w_ref[0:SSM_WIDTH, :], preferred_element_type=F32)
              + jnp.dot(ma, w_ref[SSM_WIDTH:, :], preferred_element_type=F32))
        xm_ref[...] = xm
        h = _rms(xm, gf_ref[...]).astype(BF16)
        lg_ref[...] = jnp.dot(h, wr_ref[...], preferred_element_type=F32) + br_ref[...]

    @pl.when(jnp.logical_not(is_prompt))
    def _():
        xm_ref[...] = jnp.zeros((TM, D_MODEL), F32)
        lg_ref[...] = jnp.zeros((TM, 128), F32)


def _outproj(x4, z_slab, o_attn4, wglu, g_ssm, g_attn, w, g_ffn, wr_bf16, br):
    nkb = SEG_LEN // TK

    def split(i):
        i = jnp.minimum(i, T_PROMPT // TM - 1)
        return i // nkb, i % nkb

    def blk4(i):
        b, k = split(i)
        return (b, 0, k, 0)

    row = lambda i: (i, 0)
    const = lambda i: (0, 0)
    return pl.pallas_call(
        _outproj_kernel, grid=(N_TOK_BLOCKS,),
        in_specs=[pl.BlockSpec((1, N_SEG, TK, D_MODEL), blk4),
                  pl.BlockSpec((1, SSM_WIDTH // LANE, TM, LANE), blk4),
                  pl.BlockSpec((1, N_SEG, TK, ATTN_WIDTH), blk4),
                  pl.BlockSpec(memory_space=pl.ANY),
                  pl.BlockSpec((1, SSM_WIDTH), const),
                  pl.BlockSpec((1, ATTN_WIDTH), const),
                  pl.BlockSpec(memory_space=pl.ANY),
                  pl.BlockSpec((1, D_MODEL), const),
                  pl.BlockSpec((D_MODEL, 128), const),
                  pl.BlockSpec((1, 128), const)],
        out_specs=[pl.BlockSpec((TM, D_MODEL), row), pl.BlockSpec((TM, 128), row)],
        out_shape=(jax.ShapeDtypeStruct((T_PAD, D_MODEL), F32), jax.ShapeDtypeStruct((T_PAD, 128), F32)),
        scratch_shapes=[pltpu.VMEM((SSM_WIDTH, SSM_WIDTH), BF16), pltpu.VMEM((D_MODEL, D_MODEL), BF16),
                        pltpu.VMEM((2, D_MODEL, CAST_COLS), F32), pltpu.SemaphoreType.DMA((2,))],
        compiler_params=_cparams(("arbitrary",)), name="outproj",
    )(x4, z_slab, o_attn4, wglu, g_ssm, g_attn, w, g_ffn, wr_bf16, br)


def _s_inproj_kernel(x_ref, g_ref, w_ref, o_ref):
    h = _rms(x_ref[...], g_ref[...])
    o_ref[...] = jnp.dot(h, w_ref[...], preferred_element_type=F32, precision=HIGHEST)


def _s_inproj(x, g, w):
    tn = 1024
    return pl.pallas_call(
        _s_inproj_kernel, grid=(D_IN // tn,),
        in_specs=[pl.BlockSpec((DEC_BATCH, D_MODEL), lambda j: (0, 0)),
                  pl.BlockSpec((1, D_MODEL), lambda j: (0, 0)),
                  pl.BlockSpec((D_MODEL, tn), lambda j: (0, j))],
        out_specs=pl.BlockSpec((DEC_BATCH, tn), lambda j: (0, j)),
        out_shape=jax.ShapeDtypeStruct((DEC_BATCH, D_IN), F32),
        compiler_params=_cparams(("arbitrary",)), name="s_inproj",
    )(x, g, w)


def _s_ssm_kernel(u_ref, sr_ref, si_ref, lr_ref, li_ref, bm_ref, cm_ref, d_ref, z_ref, nr_ref, ni_ref):
    u = u_ref[...]
    bu = jnp.dot(u, bm_ref[0], preferred_element_type=F32, precision=HIGHEST)
    lr, li = lr_ref[...], li_ref[...]
    xr, xi = sr_ref[...], si_ref[...]
    nr = lr * xr - li * xi + bu[:, :CH_ST]
    ni = lr * xi + li * xr + bu[:, CH_ST:]
    nr_ref[...] = nr
    ni_ref[...] = ni
    y = jnp.dot(jnp.concatenate([nr, ni], axis=1), cm_ref[0], preferred_element_type=F32, precision=HIGHEST)
    z_ref[...] = _gelu_tanh(y + d_ref[...] * u)


def _s_ssm(u, sr, si, lr1, li1, bm, cm, dskip):
    st = jax.ShapeDtypeStruct((DEC_BATCH, N_GROUPS * SSM_STATE), F32)
    return pl.pallas_call(
        _s_ssm_kernel, grid=(N_CHUNK,),
        in_specs=[pl.BlockSpec((DEC_BATCH, CH_IN), lambda j: (0, j)),
                  pl.BlockSpec((DEC_BATCH, CH_ST), lambda j: (0, j)),
                  pl.BlockSpec((DEC_BATCH, CH_ST), lambda j: (0, j)),
                  pl.BlockSpec((1, CH_ST), lambda j: (0, j)),
                  pl.BlockSpec((1, CH_ST), lambda j: (0, j)),
                  pl.BlockSpec((1, CH_IN, 2 * CH_ST), lambda j: (j, 0, 0)),
                  pl.BlockSpec((1, 2 * CH_ST, CH_IN), lambda j: (j, 0, 0)),
                  pl.BlockSpec((1, CH_IN), lambda j: (0, j))],
        out_specs=[pl.BlockSpec((DEC_BATCH, CH_IN), lambda j: (0, j)),
                   pl.BlockSpec((DEC_BATCH, CH_ST), lambda j: (0, j)),
                   pl.BlockSpec((DEC_BATCH, CH_ST), lambda j: (0, j))],
        out_shape=(jax.ShapeDtypeStruct((DEC_BATCH, SSM_WIDTH), F32), st, st),
        compiler_params=_cparams(("arbitrary",)), name="s_ssm",
    )(u, sr, si, lr1, li1, bm, cm, dskip)


def _s_glu_kernel(z_ref, w_ref, g_ref, o_ref):
    z = z_ref[...]
    zz = z * _sigmoid(jnp.dot(z, w_ref[...], preferred_element_type=F32, precision=HIGHEST))
    o_ref[...] = _rms(zz, g_ref[...])


def _s_glu(z, w, g):
    return pl.pallas_call(
        _s_glu_kernel, out_shape=jax.ShapeDtypeStruct((DEC_BATCH, SSM_WIDTH), F32),
        compiler_params=pltpu.CompilerParams(vmem_limit_bytes=VMEM_LIMIT), name="s_glu",
    )(z, w, g)


def _s_attn_kernel(q_ref, k_ref, v_ref, inv_ref, k1_ref, v1_ref, k4_ref, v4_ref, k16_ref, v16_ref,
                   o_ref, kr_ref):
    scale = HEAD_DIM ** -0.5
    pos = jnp.full((N_HEADS, 1), float(PAST_LEN), F32)
    cos, sin = _rope_tables(pos, inv_ref[...])
    q = q_ref[...] * cos + pltpu.roll(q_ref[...], HEAD_DIM // 2, 1) * sin
    kn = k_ref[...] * cos + pltpu.roll(k_ref[...], HEAD_DIM // 2, 1) * sin
    vn = v_ref[...]
    kr_ref[...] = kn
    s0 = jnp.sum(q * kn, axis=-1, keepdims=True) * scale
    outs, lses = [], []
    for kc_ref, vc_ref in ((k1_ref, v1_ref), (k4_ref, v4_ref), (k16_ref, v16_ref)):
        s = jnp.sum(kc_ref[...] * q[None], axis=-1, keepdims=True) * scale
        m = jnp.maximum(jnp.max(s, axis=0), s0)
        p = jnp.exp(s - m[None])
        p0 = jnp.exp(s0 - m)
        l = jnp.sum(p, axis=0) + p0
        outs.append((jnp.sum(p * vc_ref[...], axis=0) + p0 * vn) / l)
        lses.append(m + jnp.log(l))
    mm = jnp.maximum(jnp.maximum(lses[0], lses[1]), lses[2])
    es = [jnp.exp(x - mm) for x in lses]
    o_ref[...] = (es[0] * outs[0] + es[1] * outs[1] + es[2] * outs[2]) / (es[0] + es[1] + es[2])


def _s_attention(q, k, v, inv, cache_k, cache_v):
    hd = (N_HEADS, HEAD_DIM)
    tok = pl.BlockSpec((None,) + hd, lambda b: (b, 0, 0))
    args, specs = [], []
    for _, d in BRANCHES:
        nrow = CACHE_LEN // d
        last = nrow // BAND - 1
        spec = pl.BlockSpec((None, BAND, None) + hd, lambda b, last=last: (b, last, 0, 0, 0))
        for c in (cache_k, cache_v):
            args.append(c.reshape(DEC_BATCH, nrow, d, *hd))
            specs.append(spec)
    out = jax.ShapeDtypeStruct((DEC_BATCH,) + hd, F32)
    return pl.pallas_call(
        _s_attn_kernel, grid=(DEC_BATCH,),
        in_specs=[tok, tok, tok, pl.BlockSpec((1, HEAD_DIM), lambda b: (0, 0))] + specs,
        out_specs=[tok, tok], out_shape=(out, out),
        compiler_params=_cparams(("arbitrary",)), name="s_attn",
    )(q, k, v, inv, *args)


def _s_outproj_kernel(xa_ref, lga_ref, x_ref, ms_ref, o_ref, ga_ref, w_ref, gf_ref, wr_ref, br_ref,
                      xm_ref, lg_ref):
    del xa_ref, lga_ref
    mix = jnp.concatenate([ms_ref[...], _rms(o_ref[...], ga_ref[...])], axis=1)
    xm = x_ref[...] + jnp.dot(mix, w_ref[...], preferred_element_type=F32, precision=HIGHEST)
    h = _rms(xm, gf_ref[...])
    lg = jnp.dot(h, wr_ref[...], preferred_element_type=F32, precision=HIGHEST) + br_ref[...]
    pad = TM - DEC_BATCH
    xm_ref[...] = jnp.concatenate([xm, jnp.zeros((pad, D_MODEL), F32)], axis=0)
    lg_ref[...] = jnp.concatenate([lg, jnp.zeros((pad, 128), F32)], axis=0)


def _s_outproj(xm_all, lg_all, x, mix_ssm, o_attn, g_attn, w, g_ffn, wr, br):
    last = lambda i: (N_TOK_BLOCKS - 1, 0)
    full = lambda shape: pl.BlockSpec(shape, lambda i: (0, 0))
    any_spec = pl.BlockSpec(memory_space=pl.ANY)
    return pl.pallas_call(
        _s_outproj_kernel, grid=(1,),
        in_specs=[any_spec, any_spec,
                  full((DEC_BATCH, D_MODEL)), full((DEC_BATCH, SSM_WIDTH)), full((DEC_BATCH, ATTN_WIDTH)),
                  full((1, ATTN_WIDTH)), full((D_MODEL, D_MODEL)), full((1, D_MODEL)),
                  full((D_MODEL, 128)), full((1, 128))],
        out_specs=[pl.BlockSpec((TM, D_MODEL), last), pl.BlockSpec((TM, 128), last)],
        out_shape=(jax.ShapeDtypeStruct(xm_all.shape, F32), jax.ShapeDtypeStruct(lg_all.shape, F32)),
        input_output_aliases={0: 0, 1: 1},
        compiler_params=_cparams(("arbitrary",)), name="s_outproj",
    )(xm_all, lg_all, x, mix_ssm, o_attn, g_attn, w, g_ffn, wr, br)


def _route_kernel(lg_ref, ri_ref, rw_ref, cnt_ref, carry):
    i = pl.program_id(0)

    @pl.when(i == 0)
    def _():
        carry[...] = jnp.zeros((1, 128), F32)

    x = lg_ref[...]
    lane = lax.broadcasted_iota(I32, (ROUTE_ROWS, 128), 1)
    row = lax.broadcasted_iota(I32, (ROUTE_ROWS, 128), 0) + i * ROUTE_ROWS
    valid = row < T_REAL
    big = jnp.int32(1 << 20)
    gmask = lane < N_EXPERT_GROUPS
    lgm = jnp.where(gmask, x, NEG)
    m = jnp.max(lgm, axis=-1, keepdims=True)
    gate1 = 1.0 / jnp.sum(jnp.where(gmask, jnp.exp(lgm - m), 0.0), axis=-1, keepdims=True)
    grp = jnp.min(jnp.where(gmask & (lgm == m), lane, big), axis=-1, keepdims=True)
    lo = N_EXPERT_GROUPS + EXPERTS_PER_GROUP * grp
    emask = (lane >= lo) & (lane < lo + EXPERTS_PER_GROUP)
    le = jnp.where(emask, x, NEG)
    t1 = jnp.max(le, axis=-1, keepdims=True)
    i1 = jnp.min(jnp.where(emask & (le == t1), lane, big), axis=-1, keepdims=True)
    emask2 = emask & (lane != i1)
    le2 = jnp.where(emask2, x, NEG)
    t2 = jnp.max(le2, axis=-1, keepdims=True)
    i2 = jnp.min(jnp.where(emask2 & (le2 == t2), lane, big), axis=-1, keepdims=True)
    e21 = jnp.exp(t2 - t1)
    w1 = gate1 / (1.0 + e21)
    w2 = gate1 * e21 / (1.0 + e21)
    eid1, eid2 = i1 - N_EXPERT_GROUPS, i2 - N_EXPERT_GROUPS
    oh1 = jnp.where(valid & (lane == eid1), 1.0, 0.0)
    oh2 = jnp.where(valid & (lane == eid2), 1.0, 0.0)
    a = oh1 + oh2
    rr = lax.broadcasted_iota(I32, (ROUTE_ROWS, ROUTE_ROWS), 0)
    cc = lax.broadcasted_iota(I32, (ROUTE_ROWS, ROUTE_ROWS), 1)
    before = jnp.where(cc < rr, 1.0, 0.0).astype(BF16)
    pre = jnp.dot(before, a.astype(BF16), preferred_element_type=F32) + carry[...]
    rank1 = jnp.sum(oh1 * pre, axis=-1, keepdims=True).astype(I32)
    rank2 = jnp.sum(oh2 * pre, axis=-1, keepdims=True).astype(I32)
    carry[...] = carry[...] + jnp.sum(a, axis=0, keepdims=True)
    cnt_ref[...] = jnp.broadcast_to(carry[...], (8, 128))
    zi = jnp.zeros((ROUTE_ROWS, 128), I32)
    ri = jnp.where(lane == 0, eid1, jnp.where(lane == 1, eid2, jnp.where(lane == 2, rank1, jnp.where(lane == 3, rank2, zi))))
    ri_ref[...] = jnp.where(valid, ri, zi)
    rw = jnp.where(lane == 0, w1, jnp.where(lane == 1, w2, 0.0))
    rw_ref[...] = jnp.where(valid, rw, 0.0)


def _route(logits):
    blk = pl.BlockSpec((ROUTE_ROWS, 128), lambda i: (i, 0))
    return pl.pallas_call(
        _route_kernel, grid=(T_PAD // ROUTE_ROWS,), in_specs=[blk],
        out_specs=[blk, blk, pl.BlockSpec((8, 128), lambda i: (0, 0))],
        out_shape=(jax.ShapeDtypeStruct((T_PAD, 128), I32), jax.ShapeDtypeStruct((T_PAD, 128), F32),
                   jax.ShapeDtypeStruct((8, 128), F32)),
        scratch_shapes=[pltpu.VMEM((1, 128), F32)],
        compiler_params=_cparams(("arbitrary",)), name="route",
    )(logits)


def _row_copy(src, s, dst, d, sem):
    return pltpu.make_async_copy(src.at[pl.ds(s, 1)], dst.at[pl.ds(d, 1)], sem)


def _invert_kernel(poff_ref, ri_ref, dest_ref, rt_ref):
    i = pl.program_id(0)

    @pl.when(i == 0)
    def _():
        def clear(p, c):
            rt_ref[p] = 0
            return c
        lax.fori_loop(0, N_MOE_ROWS, clear, 0, unroll=32)

    def put(r, c):
        t = i * TM + r
        for s in range(2):
            d = poff_ref[ri_ref[0, 0, 4 * r + s]] + ri_ref[0, 0, 4 * r + 2 + s]
            dest_ref[0, 0, 2 * r + s] = d
            rt_ref[d] = t
        return c

    def blank(r, c):
        dest_ref[0, 0, 2 * r] = 0
        dest_ref[0, 0, 2 * r + 1] = 0
        return c

    @pl.when(i < T_PROMPT // TM)
    def _():
        lax.fori_loop(0, TM, put, 0, unroll=8)

    @pl.when(i == T_PROMPT // TM)
    def _():
        lax.fori_loop(0, DEC_BATCH, put, 0, unroll=8)
        lax.fori_loop(DEC_BATCH, TM, blank, 0, unroll=8)


def _invert(poff, ri4):
    grid_spec = pltpu.PrefetchScalarGridSpec(
        num_scalar_prefetch=1, grid=(N_TOK_BLOCKS,),
        in_specs=[pl.BlockSpec((1, 1, 4 * TM), lambda i, po: (i, 0, 0), memory_space=pltpu.SMEM)],
        out_specs=[pl.BlockSpec((1, 1, 2 * TM), lambda i, po: (i, 0, 0), memory_space=pltpu.SMEM),
                   pl.BlockSpec(memory_space=pltpu.SMEM)])
    return pl.pallas_call(
        _invert_kernel, grid_spec=grid_spec,
        out_shape=(jax.ShapeDtypeStruct((N_TOK_BLOCKS, 1, 2 * TM), I32),
                   jax.ShapeDtypeStruct((N_MOE_ROWS,), I32)),
        compiler_params=_cparams(("arbitrary",)), name="moe_invert",
    )(poff, ri4)


def _expert_kernel(fb_ref, nv_ref, rt_ref, xm_ref, gf_ref, wg_ref, wu_ref, wd_ref, y_ref,
                   xbuf, ybuf, gsem, ysem, wgu_scr, wd_scr):
    e = pl.program_id(0)
    first, end, total = fb_ref[e], fb_ref[e + 1], fb_ref[N_EXPERTS]

    def y_copy(g, slot):
        return pltpu.make_async_copy(ybuf.at[slot], y_ref.at[pl.ds(g * MOE_BLK, MOE_BLK)], ysem.at[slot])

    def gather(g, slot, wait):
        def body(r, c):
            if wait:
                _row_copy(xm_ref, 0, xbuf.at[slot], 0, gsem.at[slot]).wait()
            else:
                _row_copy(xm_ref, rt_ref[g * MOE_BLK + r], xbuf.at[slot], r,
                          gsem.at[slot]).start(priority=GATHER_PRIORITY)
            return c
        lax.fori_loop(0, nv_ref[g], body, 0)

    @pl.when(e == 0)
    def _():
        xbuf[...] = jnp.zeros(xbuf.shape, F32)
        for d in range(GATHER_DEPTH - 1):
            gather(d, d, wait=False)

    @pl.when(end > first)
    def _():
        wgu_scr[:, 0:EXPERT_FF] = wg_ref[0].astype(BF16)
        wgu_scr[:, EXPERT_FF:] = wu_ref[0].astype(BF16)
        wd_scr[...] = wd_ref[0].astype(BF16)

    def block(g, carry):
        slot = g % 2
        xslot = g % GATHER_DEPTH

        gather(g + GATHER_DEPTH - 1, (g + GATHER_DEPTH - 1) % GATHER_DEPTH, wait=False)
        gather(g, xslot, wait=True)
        h = _rms(xbuf[xslot], gf_ref[...]).astype(BF16)
        gu = jnp.dot(h, wgu_scr[...], preferred_element_type=F32)
        gate, up = gu[:, :EXPERT_FF], gu[:, EXPERT_FF:]
        act = (gate * _sigmoid(gate) * up).astype(BF16)
        y = jnp.dot(act, wd_scr[...], preferred_element_type=F32)

        @pl.when(g >= 2)
        def _():
            y_copy(g - 2, slot).wait()
        ybuf[slot] = y
        y_copy(g, slot).start()
        return carry

    lax.fori_loop(first, end, block, 0)

    @pl.when(e == N_EXPERTS - 1)
    def _():
        @pl.when(total >= 2)
        def _():
            y_copy(total - 2, total % 2).wait()

        @pl.when(total >= 1)
        def _():
            y_copy(total - 1, (total - 1) % 2).wait()

        ybuf[0] = jnp.zeros((MOE_BLK, D_MODEL), F32)

        def zero_start(g, c):
            y_copy(g, 0).start()
            return c

        def zero_wait(g, c):
            y_copy(g, 0).wait()
            return c

        lax.fori_loop(total, N_MOE_BLOCKS, zero_start, 0)
        lax.fori_loop(total, N_MOE_BLOCKS, zero_wait, 0)


def _experts(first_blk, nvalid, row_tok, xm_all, g_ffn, w_g, w_u, w_d):
    grid_spec = pltpu.PrefetchScalarGridSpec(
        num_scalar_prefetch=3, grid=(N_EXPERTS,),
        in_specs=[pl.BlockSpec(memory_space=pl.ANY),
                  pl.BlockSpec((1, D_MODEL), lambda e, fb, nv, rt: (0, 0)),
                  pl.BlockSpec((1, D_MODEL, EXPERT_FF), lambda e, fb, nv, rt: (e, 0, 0)),
                  pl.BlockSpec((1, D_MODEL, EXPERT_FF), lambda e, fb, nv, rt: (e, 0, 0)),
                  pl.BlockSpec((1, EXPERT_FF, D_MODEL), lambda e, fb, nv, rt: (e, 0, 0))],
        out_specs=pl.BlockSpec(memory_space=pl.ANY),
        scratch_shapes=[pltpu.VMEM((GATHER_DEPTH, MOE_BLK, D_MODEL), F32), pltpu.VMEM((2, MOE_BLK, D_MODEL), F32),
                        pltpu.SemaphoreType.DMA((GATHER_DEPTH,)), pltpu.SemaphoreType.DMA((2,)),
                        pltpu.VMEM((D_MODEL, 2 * EXPERT_FF), BF16), pltpu.VMEM((EXPERT_FF, D_MODEL), BF16)])
    return pl.pallas_call(
        _expert_kernel, grid_spec=grid_spec,
        out_shape=jax.ShapeDtypeStruct((N_MOE_ROWS, D_MODEL), F32),
        compiler_params=_cparams(("arbitrary",)), name="moe_experts",
    )(first_blk, nvalid, row_tok, xm_all, g_ffn, w_g, w_u, w_d)


def _combine_kernel(dest_ref, destn_ref, rw_ref, xm_ref, g_ref, y_ref, o_ref, ybuf, sems, *, rows, nblk):
    i = pl.program_id(0)
    slot = i % 2

    def gather(d_ref, s, wait):
        def body(r, c):
            for j in range(2):
                cp = _row_copy(y_ref, 0 if wait else d_ref[0, 0, 2 * r + j], ybuf.at[s, j], r, sems.at[s])
                cp.wait() if wait else cp.start(priority=j)
            return c
        lax.fori_loop(0, rows, body, 0, unroll=8)

    def start_all(d_ref, s):
        for r in range(rows):
            for j in range(2):
                _row_copy(y_ref, d_ref[0, 0, 2 * r + j], ybuf.at[s, j], r, sems.at[s]).start(priority=j)

    @pl.when(i == 0)
    def _():
        start_all(dest_ref, 0)

    for nxt_slot in range(2):
        @pl.when((i + 1 < nblk) & (slot == 1 - nxt_slot))
        def _(nxt_slot=nxt_slot):
            start_all(destn_ref, nxt_slot)

    gather(dest_ref, slot, wait=True)
    w = rw_ref[...]
    x = xm_ref[...] + w[:, 0:1] * ybuf[slot, 0] + w[:, 1:2] * ybuf[slot, 1]
    o_ref[...] = _rms(x, g_ref[...]).reshape(o_ref.shape)


def _combine(dest3, rw, xm, g, y_rows, rows, nblk, blk0, out_spec, out_shape):
    smem_blk = lambda f: pl.BlockSpec((1, 1, 2 * rows), f, memory_space=pltpu.SMEM)
    return pl.pallas_call(
        functools.partial(_combine_kernel, rows=rows, nblk=nblk), grid=(nblk,),
        in_specs=[smem_blk(lambda i: (i, 0, 0)),
                  smem_blk(lambda i: (jnp.minimum(i + 1, nblk - 1), 0, 0)),
                  pl.BlockSpec((rows, 128), lambda i: (blk0 + i, 0)),
                  pl.BlockSpec((rows, D_MODEL), lambda i: (blk0 + i, 0)),
                  pl.BlockSpec((1, D_MODEL), lambda i: (0, 0)),
                  pl.BlockSpec(memory_space=pl.ANY)],
        out_specs=out_spec, out_shape=out_shape,
        scratch_shapes=[pltpu.VMEM((2, 2, rows, D_MODEL), F32), pltpu.SemaphoreType.DMA((2,))],
        compiler_params=_cparams(("arbitrary",)), name="moe_combine",
    )(dest3, dest3, rw, xm, g, y_rows)


def kernel(x_prompt, x_sample, cache_k, cache_v, state_ssm_re, state_ssm_im, g_mix_norm, w_in, ssm_a_re, ssm_a_im, ssm_log_dt, ssm_b_re, ssm_b_im, ssm_c_re, ssm_c_im, ssm_d, w_glu, g_ssm_out, g_attn_out, w_out, g_ffn_norm, w_router_group, b_router_group, w_router_expert, b_router_expert, w_expert_gate, w_expert_up, w_expert_down, g_final):
    row = lambda a: a.reshape(1, -1)
    half = HEAD_DIM // 2
    inv = ROPE_THETA ** (-jnp.arange(half, dtype=F32) / half)
    inv = jnp.concatenate([inv, inv]).reshape(1, HEAD_DIM)

    lam_r, lam_i, lams_r, lams_i, bb_r, bb_i = _ssm_params(ssm_a_re[0], ssm_a_im[0], ssm_log_dt[0], ssm_b_re[0], ssm_b_im[0])
    bm, bm16, cm, cm16 = _blockdiag(bb_r, bb_i, ssm_c_re[0], ssm_c_im[0])
    dskip = row(ssm_d[0])
    bcast8 = lambda a: jnp.broadcast_to(a.reshape(1, -1), (N_SEG, N_GROUPS * SSM_STATE))
    lr8, li8, lsr8, lsi8 = bcast8(lam_r), bcast8(lam_i), bcast8(lams_r), bcast8(lams_i)

    wr = jnp.concatenate([w_router_group[0], w_router_expert[0].reshape(D_MODEL, N_EXPERTS)], axis=1)
    wr = jnp.pad(wr, ((0, 0), (0, 128 - wr.shape[1])))
    br = jnp.pad(jnp.concatenate([b_router_group[0], b_router_expert[0].reshape(-1)]), (0, 128 - 36)).reshape(1, 128)

    x4 = x_prompt.reshape(BATCH, N_SEG, SEG_LEN, D_MODEL)
    u_il, q4, k4, v4, k_new, v_new = _inproj(x4, row(g_mix_norm[0]), inv, w_in[0])
    ends = _ssm_scan(u_il, bm16, lr8, li8, final=False)
    z_il, xend = _ssm_scan(u_il, bm16, lr8, li8, final=True, e=ends, lsr8=lsr8, lsi8=lsi8, cm=cm16, dskip=dskip)
    flat = lambda a: a.reshape(BATCH, SEQ, ATTN_WIDTH)
    o_attn = _attention(flat(q4), flat(k4), flat(v4))
    xm_all, lg_all = _outproj(
        x4, z_il, o_attn.reshape(BATCH, N_SEG, SEG_LEN, ATTN_WIDTH), w_glu[0], row(g_ssm_out[0]),
        row(g_attn_out[0]), w_out[0], row(g_ffn_norm[0]), wr.astype(BF16), br)

    xs = x_sample.reshape(DEC_BATCH, D_MODEL)
    proj = _s_inproj(xs, row(g_mix_norm[0]), w_in[0])
    us = proj[:, :SSM_WIDTH]
    qkv = proj[:, SSM_WIDTH:].reshape(DEC_BATCH, 3, N_HEADS, HEAD_DIM)
    zs, ns_r, ns_i = _s_ssm(us, state_ssm_re[0].reshape(DEC_BATCH, -1), state_ssm_im[0].reshape(DEC_BATCH, -1),
                            row(lam_r), row(lam_i), bm, cm, dskip)
    mix_ssm_s = _s_glu(zs, w_glu[0], row(g_ssm_out[0]))
    o_s, k_rot = _s_attention(qkv[:, 0], qkv[:, 1], qkv[:, 2], inv, cache_k[0], cache_v[0])
    xm_all, lg_all = _s_outproj(
        xm_all, lg_all, xs, mix_ssm_s, o_s.reshape(DEC_BATCH, ATTN_WIDTH), row(g_attn_out[0]),
        w_out[0], row(g_ffn_norm[0]), wr, br)

    ri, rw, cnt = _route(lg_all)
    counts = cnt[0, :N_EXPERTS].astype(I32)
    padded = (counts + MOE_BLK - 1) // MOE_BLK * MOE_BLK
    pend = jnp.cumsum(padded)
    poff = pend - padded
    first_blk = jnp.concatenate([poff, pend[-1:]]) // MOE_BLK
    blk_start = jnp.arange(N_MOE_BLOCKS + GATHER_DEPTH, dtype=I32) * MOE_BLK
    block_e = jnp.minimum(jnp.sum(pend[None, :] <= blk_start[:, None], axis=1), N_EXPERTS - 1)
    nvalid = jnp.clip(counts[block_e] - (blk_start - poff[block_e]), 0, MOE_BLK).astype(I32)
    dest3, row_tok = _invert(poff, ri[:, 0:4].reshape(N_TOK_BLOCKS, 1, 4 * TM))
    y_rows = _experts(first_blk, nvalid, row_tok, xm_all, row(g_ffn_norm[0]),
                      w_expert_gate[0], w_expert_up[0], w_expert_down[0])
    gfin = row(g_final)
    nkb = SEG_LEN // TK
    y_prompt = _combine(dest3, rw, xm_all, gfin, y_rows, TM, T_PROMPT // TM, 0,
                        pl.BlockSpec((1, N_SEG, TK, D_MODEL), lambda i: (i // nkb, 0, i % nkb, 0)),
                        jax.ShapeDtypeStruct((BATCH, N_SEG, SEG_LEN, D_MODEL), F32))
    dest_s = dest3[N_TOK_BLOCKS - 1:, :, :2 * DEC_BATCH]
    y_sample = _combine(dest_s, rw, xm_all, gfin, y_rows, DEC_BATCH, 1, T_PROMPT // DEC_BATCH,
                        pl.BlockSpec((DEC_BATCH, D_MODEL), lambda i: (0, 0)),
                        jax.ShapeDtypeStruct((DEC_BATCH, D_MODEL), F32))

    kv_shape = (1, BATCH, CACHE_LEN, N_HEADS, HEAD_DIM)
    st_p = lambda a: a[:, :, N_SEG - 1, :].reshape(1, BATCH, N_GROUPS, SSM_STATE)
    st_s = lambda a: a.reshape(1, DEC_BATCH, N_GROUPS, SSM_STATE)
    kvs = lambda a: a.reshape(1, DEC_BATCH, 1, N_HEADS, HEAD_DIM)
    return (y_prompt.reshape(BATCH, SEQ, D_MODEL), y_sample.reshape(DEC_BATCH, 1, D_MODEL),
            k_new.reshape(kv_shape), v_new.reshape(kv_shape),
            st_p(xend[..., :CH_ST]), st_p(xend[..., CH_ST:]),
            kvs(k_rot), kvs(qkv[:, 2]), st_s(ns_r), st_s(ns_i))
```
